```python
import jax, jax.numpy as jnp
from jax import lax
import numpy as np

D_MODEL = 1024
BATCH = 32
SEQ = 2048
DEPTH = 2
DEC_BATCH = 8
DEC_SEQ = 16
PAST_LEN = 2048

CHUNK = 64
Q_BLOCK = 128
D_CONV = 512
CONV_W = 3
N_HEADS = 8
N_KV_HEADS = 4
HEAD_DIM = 64
GROUP = N_HEADS // N_KV_HEADS
N_IDX_HEADS = 16
IDX_DIM = 64
TOPK_MAX = 256
D_FF = 2816
N_EXPERTS = 8
TOP_E = 2
D_FF_EXPERT = 3584
N_DENSE = (DEPTH + 1) // 2
N_MOE = DEPTH // 2
EPS = 1e-6
IN_SPLITS = (D_CONV, D_CONV, D_CONV,
             N_HEADS * HEAD_DIM, N_KV_HEADS * HEAD_DIM, N_KV_HEADS * HEAD_DIM,
             N_IDX_HEADS * IDX_DIM, IDX_DIM, N_IDX_HEADS,
             D_MODEL, D_MODEL)
D_IN_PROJ = 3 * D_CONV + (N_HEADS + 2 * N_KV_HEADS) * HEAD_DIM + (N_IDX_HEADS + 1) * IDX_DIM + N_IDX_HEADS + 2 * D_MODEL

kernel_name = "hybrid_streaming_conv_dsa_step"


def rmsnorm(x, g):
    x32 = x.astype(jnp.float32)
    y = x32 * lax.rsqrt(jnp.mean(x32 * x32, axis=-1, keepdims=True) + EPS)
    return (y * g.astype(jnp.float32)).astype(x.dtype)


def split_in_proj(p):
    cuts = np.cumsum(np.array(IN_SPLITS))[:-1].tolist()
    return jnp.split(p, cuts, axis=-1)


def topk_count(n_keys):
    return min(TOPK_MAX, n_keys // 4)


def short_conv(u, state, w):
    T = u.shape[1]
    up = jnp.concatenate([state, u], axis=1)
    y = w[0] * up[:, 0:T]
    for j in range(1, CONV_W):
        y = y + w[j] * up[:, j:j + T]
    return y, up[:, -(CONV_W - 1):]


def sparse_attention(q, qi, wi, pos_q, k, v, ki, topk):
    Bn, Q = q.shape[:2]
    L = k.shape[1]
    limit = (pos_q // CHUNK + 1) * CHUNK
    admissible = jnp.arange(L)[None, :] < limit[:, None]
    dots = jnp.einsum('bqhd,bld->bqhl', qi.astype(jnp.float32), ki.astype(jnp.float32)) * (IDX_DIM ** -0.5)
    index_score = jnp.einsum('bqh,bqhl->bql', wi.astype(jnp.float32), jax.nn.relu(dots))
    index_score = jnp.where(admissible[None], index_score, -jnp.inf)
    _, sel = lax.top_k(index_score, topk)
    sel_valid = sel < limit[None, :, None]
    gather_rows = jax.vmap(lambda rows, ids: rows[ids])
    k_sel = gather_rows(k, sel)
    v_sel = gather_rows(v, sel)
    s = jnp.einsum('bqhgd,bqjhd->bqhgj', q, k_sel).astype(jnp.float32) * (HEAD_DIM ** -0.5)
    s = jnp.where(sel_valid[:, :, None, None, :], s, -jnp.inf)
    p = jax.nn.softmax(s, axis=-1).astype(v.dtype)
    o = jnp.einsum('bqhgj,bqjhd->bqhgd', p, v_sel)
    return o.reshape(Bn, Q, N_HEADS * HEAD_DIM)


def prompt_attention(q, qi, wi, k, v, ki):
    Bn, S = q.shape[:2]
    nb = S // Q_BLOCK
    topk = topk_count(S)

    def blocks(a):
        return jnp.moveaxis(a.reshape((Bn, nb, Q_BLOCK) + a.shape[2:]), 1, 0)

    pos = jnp.arange(S, dtype=jnp.int32).reshape(nb, Q_BLOCK)
    out = lax.map(lambda xs: sparse_attention(xs[0], xs[1], xs[2], xs[3], k, v, ki, topk),
                  (blocks(q), blocks(qi), blocks(wi), pos))
    return jnp.moveaxis(out, 0, 1).reshape(Bn, S, N_HEADS * HEAD_DIM)


def token_mixer(h, conv_state, past, w_in, conv_w, w_conv_out, w_attn_out, w_o):
    Bn, T, _ = h.shape
    b_g, c_g, xin, q, k, v, qi, ki, wi, g_conv, g_attn = split_in_proj(h @ w_in)
    if conv_state is None:
        conv_state = jnp.zeros((Bn, CONV_W - 1, D_CONV), h.dtype)
    conv_y, conv_tail = short_conv(c_g * xin, conv_state, conv_w)
    branch_a = (b_g * conv_y) @ w_conv_out
    q = q.reshape(Bn, T, N_KV_HEADS, GROUP, HEAD_DIM)
    k = k.reshape(Bn, T, N_KV_HEADS, HEAD_DIM)
    v = v.reshape(Bn, T, N_KV_HEADS, HEAD_DIM)
    qi = qi.reshape(Bn, T, N_IDX_HEADS, IDX_DIM)
    wi = wi * (N_IDX_HEADS ** -0.5)
    if past is None:
        attn_o = prompt_attention(q, qi, wi, k, v, ki)
    else:
        k_past, v_past, ki_past = past
        P = k_past.shape[1]
        k_all = jnp.concatenate([k_past, k], axis=1)
        v_all = jnp.concatenate([v_past, v], axis=1)
        ki_all = jnp.concatenate([ki_past, ki], axis=1)
        pos_q = P + jnp.arange(T, dtype=jnp.int32)
        attn_o = sparse_attention(q, qi, wi, pos_q, k_all, v_all, ki_all, topk_count(P + T))
    branch_b = attn_o @ w_attn_out
    merged = jax.nn.sigmoid(g_conv) * branch_a + jax.nn.sigmoid(g_attn) * branch_b
    return merged @ w_o, (k, v, ki, conv_tail)


def dense_swiglu(h, w1, w3, w2):
    return (jax.nn.silu(h @ w1) * (h @ w3)) @ w2


def moe_swiglu(h, router_w, router_b, w1, w3, w2):
    logits = (h @ router_w).astype(jnp.float32) + router_b.astype(jnp.float32)
    top_val, top_idx = lax.top_k(logits, TOP_E)
    top_gate = jax.nn.softmax(top_val, axis=-1)
    gates = jnp.sum(jax.nn.one_hot(top_idx, N_EXPERTS, dtype=jnp.float32) * top_gate[..., None], axis=-2)
    out = jnp.zeros_like(h)
    for e in range(N_EXPERTS):
        out = out + gates[..., e:e + 1].astype(h.dtype) * dense_swiglu(h, w1[e], w3[e], w2[e])
    return out


def run_trunk(x, c, caches, params):
    (w_ada, b_ada, norm1_g, w_in, conv_w, w_conv_out, w_attn_out, w_o, norm2_g,
     ffn_w1, ffn_w3, ffn_w2, router_w, router_b, moe_w1, moe_w3, moe_w2, final_g) = params
    cond = jax.nn.silu(c)
    ks, vs, kis, convs = [], [], [], []
    for l in range(DEPTH):
        mod = cond @ w_ada[l] + b_ada[l]
        sh1, sc1, g1, sh2, sc2, g2 = [m[:, None, :] for m in jnp.split(mod, 6, axis=-1)]
        if caches is None:
            conv_state, past = None, None
        else:
            conv_state = caches[3][l]
            past = (caches[0][l], caches[1][l], caches[2][l])
        h = rmsnorm(x, norm1_g[l]) * (1 + sc1) + sh1
        mix, (k, v, ki, tail) = token_mixer(h, conv_state, past, w_in[l], conv_w[l],
                                            w_conv_out[l], w_attn_out[l], w_o[l])
        x = x + g1 * mix
        h = rmsnorm(x, norm2_g[l]) * (1 + sc2) + sh2
        if l % 2 == 0:
            f = dense_swiglu(h, ffn_w1[l // 2], ffn_w3[l // 2], ffn_w2[l // 2])
        else:
            f = moe_swiglu(h, router_w[l // 2], router_b[l // 2], moe_w1[l // 2], moe_w3[l // 2], moe_w2[l // 2])
        x = x + g2 * f
        ks.append(k); vs.append(v); kis.append(ki); convs.append(tail)
    return rmsnorm(x, final_g), jnp.stack(ks), jnp.stack(vs), jnp.stack(kis), jnp.stack(convs)


def setup_inputs(seed: int = 0) -> dict:
    key = jax.random.key(seed)
    ks = jax.random.split(key, 32)
    n = lambda i, shape, s: jax.random.normal(ks[i], shape, jnp.float32) * s
    D = D_MODEL
    return {
        "x_prompt": n(0, (BATCH, SEQ, D), 1.0),
        "x_sample": n(1, (DEC_BATCH, DEC_SEQ, D), 1.0),
        "c_prompt": n(2, (BATCH, D), 1.0),
        "c_sample": n(3, (DEC_BATCH, D), 1.0),
        "cache_k": n(4, (DEPTH, DEC_BATCH, PAST_LEN, N_KV_HEADS, HEAD_DIM), 1.0),
        "cache_v": n(5, (DEPTH, DEC_BATCH, PAST_LEN, N_KV_HEADS, HEAD_DIM), 1.0),
        "cache_idx_k": n(6, (DEPTH, DEC_BATCH, PAST_LEN, IDX_DIM), 1.0),
        "state_conv": n(7, (DEPTH, DEC_BATCH, CONV_W - 1, D_CONV), 1.0),
        "w_ada": n(8, (DEPTH, D, 6 * D), 0.5 * D ** -0.5),
        "b_ada": n(9, (DEPTH, 6 * D), 0.02),
        "norm1_g": 1.0 + n(10, (DEPTH, D), 0.02),
        "w_in": n(11, (DEPTH, D, D_IN_PROJ), D ** -0.5),
        "conv_w": n(12, (DEPTH, CONV_W, D_CONV), CONV_W ** -0.5),
        "w_conv_out": n(13, (DEPTH, D_CONV, D), D_CONV ** -0.5),
        "w_attn_out": n(14, (DEPTH, N_HEADS * HEAD_DIM, D), (N_HEADS * HEAD_DIM) ** -0.5),
        "w_o": n(15, (DEPTH, D, D), D ** -0.5),
        "norm2_g": 1.0 + n(16, (DEPTH, D), 0.02),
        "ffn_w1": n(17, (N_DENSE, D, D_FF), D ** -0.5),
        "ffn_w3": n(18, (N_DENSE, D, D_FF), D ** -0.5),
        "ffn_w2": n(19, (N_DENSE, D_FF, D), D_FF ** -0.5),
        "router_w": n(20, (N_MOE, D, N_EXPERTS), D ** -0.5),
        "router_b": n(21, (N_MOE, N_EXPERTS), 0.01),
        "moe_w1": n(22, (N_MOE, N_EXPERTS, D, D_FF_EXPERT), D ** -0.5),
        "moe_w3": n(23, (N_MOE, N_EXPERTS, D, D_FF_EXPERT), D ** -0.5),
        "moe_w2": n(24, (N_MOE, N_EXPERTS, D_FF_EXPERT, D), D_FF_EXPERT ** -0.5),
        "final_g": 1.0 + n(25, (D,), 0.02),
    }


def reference(x_prompt, x_sample, c_prompt, c_sample, cache_k, cache_v, cache_idx_k, state_conv,
              w_ada, b_ada, norm1_g, w_in, conv_w, w_conv_out, w_attn_out, w_o, norm2_g,
              ffn_w1, ffn_w3, ffn_w2, router_w, router_b, moe_w1, moe_w3, moe_w2, final_g):
    params = (w_ada, b_ada, norm1_g, w_in, conv_w, w_conv_out, w_attn_out, w_o, norm2_g,
              ffn_w1, ffn_w3, ffn_w2, router_w, router_b, moe_w1, moe_w3, moe_w2, final_g)
    y_prompt, new_k_prompt, new_v_prompt, new_idx_k_prompt, new_conv_prompt = run_trunk(
        x_prompt, c_prompt, None, params)
    y_sample, new_k_sample, new_v_sample, new_idx_k_sample, new_conv_sample = run_trunk(
        x_sample, c_sample, (cache_k, cache_v, cache_idx_k, state_conv), params)
    return (y_prompt, y_sample, new_k_prompt, new_v_prompt, new_idx_k_prompt, new_conv_prompt,
            new_k_sample, new_v_sample, new_idx_k_sample, new_conv_sample)
```

```python
import functools

import jax
import jax.numpy as jnp
from jax import lax
from jax.experimental import pallas as pl
from jax.experimental.pallas import tpu as pltpu

F32 = jnp.float32
BF16 = jnp.bfloat16
I32 = jnp.int32

D_MODEL = 1024
D_CONV = 512
CONV_W = 3
N_HEADS = 8
N_KV_HEADS = 4
HEAD_DIM = 64
GROUP = N_HEADS // N_KV_HEADS
N_IDX_HEADS = 16
IDX_DIM = 64
TOPK_MAX = 256
CHUNK = 64
D_FF = 2816
N_EXPERTS = 8
D_FF_EXPERT = 3584
EPS = 1e-6

D_Q = N_HEADS * HEAD_DIM
D_KV = N_KV_HEADS * HEAD_DIM
D_QI = N_IDX_HEADS * IDX_DIM
LANES = 128
C_CONV = 0
C_Q = 3 * D_CONV
C_KV = C_Q + D_Q
C_QI = C_KV + 2 * D_KV
C_KIW = C_QI + D_QI
C_GATE = C_KIW + LANES
D_PACK = C_GATE + 2 * D_MODEL
WI_SCALE = (IDX_DIM ** -0.5) * (N_IDX_HEADS ** -0.5)
Q_SCALE = HEAD_DIM ** -0.5

KEY_BLOCK = 256
FFN_CHUNK = 256
MOE_CHUNK = 512
V7X_VMEM_LIMIT = 56 * 1024 * 1024

KEY_NEG_INF = -2139095041
KEY_MIN_FINITE = KEY_NEG_INF + 1
MASK_BIAS = -1e30


def _sigmoid(x):
    return 1.0 / (1.0 + jnp.exp(-x))


def _rmsnorm(x, g):
    ms = jnp.mean(x * x, axis=-1, keepdims=True)
    return x * lax.rsqrt(ms + EPS) * g


def _cparams(n_axes):
    return pltpu.CompilerParams(dimension_semantics=("arbitrary",) * n_axes,
                                vmem_limit_bytes=V7X_VMEM_LIMIT)


def _resident(shape):
    nd = len(shape)
    return pl.BlockSpec(shape, lambda *_: (0,) * nd, pipeline_mode=pl.Buffered(1))


def _adaln_kernel(c_ref, w_ref, b_ref, o_ref):
    c = c_ref[...]
    cond = (c * _sigmoid(c)).astype(BF16)
    o_ref[0] = jnp.dot(cond, w_ref[0], preferred_element_type=F32) + b_ref[0]


def _adaln(c_all, w_ada, b_ada):
    depth, d, n6 = w_ada.shape
    bc = c_all.shape[0]
    nblk = n6 // D_MODEL
    return pl.pallas_call(
        _adaln_kernel,
        grid=(depth, nblk),
        in_specs=[pl.BlockSpec((bc, d), lambda l, n: (0, 0)),
                  pl.BlockSpec((1, d, D_MODEL), lambda l, n: (l, 0, n)),
                  pl.BlockSpec((1, 1, D_MODEL), lambda l, n: (l, 0, n))],
        out_specs=pl.BlockSpec((1, bc, D_MODEL), lambda l, n: (l, 0, n)),
        out_shape=jax.ShapeDtypeStruct((depth, bc, n6), F32),
        compiler_params=_cparams(2), name="adaln",
    )(c_all, w_ada, b_ada.reshape(depth, 1, n6))


def _inproj_kernel(x_ref, mod_ref, g_ref, w_ref, cw_ref, st_ref,
                   ya_ref, q_ref, k_ref, v_ref, qi_ref, kiw_ref, ki_ref, sg_ref, tail_ref,
                   state_scr, *, tm):
    t = pl.program_id(1)

    @pl.when(t == 0)
    def _():
        state_scr[...] = st_ref[0]

    x = x_ref[0]
    h = _rmsnorm(x, g_ref[...]) * (1.0 + mod_ref[0, 1]) + mod_ref[0, 0]
    hb = h.astype(BF16)

    def seg(a, b):
        return jnp.dot(hb, w_ref[:, a:b], preferred_element_type=F32)

    bg = seg(C_CONV, C_CONV + D_CONV)
    u = seg(C_CONV + D_CONV, C_CONV + 2 * D_CONV) * seg(C_CONV + 2 * D_CONV, C_CONV + 3 * D_CONV)
    s0 = state_scr[0:1, :]
    s1 = state_scr[1:2, :]
    ri = lax.broadcasted_iota(I32, u.shape, 0)
    um1 = jnp.where(ri == 0, s1, pltpu.roll(u, 1, 0))
    um2 = jnp.where(ri == 0, s0, jnp.where(ri == 1, s1, pltpu.roll(u, 2, 0)))
    cw = cw_ref[...]
    y = cw[0:1] * um2
    y = y + cw[1:2] * um1
    y = y + cw[2:3] * u
    ya_ref[0] = (bg * y).astype(BF16)
    tail = u[tm - 2:tm, :]
    state_scr[...] = tail
    tail_ref[0] = tail

    q_ref[0] = (seg(C_Q, C_Q + D_Q) * Q_SCALE).astype(BF16)
    k_ref[0] = seg(C_KV, C_KV + D_KV)
    v_ref[0] = seg(C_KV + D_KV, C_KV + 2 * D_KV)
    half = D_QI // 2
    qi_ref[0, :, 0:half] = seg(C_QI, C_QI + half).astype(BF16)
    qi_ref[0, :, half:D_QI] = seg(C_QI + half, C_QI + D_QI).astype(BF16)
    kw = seg(C_KIW, C_KIW + LANES)
    ki_ref[0] = kw[:, 0:IDX_DIM]
    lane = lax.broadcasted_iota(I32, (1, LANES), 1)
    scale = jnp.where(lane < IDX_DIM, 1.0, jnp.where(lane < IDX_DIM + N_IDX_HEADS, WI_SCALE, 0.0))
    kiw_ref[0] = kw * scale

    for i in range(4):
        a = C_GATE + i * 512
        sg_ref[0, :, i * 512:(i + 1) * 512] = _sigmoid(seg(a, a + 512)).astype(BF16)


def _inproj(x, mod, g1, w_pack, conv_w, conv_state, tm):
    b, s, d = x.shape
    nt = s // tm
    row = lambda w, dt: jax.ShapeDtypeStruct((b, s, w), dt)
    rspec = lambda w: pl.BlockSpec((1, tm, w), lambda bi, ti: (bi, ti, 0))
    return pl.pallas_call(
        functools.partial(_inproj_kernel, tm=tm),
        grid=(b, nt),
        in_specs=[rspec(d),
                  pl.BlockSpec((1, 6, 1, d), lambda bi, ti: (bi, 0, 0, 0)),
                  _resident((1, d)),
                  _resident((d, D_PACK)),
                  _resident((CONV_W, D_CONV)),
                  pl.BlockSpec((1, CONV_W - 1, D_CONV), lambda bi, ti: (bi, 0, 0))],
        out_specs=[rspec(D_CONV), rspec(D_Q), rspec(D_KV), rspec(D_KV), rspec(D_QI),
                   rspec(LANES), rspec(IDX_DIM), rspec(2 * D_MODEL),
                   pl.BlockSpec((1, CONV_W - 1, D_CONV), lambda bi, ti: (bi, 0, 0))],
        out_shape=[row(D_CONV, BF16), row(D_Q, BF16), row(D_KV, F32), row(D_KV, F32),
                   row(D_QI, BF16), row(LANES, F32), row(IDX_DIM, F32), row(2 * D_MODEL, BF16),
                   jax.ShapeDtypeStruct((b, CONV_W - 1, D_CONV), F32)],
        scratch_shapes=[pltpu.VMEM((CONV_W - 1, D_CONV), F32)],
        compiler_params=_cparams(2), name="inproj",
    )(x, mod, g1, w_pack, conv_w, conv_state)


def _attn_kernel(q_ref, qi_ref, kiw_ref, k_ref, v_ref, ki_ref, o_ref,
                 kb_scr, vb_scr, kib_scr, qh_scr, qih_scr, key_scr, bias_scr, m_scr, l_scr, acc_scr,
                 *, tq, pos0, l_valid, topk):
    kb = KEY_BLOCK
    j = pl.program_id(1)

    @pl.when(j == 0)
    def _():
        for g in range(N_KV_HEADS):
            kb_scr[g] = k_ref[0, :, g * HEAD_DIM:(g + 1) * HEAD_DIM].astype(BF16)
            vb_scr[g] = v_ref[0, :, g * HEAD_DIM:(g + 1) * HEAD_DIM].astype(BF16)
        kib_scr[...] = ki_ref[0].astype(BF16)

    for h in range(N_HEADS):
        qh_scr[h] = q_ref[0, :, h * HEAD_DIM:(h + 1) * HEAD_DIM]
    for h in range(N_IDX_HEADS):
        qih_scr[h] = qi_ref[0, :, h * IDX_DIM:(h + 1) * IDX_DIM]
    wi = kiw_ref[0][:, IDX_DIM:IDX_DIM + N_IDX_HEADS]

    row0 = pos0 + j * tq
    rows = row0 + lax.broadcasted_iota(I32, (tq, 1), 0)
    lim = jnp.minimum(((rows >> 6) + 1) << 6, l_valid)
    last = row0 + tq - 1
    lim_max = jnp.minimum(((last >> 6) + 1) << 6, l_valid)
    nkc = (lim_max + kb - 1) >> 8

    def phase_a(c, carry):
        kic = kib_scr[pl.ds(pl.multiple_of(c * kb, kb), kb), :]
        acc = jnp.zeros((tq, kb), F32)
        for h in range(N_IDX_HEADS):
            d = lax.dot_general(qih_scr[h], kic, (((1,), (1,)), ((), ())),
                                preferred_element_type=F32)
            acc = acc + wi[:, h:h + 1] * jnp.maximum(d, 0.0)
        col = c * kb + lax.broadcasted_iota(I32, (tq, kb), 1)
        sc = jnp.where(col < lim, acc, -jnp.inf)
        bits = lax.bitcast_convert_type(sc, I32)
        key_scr[c] = bits ^ ((bits >> 31) & 0x7FFFFFFF)
        return carry

    lax.fori_loop(0, nkc, phase_a, 0)

    def count_ge(cand):
        def body(c, cnt):
            m = jnp.where(key_scr[c] >= cand, 1.0, 0.0)
            part = m[:, 0:LANES]
            for i in range(1, kb // LANES):
                part = part + m[:, i * LANES:(i + 1) * LANES]
            return cnt + part
        cnt = lax.fori_loop(0, nkc, body, jnp.zeros((tq, LANES), F32))
        return jnp.sum(cnt, axis=1, keepdims=True)

    kf = float(topk)
    t0 = jnp.where(count_ge(jnp.zeros((tq, 1), I32)) >= kf, 0, -2 ** 31).astype(I32)

    def bit_body(i, t):
        cand = t | lax.shift_left(jnp.int32(1), 30 - i)
        return jnp.where(count_ge(cand) >= kf, cand, t)

    thr = lax.fori_loop(0, 31, bit_body, t0)
    few = lim <= topk
    thr = jnp.where(few, KEY_MIN_FINITE, thr)

    tie = jnp.logical_and(count_ge(thr) > kf, jnp.logical_not(few))
    any_tie = jnp.max(jnp.where(tie, 1.0, 0.0)) > 0.0

    @pl.when(any_tie)
    def _():
        need = kf - count_ge(thr + 1)
        r_i = lax.broadcasted_iota(I32, (kb, kb), 0)
        c_i = lax.broadcasted_iota(I32, (kb, kb), 1)
        tri = jnp.where(r_i < c_i, 1.0, 0.0).astype(BF16)

        def body(c, run):
            kc = key_scr[c]
            eqf = jnp.where(kc == thr, 1.0, 0.0)
            rank = run + jnp.dot(eqf.astype(BF16), tri, preferred_element_type=F32)
            drop = jnp.logical_and(jnp.logical_and(tie, kc == thr), rank >= need)
            key_scr[c] = jnp.where(drop, KEY_NEG_INF, kc)
            return run + jnp.sum(eqf, axis=1, keepdims=True)

        lax.fori_loop(0, nkc, body, jnp.zeros((tq, 1), F32))

    m_scr[...] = jnp.full(m_scr.shape, MASK_BIAS, F32)
    l_scr[...] = jnp.zeros(l_scr.shape, F32)
    acc_scr[...] = jnp.zeros(acc_scr.shape, F32)

    def phase_c(c, carry):
        bias_scr[...] = jnp.where(key_scr[c] >= thr, 0.0, MASK_BIAS)
        start = pl.multiple_of(c * kb, kb)
        for h in range(N_HEADS):
            g = h // GROUP
            kc = kb_scr[g, pl.ds(start, kb), :]
            vc = vb_scr[g, pl.ds(start, kb), :]
            s = lax.dot_general(qh_scr[h], kc, (((1,), (1,)), ((), ())),
                                preferred_element_type=F32) + bias_scr[...]
            m_prev = m_scr[h]
            m_new = jnp.maximum(m_prev, jnp.max(s, axis=1, keepdims=True))
            alpha = jnp.exp(m_prev - m_new)
            p = jnp.exp(s - m_new)
            l_scr[h] = alpha * l_scr[h] + jnp.sum(p, axis=1, keepdims=True)
            acc_scr[h] = alpha * acc_scr[h] + jnp.dot(p.astype(BF16), vc, preferred_element_type=F32)
            m_scr[h] = m_new
        return carry

    lax.fori_loop(0, nkc, phase_c, 0)

    for h in range(N_HEADS):
        o_ref[0, :, h * HEAD_DIM:(h + 1) * HEAD_DIM] = (acc_scr[h] / l_scr[h]).astype(BF16)


def _attention(q, qi, kiw, k_all, v_all, ki_all, *, tq, pos0, l_valid, topk):
    b, s, _ = q.shape
    l_pad = k_all.shape[1]
    nkc_max = l_pad // KEY_BLOCK
    qspec = lambda w: pl.BlockSpec((1, tq, w), lambda bi, ji: (bi, ji, 0))
    kspec = lambda w: pl.BlockSpec((1, l_pad, w), lambda bi, ji: (bi, 0, 0))
    return pl.pallas_call(
        functools.partial(_attn_kernel, tq=tq, pos0=pos0, l_valid=l_valid, topk=topk),
        grid=(b, s // tq),
        in_specs=[qspec(D_Q), qspec(D_QI), qspec(LANES), kspec(D_KV), kspec(D_KV), kspec(IDX_DIM)],
        out_specs=qspec(D_Q),
        out_shape=jax.ShapeDtypeStruct((b, s, D_Q), BF16),
        scratch_shapes=[pltpu.VMEM((N_KV_HEADS, l_pad, HEAD_DIM), BF16),
                        pltpu.VMEM((N_KV_HEADS, l_pad, HEAD_DIM), BF16),
                        pltpu.VMEM((l_pad, IDX_DIM), BF16),
                        pltpu.VMEM((N_HEADS, tq, HEAD_DIM), BF16),
                        pltpu.VMEM((N_IDX_HEADS, tq, IDX_DIM), BF16),
                        pltpu.VMEM((nkc_max, tq, KEY_BLOCK), I32),
                        pltpu.VMEM((tq, KEY_BLOCK), F32),
                        pltpu.VMEM((N_HEADS, tq, 1), F32),
                        pltpu.VMEM((N_HEADS, tq, 1), F32),
                        pltpu.VMEM((N_HEADS, tq, HEAD_DIM), F32)],
        compiler_params=_cparams(2), name="attention",
    )(q, qi, kiw, k_all, v_all, ki_all)


def _split3(a):
    hi = a.astype(BF16)
    r = a - hi.astype(F32)
    mid = r.astype(BF16)
    lo = (r - mid.astype(F32)).astype(BF16)
    return hi, mid, lo


def _merge_kernel(*refs, with_router):
    if with_router:
        (x_ref, ya_ref, ao_ref, sg_ref, mod_ref, wc_ref, wa_ref, wo_ref, g2_ref, rw_ref, rb_ref,
         x1_ref, h2_ref, lg_ref) = refs
    else:
        (x_ref, ya_ref, ao_ref, sg_ref, mod_ref, wc_ref, wa_ref, wo_ref, g2_ref,
         x1_ref, h2_ref) = refs
    a = jnp.dot(ya_ref[0], wc_ref[...], preferred_element_type=F32)
    b = jnp.dot(ao_ref[0], wa_ref[...], preferred_element_type=F32)
    merged = sg_ref[0, :, 0:D_MODEL].astype(F32) * a + sg_ref[0, :, D_MODEL:2 * D_MODEL].astype(F32) * b
    mix = jnp.dot(merged.astype(BF16), wo_ref[...], preferred_element_type=F32)
    x1 = x_ref[0] + mod_ref[0, 2] * mix
    x1_ref[0] = x1
    h2 = _rmsnorm(x1, g2_ref[...]) * (1.0 + mod_ref[0, 4]) + mod_ref[0, 3]
    h2_ref[0] = h2.astype(BF16)
    if with_router:
        hs = _split3(h2)
        acc = jnp.zeros((h2.shape[0], LANES), F32)
        for i in range(3):
            for jj in range(3 - i):
                acc = acc + jnp.dot(hs[i], rw_ref[jj], preferred_element_type=F32)
        lg_ref[0] = acc + rb_ref[...]


def _merge(x, ya, ao, sg, mod, wc, wa, wo, g2, router=None, *, tm):
    b, s, d = x.shape
    r = mod.shape[2]
    rspec = lambda w: pl.BlockSpec((1, tm, w), lambda bi, ti: (bi, ti, 0))
    mspec = (pl.BlockSpec((1, 6, 1, d), lambda bi, ti: (bi, 0, 0, 0)) if r == 1 else
             pl.BlockSpec((1, 6, tm, d), lambda bi, ti: (bi, 0, ti, 0)))
    in_specs = [rspec(d), rspec(D_CONV), rspec(D_Q), rspec(2 * d), mspec,
                _resident((D_CONV, d)), _resident((D_Q, d)), _resident((d, d)), _resident((1, d))]
    args = [x, ya, ao, sg, mod, wc, wa, wo, g2]
    out_specs = [rspec(d), rspec(d)]
    out_shape = [jax.ShapeDtypeStruct((b, s, d), F32), jax.ShapeDtypeStruct((b, s, d), BF16)]
    if router is not None:
        in_specs += [_resident((3, d, LANES)), _resident((1, LANES))]
        args += list(router)
        out_specs.append(rspec(LANES))
        out_shape.append(jax.ShapeDtypeStruct((b, s, LANES), F32))
    return pl.pallas_call(
        functools.partial(_merge_kernel, with_router=router is not None),
        grid=(b, s // tm),
        in_specs=in_specs, out_specs=out_specs, out_shape=out_shape,
        compiler_params=_cparams(2), name="merge",
    )(*args)


def _ffn_kernel(h2_ref, x1_ref, mod_ref, w13_ref, w2_ref, o_ref, acc_scr, *, nc, fc):
    h = h2_ref[0]
    acc_scr[...] = jnp.zeros(acc_scr.shape, F32)

    def body(c, carry):
        ab = jnp.dot(h, w13_ref[c], preferred_element_type=F32)
        a = ab[:, 0:fc]
        hm = (a * _sigmoid(a) * ab[:, fc:2 * fc]).astype(BF16)
        acc_scr[...] += jnp.dot(hm, w2_ref[c], preferred_element_type=F32)
        return carry

    lax.fori_loop(0, nc, body, 0)
    o_ref[0] = x1_ref[0] + mod_ref[0, 5] * acc_scr[...]


def _ffn(h2, x1, mod, w13, w2, *, tm):
    b, s, d = x1.shape
    nc, _, fc2 = w13.shape
    r = mod.shape[2]
    rspec = lambda w: pl.BlockSpec((1, tm, w), lambda bi, ti: (bi, ti, 0))
    mspec = (pl.BlockSpec((1, 6, 1, d), lambda bi, ti: (bi, 0, 0, 0)) if r == 1 else
             pl.BlockSpec((1, 6, tm, d), lambda bi, ti: (bi, 0, ti, 0)))
    return pl.pallas_call(
        functools.partial(_ffn_kernel, nc=nc, fc=fc2 // 2),
        grid=(b, s // tm),
        in_specs=[rspec(d), rspec(d), mspec, _resident(w13.shape), _resident(w2.shape)],
        out_specs=rspec(d),
        out_shape=jax.ShapeDtypeStruct((b, s, d), F32),
        scratch_shapes=[pltpu.VMEM((tm, d), F32)],
        compiler_params=_cparams(2), name="ffn",
    )(h2, x1, mod, w13, w2)


def _moe_kernel(h2_ref, x1_ref, lg_ref, mod_ref, w13_ref, w2_ref, fg_ref, o_ref,
                gates_scr, eacc_scr, acc_scr, *, fc, final_norm):
    e = pl.program_id(2)
    c = pl.program_id(3)
    ne = pl.num_programs(2)
    nc = pl.num_programs(3)
    lane = lax.broadcasted_iota(I32, lg_ref.shape[1:], 1)

    @pl.when(jnp.logical_and(e == 0, c == 0))
    def _():
        lg = jnp.where(lane < N_EXPERTS, lg_ref[0], -jnp.inf)
        m1 = jnp.max(lg, axis=1, keepdims=True)
        i1 = jnp.min(jnp.where(lg == m1, lane, LANES), axis=1, keepdims=True)
        rest = jnp.where(lane == i1, -jnp.inf, lg)
        m2 = jnp.max(rest, axis=1, keepdims=True)
        i2 = jnp.min(jnp.where(rest == m2, lane, LANES), axis=1, keepdims=True)
        e2 = jnp.exp(m2 - m1)
        den = 1.0 + e2
        gates_scr[...] = (jnp.where(lane == i1, 1.0 / den, 0.0)
                          + jnp.where(lane == i2, e2 / den, 0.0))
        acc_scr[...] = jnp.zeros(acc_scr.shape, F32)

    @pl.when(c == 0)
    def _():
        eacc_scr[...] = jnp.zeros(eacc_scr.shape, F32)

    ab = jnp.dot(h2_ref[0], w13_ref[0, 0], preferred_element_type=F32)
    a = ab[:, 0:fc]
    hm = (a * _sigmoid(a) * ab[:, fc:2 * fc]).astype(BF16)
    eacc_scr[...] += jnp.dot(hm, w2_ref[0, 0], preferred_element_type=F32)

    @pl.when(c == nc - 1)
    def _():
        gate = jnp.sum(jnp.where(lane == e, gates_scr[...], 0.0), axis=1, keepdims=True)
        acc_scr[...] += gate * eacc_scr[...]

    @pl.when(jnp.logical_and(e == ne - 1, c == nc - 1))
    def _():
        x2 = x1_ref[0] + mod_ref[0, 5] * acc_scr[...]
        o_ref[0] = _rmsnorm(x2, fg_ref[...]) if final_norm else x2


def _moe(h2, x1, logits, mod, w13, w2, fg, *, tm, final_norm):
    b, s, d = x1.shape
    ne, nc, _, fc2 = w13.shape
    r = mod.shape[2]
    rspec = lambda w: pl.BlockSpec((1, tm, w), lambda bi, ti, ei, ci: (bi, ti, 0))
    mspec = (pl.BlockSpec((1, 6, 1, d), lambda bi, ti, ei, ci: (bi, 0, 0, 0)) if r == 1 else
             pl.BlockSpec((1, 6, tm, d), lambda bi, ti, ei, ci: (bi, 0, ti, 0)))
    return pl.pallas_call(
        functools.partial(_moe_kernel, fc=fc2 // 2, final_norm=final_norm),
        grid=(b, s // tm, ne, nc),
        in_specs=[rspec(d), rspec(d), rspec(LANES), mspec,
                  pl.BlockSpec((1, 1, d, fc2), lambda bi, ti, ei, ci: (ei, ci, 0, 0)),
                  pl.BlockSpec((1, 1, fc2 // 2, d), lambda bi, ti, ei, ci: (ei, ci, 0, 0)),
                  _resident((1, d))],
        out_specs=rspec(d),
        out_shape=jax.ShapeDtypeStruct((b, s, d), F32),
        scratch_shapes=[pltpu.VMEM((tm, LANES), F32), pltpu.VMEM((tm, d), F32), pltpu.VMEM((tm, d), F32)],
        compiler_params=_cparams(4), name="moe",
    )(h2, x1, logits, mod, w13, w2, fg)


def _pack_w_in(w):
    cut = C_KIW + IDX_DIM + N_IDX_HEADS
    pad = jnp.zeros((w.shape[0], C_GATE - cut), w.dtype)
    return jnp.concatenate([w[:, :cut], pad, w[:, cut:]], axis=1).astype(BF16)


def _chunk_w13(w1, w3, fc):
    *lead, d, f = w1.shape
    nc = f // fc
    a = w1.reshape(*lead, d, nc, fc)
    b = w3.reshape(*lead, d, nc, fc)
    ab = jnp.concatenate([a, b], axis=-1)
    return jnp.moveaxis(ab, -2, -3).astype(BF16)


def _chunk_w2(w2, fc):
    *lead, f, d = w2.shape
    return w2.reshape(*lead, f // fc, fc, d).astype(BF16)


def _router_terms(rw, rb):
    d = rw.shape[0]
    rwp = jnp.zeros((d, LANES), F32).at[:, :N_EXPERTS].set(rw)
    hi, mid, lo = _split3(rwp)
    rbp = jnp.zeros((1, LANES), F32).at[0, :N_EXPERTS].set(rb)
    return jnp.stack([hi, mid, lo]), rbp


def _trunk(x, mod_all, caches, p, *, tm, tq):
    b, t, d = x.shape
    depth = mod_all.shape[0]
    flat = caches is not None
    ks, vs, kis, tails = [], [], [], []
    for l in range(depth):
        mod = mod_all[l].reshape(b, 6, 1, d)
        if caches is None:
            conv_state = jnp.zeros((b, CONV_W - 1, D_CONV), F32)
        else:
            conv_state = caches[3][l]
        ya, q, k, v, qi, kiw, ki, sg, tail = _inproj(
            x, mod, p["norm1_g"][l], p["w_in"][l], p["conv_w"][l], conv_state, tm)
        if caches is None:
            k_all, v_all, ki_all = k, v, ki
            pos0, l_valid = 0, t
        else:
            past = caches[0][l].shape[1]
            l_valid = past + t
            l_pad = -(-l_valid // KEY_BLOCK) * KEY_BLOCK
            cat = lambda old, new: jnp.concatenate(
                [old.reshape(b, past, -1), new, jnp.zeros((b, l_pad - l_valid, new.shape[-1]), F32)], axis=1)
            k_all, v_all, ki_all = cat(caches[0][l], k), cat(caches[1][l], v), cat(caches[2][l], ki)
            pos0 = past
        ao = _attention(q, qi, kiw, k_all, v_all, ki_all, tq=tq, pos0=pos0, l_valid=l_valid,
                        topk=min(TOPK_MAX, l_valid // 4))
        is_moe = l % 2 == 1
        router = (p["router_w"][l // 2], p["router_b"][l // 2]) if is_moe else None
        if flat:
            n = b * t
            fl = lambda a: a.reshape(1, n, a.shape[-1])
            modf = jnp.repeat(mod_all[l].reshape(b, 6, d), t, axis=0).transpose(1, 0, 2)[None]
            xs, yas, aos, sgs, tmf = fl(x), fl(ya), fl(ao), fl(sg), n
        else:
            modf, xs, yas, aos, sgs, tmf = mod, x, ya, ao, sg, tm
        outs = _merge(xs, yas, aos, sgs, modf, p["w_conv_out"][l], p["w_attn_out"][l], p["w_o"][l],
                      p["norm2_g"][l], router, tm=tmf)
        if is_moe:
            x1, h2, logits = outs
            x = _moe(h2, x1, logits, modf, p["moe_w13"][l // 2], p["moe_w2"][l // 2],
                     p["final_g"], tm=min(xs.shape[1], 1024), final_norm=l == depth - 1)
        else:
            x1, h2 = outs
            x = _ffn(h2, x1, modf, p["ffn_w13"][l // 2], p["ffn_w2"][l // 2], tm=tmf)
        x = x.reshape(b, t, d)
        ks.append(k.reshape(b, t, N_KV_HEADS, HEAD_DIM))
        vs.append(v.reshape(b, t, N_KV_HEADS, HEAD_DIM))
        kis.append(ki)
        tails.append(tail)
    return x, jnp.stack(ks), jnp.stack(vs), jnp.stack(kis), jnp.stack(tails)


def kernel(x_prompt, x_sample, c_prompt, c_sample, cache_k, cache_v, cache_idx_k, state_conv, w_ada, b_ada, norm1_g, w_in, conv_w, w_conv_out, w_attn_out, w_o, norm2_g, ffn_w1, ffn_w3, ffn_w2, router_w, router_b, moe_w1, moe_w3, moe_w2, final_g):
    depth = w_in.shape[0]
    bp = x_prompt.shape[0]
    rterms = [_router_terms(router_w[i], router_b[i]) for i in range(router_w.shape[0])]
    p = {
        "norm1_g": norm1_g.reshape(depth, 1, -1),
        "norm2_g": norm2_g.reshape(depth, 1, -1),
        "final_g": final_g.reshape(1, -1),
        "w_in": jnp.stack([_pack_w_in(w_in[l]) for l in range(depth)]),
        "conv_w": conv_w,
        "w_conv_out": w_conv_out.astype(BF16),
        "w_attn_out": w_attn_out.astype(BF16),
        "w_o": w_o.astype(BF16),
        "ffn_w13": _chunk_w13(ffn_w1, ffn_w3, FFN_CHUNK),
        "ffn_w2": _chunk_w2(ffn_w2, FFN_CHUNK),
        "moe_w13": _chunk_w13(moe_w1, moe_w3, MOE_CHUNK),
        "moe_w2": _chunk_w2(moe_w2, MOE_CHUNK),
        "router_w": [r[0] for r in rterms],
        "router_b": [r[1] for r in rterms],
    }
    mod_all = _adaln(jnp.concatenate([c_prompt, c_sample], axis=0), w_ada.astype(BF16), b_ada)
    seq = x_prompt.shape[1]
    tm = min(512, seq)
    tq = min(256, seq)
    yp, kp, vp, kip, cp = _trunk(x_prompt, mod_all[:, :bp], None, p, tm=tm, tq=tq)
    ts = x_sample.shape[1]
    ys, ksm, vsm, kism, csm = _trunk(x_sample, mod_all[:, bp:], (cache_k, cache_v, cache_idx_k, state_conv), p,
                                     tm=ts, tq=ts)
    return (yp, ys, kp, vp, kip, cp, ksm, vsm, kism, csm)
```

```python
import functools

import jax
import jax.numpy as jnp
from jax import lax
from jax.experimental import pallas as pl
from jax.experimental.pallas import tpu as pltpu

F32 = jnp.float32
BF16 = jnp.bfloat16
I32 = jnp.int32

D_MODEL = 1024
D_CONV = 512
CONV_W = 3
N_HEADS = 8
N_KV_HEADS = 4
HEAD_DIM = 64
GROUP = N_HEADS // N_KV_HEADS
N_IDX_HEADS = 16
IDX_DIM = 64
TOPK_MAX = 256
CHUNK = 64
D_FF = 2816
N_EXPERTS = 8
D_FF_EXPERT = 3584
EPS = 1e-6

D_Q = N_HEADS * HEAD_DIM
D_KV = N_KV_HEADS * HEAD_DIM
D_QI = N_IDX_HEADS * IDX_DIM
LANES = 128
C_CONV = 0
C_Q = 3 * D_CONV
C_KV = C_Q + D_Q
C_QI = C_KV + 2 * D_KV
C_KIW = C_QI + D_QI
C_GATE = C_KIW + LANES
D_PACK = C_GATE + 2 * D_MODEL
WI_SCALE = (IDX_DIM ** -0.5) * (N_IDX_HEADS ** -0.5)
Q_SCALE = HEAD_DIM ** -0.5

KEY_BLOCK = 256
FFN_CHUNK = 256
MOE_CHUNK = 512
V7X_VMEM_LIMIT = 56 * 1024 * 1024

KEY_NEG_INF = -2139095041
KEY_MIN_FINITE = KEY_NEG_INF + 1
MASK_BIAS = -1e30


def _sigmoid(x):
    return 1.0 / (1.0 + jnp.exp(-x))


def _rmsnorm(x, g):
    ms = jnp.mean(x * x, axis=-1, keepdims=True)
    return x * lax.rsqrt(ms + EPS) * g


def _cparams(n_axes):
    return pltpu.CompilerParams(dimension_semantics=("arbitrary",) * n_axes,
                                vmem_limit_bytes=V7X_VMEM_LIMIT)


def _resident(shape):
    nd = len(shape)
    return pl.BlockSpec(shape, lambda *_: (0,) * nd, pipeline_mode=pl.Buffered(1))


def _adaln_kernel(c_ref, w_ref, b_ref, o_ref):
    c = c_ref[...]
    cond = (c * _sigmoid(c)).astype(BF16)
    o_ref[0] = jnp.dot(cond, w_ref[0], preferred_element_type=F32) + b_ref[0]


def _adaln(c_all, w_ada, b_ada):
    depth, d, n6 = w_ada.shape
    bc = c_all.shape[0]
    nblk = n6 // D_MODEL
    return pl.pallas_call(
        _adaln_kernel,
        grid=(depth, nblk),
        in_specs=[pl.BlockSpec((bc, d), lambda l, n: (0, 0)),
                  pl.BlockSpec((1, d, D_MODEL), lambda l, n: (l, 0, n)),
                  pl.BlockSpec((1, 1, D_MODEL), lambda l, n: (l, 0, n))],
        out_specs=pl.BlockSpec((1, bc, D_MODEL), lambda l, n: (l, 0, n)),
        out_shape=jax.ShapeDtypeStruct((depth, bc, n6), F32),
        compiler_params=_cparams(2), name="adaln",
    )(c_all, w_ada, b_ada.reshape(depth, 1, n6))


def _inproj_kernel(x_ref, mod_ref, g_ref, w_ref, cw_ref, st_ref,
                   ya_ref, q_ref, k_ref, v_ref, qi_ref, kiw_ref, ki_ref, sg_ref, tail_ref,
                   state_scr, *, tm):
    t = pl.program_id(1)

    @pl.when(t == 0)
    def _():
        state_scr[...] = st_ref[0]

    x = x_ref[0]
    h = _rmsnorm(x, g_ref[...]) * (1.0 + mod_ref[0, 1]) + mod_ref[0, 0]
    hb = h.astype(BF16)

    def seg(a, b):
        return jnp.dot(hb, w_ref[:, a:b], preferred_element_type=F32)

    bg = seg(C_CONV, C_CONV + D_CONV)
    u = seg(C_CONV + D_CONV, C_CONV + 2 * D_CONV) * seg(C_CONV + 2 * D_CONV, C_CONV + 3 * D_CONV)
    s0 = state_scr[0:1, :]
    s1 = state_scr[1:2, :]
    ri = lax.broadcasted_iota(I32, u.shape, 0)
    um1 = jnp.where(ri == 0, s1, pltpu.roll(u, 1, 0))
    um2 = jnp.where(ri == 0, s0, jnp.where(ri == 1, s1, pltpu.roll(u, 2, 0)))
    cw = cw_ref[...]
    y = cw[0:1] * um2
    y = y + cw[1:2] * um1
    y = y + cw[2:3] * u
    ya_ref[0] = (bg * y).astype(BF16)
    tail = u[tm - 2:tm, :]
    state_scr[...] = tail
    tail_ref[0] = tail

    q_ref[0] = (seg(C_Q, C_Q + D_Q) * Q_SCALE).astype(BF16)
    k_ref[0] = seg(C_KV, C_KV + D_KV)
    v_ref[0] = seg(C_KV + D_KV, C_KV + 2 * D_KV)
    half = D_QI // 2
    qi_ref[0, :, 0:half] = seg(C_QI, C_QI + half).astype(BF16)
    qi_ref[0, :, half:D_QI] = seg(C_QI + half, C_QI + D_QI).astype(BF16)
    kw = seg(C_KIW, C_KIW + LANES)
    ki_ref[0] = kw[:, 0:IDX_DIM]
    lane = lax.broadcasted_iota(I32, (1, LANES), 1)
    scale = jnp.where(lane < IDX_DIM, 1.0, jnp.where(lane < IDX_DIM + N_IDX_HEADS, WI_SCALE, 0.0))
    kiw_ref[0] = kw * scale

    for i in range(4):
        a = C_GATE + i * 512
        sg_ref[0, :, i * 512:(i + 1) * 512] = _sigmoid(seg(a, a + 512)).astype(BF16)


def _inproj(x, mod, g1, w_pack, conv_w, conv_state, tm):
    b, s, d = x.shape
    nt = s // tm
    row = lambda w, dt: jax.ShapeDtypeStruct((b, s, w), dt)
    rspec = lambda w: pl.BlockSpec((1, tm, w), lambda bi, ti: (bi, ti, 0))
    return pl.pallas_call(
        functools.partial(_inproj_kernel, tm=tm),
        grid=(b, nt),
        in_specs=[rspec(d),
                  pl.BlockSpec((1, 6, 1, d), lambda bi, ti: (bi, 0, 0, 0)),
                  _resident((1, d)),
                  _resident((d, D_PACK)),
                  _resident((CONV_W, D_CONV)),
                  pl.BlockSpec((1, CONV_W - 1, D_CONV), lambda bi, ti: (bi, 0, 0))],
        out_specs=[rspec(D_CONV), rspec(D_Q), rspec(D_KV), rspec(D_KV), rspec(D_QI),
                   rspec(LANES), rspec(IDX_DIM), rspec(2 * D_MODEL),
                   pl.BlockSpec((1, CONV_W - 1, D_CONV), lambda bi, ti: (bi, 0, 0))],
        out_shape=[row(D_CONV, BF16), row(D_Q, BF16), row(D_KV, F32), row(D_KV, F32),
                   row(D_QI, BF16), row(LANES, F32), row(IDX_DIM, F32), row(2 * D_MODEL, BF16),
                   jax.ShapeDtypeStruct((b, CONV_W - 1, D_CONV), F32)],
        scratch_shapes=[pltpu.VMEM((CONV_W - 1, D_CONV), F32)],
        compiler_params=_cparams(2), name="inproj",
    )(x, mod, g1, w_pack, conv_w, conv_state)


def _pair_blockdiag(tile, low_half):
    lane = lax.broadcasted_iota(I32, tile.shape, 1)
    lo = lane < HEAD_DIM
    swapped = pltpu.roll(tile, HEAD_DIM, 1)
    top = jnp.where(lo, tile if low_half else swapped, 0.0)
    bot = jnp.where(lo, 0.0, swapped if low_half else tile)
    return jnp.concatenate([top, bot], axis=0).astype(BF16)


def _attn_kernel(q_ref, qi_ref, kiw_ref, k_ref, v_ref, ki_ref, o_ref,
                 kbd_scr, vbd_scr, kibd_scr, key_scr, s_scr, m_scr, l_scr, acc_scr,
                 *, tq, pos0, l_valid, topk):
    kb = KEY_BLOCK
    j = pl.program_id(1)
    nkc_all = key_scr.shape[0]
    npair = N_IDX_HEADS // 2
    nt = (((1,), (1,)), ((), ()))

    @pl.when(j == 0)
    def _():
        def build(c, carry):
            rows = pl.ds(pl.multiple_of(c * kb, kb), kb)
            kibd_scr[c] = _pair_blockdiag(ki_ref[0, rows, :], True)
            for g in range(N_KV_HEADS):
                t = g // 2
                kbd_scr[g, c] = _pair_blockdiag(k_ref[0, rows, t * LANES:(t + 1) * LANES], g % 2 == 0)
                vbd_scr[g, c] = _pair_blockdiag(v_ref[0, rows, t * LANES:(t + 1) * LANES], g % 2 == 0)
            return carry

        lax.fori_loop(0, nkc_all, build, 0)

    w_t = kiw_ref[0].T

    row0 = pos0 + j * tq
    rows = row0 + lax.broadcasted_iota(I32, (1, tq), 1)
    lim = jnp.minimum(((rows >> 6) + 1) << 6, l_valid)
    last = row0 + tq - 1
    lim_max = jnp.minimum(((last >> 6) + 1) << 6, l_valid)
    nkc = (lim_max + kb - 1) >> 8

    def phase_a(c, carry):
        kbd = kibd_scr[c]
        acc = jnp.zeros((kb, tq), F32)
        for t in range(npair):
            d2 = lax.dot_general(kbd, qi_ref[0, :, t * LANES:(t + 1) * LANES], nt,
                                 preferred_element_type=F32)
            r = IDX_DIM + 2 * t
            acc = acc + w_t[r:r + 1, :] * jnp.maximum(d2[0:kb], 0.0)
            acc = acc + w_t[r + 1:r + 2, :] * jnp.maximum(d2[kb:2 * kb], 0.0)
        col = c * kb + lax.broadcasted_iota(I32, (kb, tq), 0)
        sc = jnp.where(col < lim, acc, -jnp.inf)
        bits = lax.bitcast_convert_type(sc, I32)
        key_scr[c] = bits ^ ((bits >> 31) & 0x7FFFFFFF)
        return carry

    lax.fori_loop(0, nkc, phase_a, 0)

    def count_ge(cand):
        def body(c, cnt):
            m = jnp.where(key_scr[c] >= cand, 1.0, 0.0)
            parts = [m[8 * i:8 * (i + 1)] for i in range(kb // 8)]
            while len(parts) > 1:
                parts = [parts[i] + parts[i + 1] for i in range(0, len(parts), 2)]
            return cnt + parts[0]
        cnt = lax.fori_loop(0, nkc, body, jnp.zeros((8, tq), F32))
        return jnp.sum(cnt, axis=0, keepdims=True)

    kf = float(topk)
    t0 = jnp.where(count_ge(jnp.zeros((1, tq), I32)) >= kf, 0, -2 ** 31).astype(I32)

    def bit_body(i, t):
        cand = t | lax.shift_left(jnp.int32(1), 30 - i)
        return jnp.where(count_ge(cand) >= kf, cand, t)

    thr = lax.fori_loop(0, 31, bit_body, t0)
    few = lim <= topk
    thr = jnp.where(few, KEY_MIN_FINITE, thr)

    tie = jnp.logical_and(count_ge(thr) > kf, jnp.logical_not(few))
    any_tie = jnp.max(jnp.where(tie, 1.0, 0.0)) > 0.0

    @pl.when(any_tie)
    def _():
        need = kf - count_ge(thr + 1)
        r_i = lax.broadcasted_iota(I32, (kb, kb), 0)
        c_i = lax.broadcasted_iota(I32, (kb, kb), 1)
        tri = jnp.where(c_i < r_i, 1.0, 0.0).astype(BF16)

        def body(c, run):
            kc = key_scr[c]
            eqf = jnp.where(kc == thr, 1.0, 0.0)
            rank = run + jnp.dot(tri, eqf.astype(BF16), preferred_element_type=F32)
            drop = jnp.logical_and(jnp.logical_and(tie, kc == thr), rank >= need)
            key_scr[c] = jnp.where(drop, KEY_NEG_INF, kc)
            return run + jnp.sum(eqf, axis=0, keepdims=True)

        lax.fori_loop(0, nkc, body, jnp.zeros((1, tq), F32))

    nl = kb // LANES
    m_scr[...] = jnp.full(m_scr.shape, MASK_BIAS, F32)
    l_scr[...] = jnp.zeros(l_scr.shape, F32)
    acc_scr[...] = jnp.zeros(acc_scr.shape, F32)

    def sweep1(c, carry):
        bias = jnp.where(key_scr[c] >= thr, 0.0, MASK_BIAS).T
        for g in range(N_KV_HEADS):
            s2 = lax.dot_general(q_ref[0, :, g * LANES:(g + 1) * LANES], kbd_scr[g, c], nt,
                                 preferred_element_type=F32)
            for r in range(GROUP):
                h = GROUP * g + r
                s = s2[:, r * kb:(r + 1) * kb] + bias
                s_scr[h, c] = s
                mx = m_scr[h]
                for i in range(nl):
                    mx = jnp.maximum(mx, s[:, i * LANES:(i + 1) * LANES])
                m_scr[h] = mx
        return carry

    lax.fori_loop(0, nkc, sweep1, 0)
    for h in range(N_HEADS):
        m_scr[h] = jnp.broadcast_to(jnp.max(m_scr[h], axis=1, keepdims=True), (tq, LANES))

    def sweep2(c, carry):
        for g in range(N_KV_HEADS):
            p2 = []
            for r in range(GROUP):
                h = GROUP * g + r
                s = s_scr[h, c]
                mb = m_scr[h]
                ps = [jnp.exp(s[:, i * LANES:(i + 1) * LANES] - mb) for i in range(nl)]
                psum = ps[0]
                for i in range(1, nl):
                    psum = psum + ps[i]
                l_scr[h] += psum
                p2 += ps
            acc_scr[g] += jnp.dot(jnp.concatenate(p2, axis=1).astype(BF16), vbd_scr[g, c],
                                  preferred_element_type=F32)
        return carry

    lax.fori_loop(0, nkc, sweep2, 0)
    low = lax.broadcasted_iota(I32, (tq, LANES), 1) < HEAD_DIM
    for g in range(N_KV_HEADS):
        l0 = jnp.sum(l_scr[GROUP * g], axis=1, keepdims=True)
        l1 = jnp.sum(l_scr[GROUP * g + 1], axis=1, keepdims=True)
        o_ref[0, :, g * LANES:(g + 1) * LANES] = (acc_scr[g] / jnp.where(low, l0, l1)).astype(BF16)


def _attention(q, qi, kiw, k_all, v_all, ki_all, *, tq, pos0, l_valid, topk):
    b, s, _ = q.shape
    l_pad = k_all.shape[1]
    nkc_max = l_pad // KEY_BLOCK
    qspec = lambda w: pl.BlockSpec((1, tq, w), lambda bi, ji: (bi, ji, 0))
    kspec = lambda w: pl.BlockSpec((1, l_pad, w), lambda bi, ji: (bi, 0, 0))
    return pl.pallas_call(
        functools.partial(_attn_kernel, tq=tq, pos0=pos0, l_valid=l_valid, topk=topk),
        grid=(b, s // tq),
        in_specs=[qspec(D_Q), qspec(D_QI), qspec(LANES), kspec(D_KV), kspec(D_KV), kspec(LANES)],
        out_specs=qspec(D_Q),
        out_shape=jax.ShapeDtypeStruct((b, s, D_Q), BF16),
        scratch_shapes=[pltpu.VMEM((N_KV_HEADS, nkc_max, 2 * KEY_BLOCK, LANES), BF16),
                        pltpu.VMEM((N_KV_HEADS, nkc_max, 2 * KEY_BLOCK, LANES), BF16),
                        pltpu.VMEM((nkc_max, 2 * KEY_BLOCK, LANES), BF16),
                        pltpu.VMEM((nkc_max, KEY_BLOCK, tq), I32),
                        pltpu.VMEM((N_HEADS, nkc_max, tq, KEY_BLOCK), F32),
                        pltpu.VMEM((N_HEADS, tq, LANES), F32),
                        pltpu.VMEM((N_HEADS, tq, LANES), F32),
                        pltpu.VMEM((N_KV_HEADS, tq, LANES), F32)],
        compiler_params=_cparams(2), name="attention",
    )(q, qi, kiw, k_all, v_all, ki_all)


def _split3(a):
    hi = a.astype(BF16)
    r = a - hi.astype(F32)
    mid = r.astype(BF16)
    lo = (r - mid.astype(F32)).astype(BF16)
    return hi, mid, lo


def _merge_kernel(*refs, with_router):
    if with_router:
        (x_ref, ya_ref, ao_ref, sg_ref, mod_ref, wc_ref, wa_ref, wo_ref, g2_ref, rw_ref, rb_ref,
         x1_ref, h2_ref, lg_ref) = refs
    else:
        (x_ref, ya_ref, ao_ref, sg_ref, mod_ref, wc_ref, wa_ref, wo_ref, g2_ref,
         x1_ref, h2_ref) = refs
    a = jnp.dot(ya_ref[0], wc_ref[...], preferred_element_type=F32)
    b = jnp.dot(ao_ref[0], wa_ref[...], preferred_element_type=F32)
    merged = sg_ref[0, :, 0:D_MODEL].astype(F32) * a + sg_ref[0, :, D_MODEL:2 * D_MODEL].astype(F32) * b
    mix = jnp.dot(merged.astype(BF16), wo_ref[...], preferred_element_type=F32)
    x1 = x_ref[0] + mod_ref[0, 2] * mix
    x1_ref[0] = x1
    h2 = _rmsnorm(x1, g2_ref[...]) * (1.0 + mod_ref[0, 4]) + mod_ref[0, 3]
    h2_ref[0] = h2.astype(BF16)
    if with_router:
        hs = _split3(h2)
        acc = jnp.zeros((h2.shape[0], LANES), F32)
        for i in range(3):
            for jj in range(3 - i):
                acc = acc + jnp.dot(hs[i], rw_ref[jj], preferred_element_type=F32)
        lg_ref[0] = acc + rb_ref[...]


def _merge(x, ya, ao, sg, mod, wc, wa, wo, g2, router=None, *, tm):
    b, s, d = x.shape
    r = mod.shape[2]
    rspec = lambda w: pl.BlockSpec((1, tm, w), lambda bi, ti: (bi, ti, 0))
    mspec = (pl.BlockSpec((1, 6, 1, d), lambda bi, ti: (bi, 0, 0, 0)) if r == 1 else
             pl.BlockSpec((1, 6, tm, d), lambda bi, ti: (bi, 0, ti, 0)))
    in_specs = [rspec(d), rspec(D_CONV), rspec(D_Q), rspec(2 * d), mspec,
                _resident((D_CONV, d)), _resident((D_Q, d)), _resident((d, d)), _resident((1, d))]
    args = [x, ya, ao, sg, mod, wc, wa, wo, g2]
    out_specs = [rspec(d), rspec(d)]
    out_shape = [jax.ShapeDtypeStruct((b, s, d), F32), jax.ShapeDtypeStruct((b, s, d), BF16)]
    if router is not None:
        in_specs += [_resident((3, d, LANES)), _resident((1, LANES))]
        args += list(router)
        out_specs.append(rspec(LANES))
        out_shape.append(jax.ShapeDtypeStruct((b, s, LANES), F32))
    return pl.pallas_call(
        functools.partial(_merge_kernel, with_router=router is not None),
        grid=(b, s // tm),
        in_specs=in_specs, out_specs=out_specs, out_shape=out_shape,
        compiler_params=_cparams(2), name="merge",
    )(*args)


def _ffn_kernel(h2_ref, x1_ref, mod_ref, w13_ref, w2_ref, o_ref, acc_scr, *, nc, fc):
    h = h2_ref[0]
    acc_scr[...] = jnp.zeros(acc_scr.shape, F32)

    def body(c, carry):
        ab = jnp.dot(h, w13_ref[c], preferred_element_type=F32)
        a = ab[:, 0:fc]
        hm = (a * _sigmoid(a) * ab[:, fc:2 * fc]).astype(BF16)
        acc_scr[...] += jnp.dot(hm, w2_ref[c], preferred_element_type=F32)
        return carry

    lax.fori_loop(0, nc, body, 0)
    o_ref[0] = x1_ref[0] + mod_ref[0, 5] * acc_scr[...]


def _ffn(h2, x1, mod, w13, w2, *, tm):
    b, s, d = x1.shape
    nc, _, fc2 = w13.shape
    r = mod.shape[2]
    rspec = lambda w: pl.BlockSpec((1, tm, w), lambda bi, ti: (bi, ti, 0))
    mspec = (pl.BlockSpec((1, 6, 1, d), lambda bi, ti: (bi, 0, 0, 0)) if r == 1 else
             pl.BlockSpec((1, 6, tm, d), lambda bi, ti: (bi, 0, ti, 0)))
    return pl.pallas_call(
        functools.partial(_ffn_kernel, nc=nc, fc=fc2 // 2),
        grid=(b, s // tm),
        in_specs=[rspec(d), rspec(d), mspec, _resident(w13.shape), _resident(w2.shape)],
        out_specs=rspec(d),
        out_shape=jax.ShapeDtypeStruct((b, s, d), F32),
        scratch_shapes=[pltpu.VMEM((tm, d), F32)],
        compiler_params=_cparams(2), name="ffn",
    )(h2, x1, mod, w13, w2)


def _moe_kernel(h2_ref, x1_ref, lg_ref, mod_ref, w13_ref, w2_ref, fg_ref, o_ref,
                gates_scr, eacc_scr, acc_scr, *, fc, final_norm):
    e = pl.program_id(2)
    c = pl.program_id(3)
    ne = pl.num_programs(2)
    nc = pl.num_programs(3)
    lane = lax.broadcasted_iota(I32, lg_ref.shape[1:], 1)

    @pl.when(jnp.logical_and(e == 0, c == 0))
    def _():
        lg = jnp.where(lane < N_EXPERTS, lg_ref[0], -jnp.inf)
        m1 = jnp.max(lg, axis=1, keepdims=True)
        i1 = jnp.min(jnp.where(lg == m1, lane, LANES), axis=1, keepdims=True)
        rest = jnp.where(lane == i1, -jnp.inf, lg)
        m2 = jnp.max(rest, axis=1, keepdims=True)
        i2 = jnp.min(jnp.where(rest == m2, lane, LANES), axis=1, keepdims=True)
        e2 = jnp.exp(m2 - m1)
        den = 1.0 + e2
        gates_scr[...] = (jnp.where(lane == i1, 1.0 / den, 0.0)
                          + jnp.where(lane == i2, e2 / den, 0.0))
        acc_scr[...] = jnp.zeros(acc_scr.shape, F32)

    @pl.when(c == 0)
    def _():
        eacc_scr[...] = jnp.zeros(eacc_scr.shape, F32)

    ab = jnp.dot(h2_ref[0], w13_ref[0, 0], preferred_element_type=F32)
    a = ab[:, 0:fc]
    hm = (a * _sigmoid(a) * ab[:, fc:2 * fc]).astype(BF16)
    eacc_scr[...] += jnp.dot(hm, w2_ref[0, 0], preferred_element_type=F32)

    @pl.when(c == nc - 1)
    def _():
        gate = jnp.sum(jnp.where(lane == e, gates_scr[...], 0.0), axis=1, keepdims=True)
        acc_scr[...] += gate * eacc_scr[...]

    @pl.when(jnp.logical_and(e == ne - 1, c == nc - 1))
    def _():
        x2 = x1_ref[0] + mod_ref[0, 5] * acc_scr[...]
        o_ref[0] = _rmsnorm(x2, fg_ref[...]) if final_norm else x2


def _moe(h2, x1, logits, mod, w13, w2, fg, *, tm, final_norm):
    b, s, d = x1.shape
    ne, nc, _, fc2 = w13.shape
    r = mod.shape[2]
    rspec = lambda w: pl.BlockSpec((1, tm, w), lambda bi, ti, ei, ci: (bi, ti, 0))
    mspec = (pl.BlockSpec((1, 6, 1, d), lambda bi, ti, ei, ci: (bi, 0, 0, 0)) if r == 1 else
             pl.BlockSpec((1, 6, tm, d), lambda bi, ti, ei, ci: (bi, 0, ti, 0)))
    return pl.pallas_call(
        functools.partial(_moe_kernel, fc=fc2 // 2, final_norm=final_norm),
        grid=(b, s // tm, ne, nc),
        in_specs=[rspec(d), rspec(d), rspec(LANES), mspec,
                  pl.BlockSpec((1, 1, d, fc2), lambda bi, ti, ei, ci: (ei, ci, 0, 0)),
                  pl.BlockSpec((1, 1, fc2 // 2, d), lambda bi, ti, ei, ci: (ei, ci, 0, 0)),
                  _resident((1, d))],
        out_specs=rspec(d),
        out_shape=jax.ShapeDtypeStruct((b, s, d), F32),
        scratch_shapes=[pltpu.VMEM((tm, LANES), F32), pltpu.VMEM((tm, d), F32), pltpu.VMEM((tm, d), F32)],
        compiler_params=_cparams(4), name="moe",
    )(h2, x1, logits, mod, w13, w2, fg)


def _pack_w_in(w):
    cut = C_KIW + IDX_DIM + N_IDX_HEADS
    pad = jnp.zeros((w.shape[0], C_GATE - cut), w.dtype)
    return jnp.concatenate([w[:, :cut], pad, w[:, cut:]], axis=1).astype(BF16)


def _chunk_w13(w1, w3, fc):
    *lead, d, f = w1.shape
    nc = f // fc
    a = w1.reshape(*lead, d, nc, fc)
    b = w3.reshape(*lead, d, nc, fc)
    ab = jnp.concatenate([a, b], axis=-1)
    return jnp.moveaxis(ab, -2, -3).astype(BF16)


def _chunk_w2(w2, fc):
    *lead, f, d = w2.shape
    return w2.reshape(*lead, f // fc, fc, d).astype(BF16)


def _router_terms(rw, rb):
    d = rw.shape[0]
    rwp = jnp.zeros((d, LANES), F32).at[:, :N_EXPERTS].set(rw)
    hi, mid, lo = _split3(rwp)
    rbp = jnp.zeros((1, LANES), F32).at[0, :N_EXPERTS].set(rb)
    return jnp.stack([hi, mid, lo]), rbp


def _trunk(x, mod_all, caches, p, *, tm, tq):
    b, t, d = x.shape
    depth = mod_all.shape[0]
    flat = caches is not None
    ks, vs, kis, tails = [], [], [], []
    for l in range(depth):
        mod = mod_all[l].reshape(b, 6, 1, d)
        if caches is None:
            conv_state = jnp.zeros((b, CONV_W - 1, D_CONV), F32)
        else:
            conv_state = caches[3][l]
        ya, q, k, v, qi, kiw, ki, sg, tail = _inproj(
            x, mod, p["norm1_g"][l], p["w_in"][l], p["conv_w"][l], conv_state, tm)
        if caches is None:
            k_all, v_all, ki_all = k, v, kiw
            pos0, l_valid = 0, t
            ao = _attention(q, qi, kiw, k_all, v_all, ki_all, tq=tq, pos0=0, l_valid=t,
                            topk=min(TOPK_MAX, t // 4))
        else:
            past = caches[0][l].shape[1]
            l_valid = past + t
            l_pad = -(-l_valid // KEY_BLOCK) * KEY_BLOCK
            cat = lambda old, new: jnp.concatenate(
                [old.reshape(b, past, -1), new, jnp.zeros((b, l_pad - l_valid, new.shape[-1]), F32)], axis=1)
            k_all, v_all = cat(caches[0][l], k), cat(caches[1][l], v)
            ki_all = jnp.pad(cat(caches[2][l], ki), ((0, 0), (0, 0), (0, LANES - IDX_DIM)))
            padq = lambda a: jnp.pad(a, ((0, 0), (0, tq - t), (0, 0)))
            ao = _attention(padq(q), padq(qi), padq(kiw), k_all, v_all, ki_all, tq=tq, pos0=past,
                            l_valid=l_valid, topk=min(TOPK_MAX, l_valid // 4))[:, :t]
        is_moe = l % 2 == 1
        router = (p["router_w"][l // 2], p["router_b"][l // 2]) if is_moe else None
        if flat:
            n = b * t
            fl = lambda a: a.reshape(1, n, a.shape[-1])
            modf = jnp.repeat(mod_all[l].reshape(b, 6, d), t, axis=0).transpose(1, 0, 2)[None]
            xs, yas, aos, sgs, tmf = fl(x), fl(ya), fl(ao), fl(sg), n
        else:
            modf, xs, yas, aos, sgs, tmf = mod, x, ya, ao, sg, tm
        outs = _merge(xs, yas, aos, sgs, modf, p["w_conv_out"][l], p["w_attn_out"][l], p["w_o"][l],
                      p["norm2_g"][l], router, tm=tmf)
        if is_moe:
            x1, h2, logits = outs
            x = _moe(h2, x1, logits, modf, p["moe_w13"][l // 2], p["moe_w2"][l // 2],
                     p["final_g"], tm=min(xs.shape[1], 1024), final_norm=l == depth - 1)
        else:
            x1, h2 = outs
            x = _ffn(h2, x1, modf, p["ffn_w13"][l // 2], p["ffn_w2"][l // 2], tm=tmf)
        x = x.reshape(b, t, d)
        ks.append(k.reshape(b, t, N_KV_HEADS, HEAD_DIM))
        vs.append(v.reshape(b, t, N_KV_HEADS, HEAD_DIM))
        kis.append(ki)
        tails.append(tail)
    return x, jnp.stack(ks), jnp.stack(vs), jnp.stack(kis), jnp.stack(tails)


def kernel(x_prompt, x_sample, c_prompt, c_sample, cache_k, cache_v, cache_idx_k, state_conv, w_ada, b_ada, norm1_g, w_in, conv_w, w_conv_out, w_attn_out, w_o, norm2_g, ffn_w1, ffn_w3, ffn_w2, router_w, router_b, moe_w1, moe_w3, moe_w2, final_g):
    depth = w_in.shape[0]
    bp = x_prompt.shape[0]
    rterms = [_router_terms(router_w[i], router_b[i]) for i in range(router_w.shape[0])]
    p = {
        "norm1_g": norm1_g.reshape(depth, 1, -1),
        "norm2_g": norm2_g.reshape(depth, 1, -1),
        "final_g": final_g.reshape(1, -1),
        "w_in": jnp.stack([_pack_w_in(w_in[l]) for l in range(depth)]),
        "conv_w": conv_w,
        "w_conv_out": w_conv_out.astype(BF16),
        "w_attn_out": w_attn_out.astype(BF16),
        "w_o": w_o.astype(BF16),
        "ffn_w13": _chunk_w13(ffn_w1, ffn_w3, FFN_CHUNK),
        "ffn_w2": _chunk_w2(ffn_w2, FFN_CHUNK),
        "moe_w13": _chunk_w13(moe_w1, moe_w3, MOE_CHUNK),
        "moe_w2": _chunk_w2(moe_w2, MOE_CHUNK),
        "router_w": [r[0] for r in rterms],
        "router_b": [r[1] for r in rterms],
    }
    mod_all = _adaln(jnp.concatenate([c_prompt, c_sample], axis=0), w_ada.astype(BF16), b_ada)
    seq = x_prompt.shape[1]
    tm = min(512, seq)
    tq = min(256, seq)
    yp, kp, vp, kip, cp = _trunk(x_prompt, mod_all[:, :bp], None, p, tm=tm, tq=tq)
    ts = x_sample.shape[1]
    ys, ksm, vsm, kism, csm = _trunk(x_sample, mod_all[:, bp:], (cache_k, cache_v, cache_idx_k, state_conv), p,
                                     tm=ts, tq=-(-ts // LANES) * LANES)
    return (yp, ys, kp, vp, kip, cp, ksm, vsm, kism, csm)
```

```python
import functools

import jax
import jax.numpy as jnp
from jax import lax
from jax.experimental import pallas as pl
from jax.experimental.pallas import tpu as pltpu

F32 = jnp.float32
BF16 = jnp.bfloat16
I32 = jnp.int32

D_MODEL = 1024
D_CONV = 512
CONV_W = 3
N_HEADS = 8
N_KV_HEADS = 4
HEAD_DIM = 64
GROUP = N_HEADS // N_KV_HEADS
N_IDX_HEADS = 16
IDX_DIM = 64
TOPK_MAX = 256
CHUNK = 64
D_FF = 2816
N_EXPERTS = 8
D_FF_EXPERT = 3584
EPS = 1e-6

D_Q = N_HEADS * HEAD_DIM
D_KV = N_KV_HEADS * HEAD_DIM
D_QI = N_IDX_HEADS * IDX_DIM
LANES = 128
C_CONV = 0
C_Q = 3 * D_CONV
C_KV = C_Q + D_Q
C_QI = C_KV + 2 * D_KV
C_KIW = C_QI + D_QI
C_GATE = C_KIW + LANES
D_PACK = C_GATE + 2 * D_MODEL
WI_SCALE = (IDX_DIM ** -0.5) * (N_IDX_HEADS ** -0.5)
Q_SCALE = HEAD_DIM ** -0.5

KEY_BLOCK = 256
FFN_CHUNK = 256
MOE_CHUNK = 512
MOE_ROWS = (D_FF_EXPERT // MOE_CHUNK) * 64
V7X_VMEM_LIMIT = 56 * 1024 * 1024

KEY_NEG_INF = -2139095041
KEY_MIN_FINITE = KEY_NEG_INF + 1
MASK_BIAS = -1e30


def _sigmoid(x):
    return 1.0 / (1.0 + jnp.exp(-x))


def _rmsnorm(x, g):
    ms = jnp.mean(x * x, axis=-1, keepdims=True)
    return x * lax.rsqrt(ms + EPS) * g


def _cparams(n_axes):
    return pltpu.CompilerParams(dimension_semantics=("arbitrary",) * n_axes,
                                vmem_limit_bytes=V7X_VMEM_LIMIT)


def _resident(shape):
    nd = len(shape)
    return pl.BlockSpec(shape, lambda *_: (0,) * nd, pipeline_mode=pl.Buffered(1))


def _adaln_kernel(c_ref, w_ref, b_ref, o_ref):
    c = c_ref[...]
    cond = (c * _sigmoid(c)).astype(BF16)
    o_ref[0] = jnp.dot(cond, w_ref[0], preferred_element_type=F32) + b_ref[0]


def _adaln(c_all, w_ada, b_ada):
    depth, d, n6 = w_ada.shape
    bc = c_all.shape[0]
    nblk = n6 // D_MODEL
    return pl.pallas_call(
        _adaln_kernel,
        grid=(depth, nblk),
        in_specs=[pl.BlockSpec((bc, d), lambda l, n: (0, 0)),
                  pl.BlockSpec((1, d, D_MODEL), lambda l, n: (l, 0, n)),
                  pl.BlockSpec((1, 1, D_MODEL), lambda l, n: (l, 0, n))],
        out_specs=pl.BlockSpec((1, bc, D_MODEL), lambda l, n: (l, 0, n)),
        out_shape=jax.ShapeDtypeStruct((depth, bc, n6), F32),
        compiler_params=_cparams(2), name="adaln",
    )(c_all, w_ada, b_ada.reshape(depth, 1, n6))


def _inproj_kernel(x_ref, mod_ref, g_ref, w_ref, cw_ref, st_ref,
                   ya_ref, q_ref, k_ref, v_ref, qi_ref, kiw_ref, ki_ref, sg_ref, tail_ref,
                   state_scr, *, tm):
    t = pl.program_id(1)

    @pl.when(t == 0)
    def _():
        state_scr[...] = st_ref[0]

    x = x_ref[0]
    h = _rmsnorm(x, g_ref[...]) * (1.0 + mod_ref[0, 1]) + mod_ref[0, 0]
    hb = h.astype(BF16)

    def seg(a, b):
        return jnp.dot(hb, w_ref[:, a:b], preferred_element_type=F32)

    bg = seg(C_CONV, C_CONV + D_CONV)
    u = seg(C_CONV + D_CONV, C_CONV + 2 * D_CONV) * seg(C_CONV + 2 * D_CONV, C_CONV + 3 * D_CONV)
    s0 = state_scr[0:1, :]
    s1 = state_scr[1:2, :]
    ri = lax.broadcasted_iota(I32, u.shape, 0)
    um1 = jnp.where(ri == 0, s1, pltpu.roll(u, 1, 0))
    um2 = jnp.where(ri == 0, s0, jnp.where(ri == 1, s1, pltpu.roll(u, 2, 0)))
    cw = cw_ref[...]
    y = cw[0:1] * um2
    y = y + cw[1:2] * um1
    y = y + cw[2:3] * u
    ya_ref[0] = (bg * y).astype(BF16)
    tail = u[tm - 2:tm, :]
    state_scr[...] = tail
    tail_ref[0] = tail

    q_ref[0] = (seg(C_Q, C_Q + D_Q) * Q_SCALE).astype(BF16)
    k_ref[0] = seg(C_KV, C_KV + D_KV)
    v_ref[0] = seg(C_KV + D_KV, C_KV + 2 * D_KV)
    half = D_QI // 2
    qi_ref[0, :, 0:half] = seg(C_QI, C_QI + half).astype(BF16)
    qi_ref[0, :, half:D_QI] = seg(C_QI + half, C_QI + D_QI).astype(BF16)
    kw = seg(C_KIW, C_KIW + LANES)
    ki_ref[0] = kw[:, 0:IDX_DIM]
    lane = lax.broadcasted_iota(I32, (1, LANES), 1)
    scale = jnp.where(lane < IDX_DIM, 1.0, jnp.where(lane < IDX_DIM + N_IDX_HEADS, WI_SCALE, 0.0))
    kiw_ref[0] = kw * scale

    for i in range(4):
        a = C_GATE + i * 512
        sg_ref[0, :, i * 512:(i + 1) * 512] = _sigmoid(seg(a, a + 512)).astype(BF16)


def _inproj(x, mod, g1, w_pack, conv_w, conv_state, tm):
    b, s, d = x.shape
    nt = s // tm
    row = lambda w, dt: jax.ShapeDtypeStruct((b, s, w), dt)
    rspec = lambda w: pl.BlockSpec((1, tm, w), lambda bi, ti: (bi, ti, 0))
    return pl.pallas_call(
        functools.partial(_inproj_kernel, tm=tm),
        grid=(b, nt),
        in_specs=[rspec(d),
                  pl.BlockSpec((1, 6, 1, d), lambda bi, ti: (bi, 0, 0, 0)),
                  _resident((1, d)),
                  _resident((d, D_PACK)),
                  _resident((CONV_W, D_CONV)),
                  pl.BlockSpec((1, CONV_W - 1, D_CONV), lambda bi, ti: (bi, 0, 0))],
        out_specs=[rspec(D_CONV), rspec(D_Q), rspec(D_KV), rspec(D_KV), rspec(D_QI),
                   rspec(LANES), rspec(IDX_DIM), rspec(2 * D_MODEL),
                   pl.BlockSpec((1, CONV_W - 1, D_CONV), lambda bi, ti: (bi, 0, 0))],
        out_shape=[row(D_CONV, BF16), row(D_Q, BF16), row(D_KV, F32), row(D_KV, F32),
                   row(D_QI, BF16), row(LANES, F32), row(IDX_DIM, F32), row(2 * D_MODEL, BF16),
                   jax.ShapeDtypeStruct((b, CONV_W - 1, D_CONV), F32)],
        scratch_shapes=[pltpu.VMEM((CONV_W - 1, D_CONV), F32)],
        compiler_params=_cparams(2), name="inproj",
    )(x, mod, g1, w_pack, conv_w, conv_state)


def _pair_blockdiag(tile, low_half):
    lane = lax.broadcasted_iota(I32, tile.shape, 1)
    lo = lane < HEAD_DIM
    swapped = pltpu.roll(tile, HEAD_DIM, 1)
    top = jnp.where(lo, tile if low_half else swapped, 0.0)
    bot = jnp.where(lo, 0.0, swapped if low_half else tile)
    return jnp.concatenate([top, bot], axis=0).astype(BF16)


def _attn_kernel(q_ref, qi_ref, kiw_ref, k_ref, v_ref, ki_ref, o_ref,
                 kbd_scr, vbd_scr, kibd_scr, key_scr, s_scr, m_scr, l_scr, acc_scr,
                 *, tq, pos0, l_valid, topk):
    kb = KEY_BLOCK
    j = pl.program_id(1)
    nkc_all = key_scr.shape[0]
    npair = N_IDX_HEADS // 2
    nt = (((1,), (1,)), ((), ()))

    @pl.when(j == 0)
    def _():
        def build(c, carry):
            rows = pl.ds(pl.multiple_of(c * kb, kb), kb)
            kibd_scr[c] = _pair_blockdiag(ki_ref[0, rows, :], True)
            for g in range(N_KV_HEADS):
                t = g // 2
                kbd_scr[g, c] = _pair_blockdiag(k_ref[0, rows, t * LANES:(t + 1) * LANES], g % 2 == 0)
                vbd_scr[g, c] = _pair_blockdiag(v_ref[0, rows, t * LANES:(t + 1) * LANES], g % 2 == 0)
            return carry

        lax.fori_loop(0, nkc_all, build, 0)

    w_t = kiw_ref[0].T

    row0 = pos0 + j * tq
    rows = row0 + lax.broadcasted_iota(I32, (1, tq), 1)
    lim = jnp.minimum(((rows >> 6) + 1) << 6, l_valid)
    last = row0 + tq - 1
    lim_max = jnp.minimum(((last >> 6) + 1) << 6, l_valid)
    nkc = (lim_max + kb - 1) >> 8

    def phase_a(c, carry):
        kbd = kibd_scr[c]
        acc = jnp.zeros((kb, tq), F32)
        for t in range(npair):
            d2 = lax.dot_general(kbd, qi_ref[0, :, t * LANES:(t + 1) * LANES], nt,
                                 preferred_element_type=F32)
            r = IDX_DIM + 2 * t
            acc = acc + w_t[r:r + 1, :] * jnp.maximum(d2[0:kb], 0.0)
            acc = acc + w_t[r + 1:r + 2, :] * jnp.maximum(d2[kb:2 * kb], 0.0)
        col = c * kb + lax.broadcasted_iota(I32, (kb, tq), 0)
        sc = jnp.where(col < lim, acc, -jnp.inf)
        bits = lax.bitcast_convert_type(sc, I32)
        key_scr[c] = bits ^ ((bits >> 31) & 0x7FFFFFFF)
        return carry

    lax.fori_loop(0, nkc, phase_a, 0)

    def count_ge(cand):
        def body(c, cnt):
            m = jnp.where(key_scr[c] >= cand, 1.0, 0.0)
            parts = [m[8 * i:8 * (i + 1)] for i in range(kb // 8)]
            while len(parts) > 1:
                parts = [parts[i] + parts[i + 1] for i in range(0, len(parts), 2)]
            return cnt + parts[0]
        cnt = lax.fori_loop(0, nkc, body, jnp.zeros((8, tq), F32))
        return jnp.sum(cnt, axis=0, keepdims=True)

    kf = float(topk)
    t0 = jnp.where(count_ge(jnp.zeros((1, tq), I32)) >= kf, 0, -2 ** 31).astype(I32)

    def bit_body(i, t):
        cand = t | lax.shift_left(jnp.int32(1), 30 - i)
        return jnp.where(count_ge(cand) >= kf, cand, t)

    thr = lax.fori_loop(0, 31, bit_body, t0)
    few = lim <= topk
    thr = jnp.where(few, KEY_MIN_FINITE, thr)

    tie = jnp.logical_and(count_ge(thr) > kf, jnp.logical_not(few))
    any_tie = jnp.max(jnp.where(tie, 1.0, 0.0)) > 0.0

    @pl.when(any_tie)
    def _():
        need = kf - count_ge(thr + 1)
        r_i = lax.broadcasted_iota(I32, (kb, kb), 0)
        c_i = lax.broadcasted_iota(I32, (kb, kb), 1)
        tri = jnp.where(c_i < r_i, 1.0, 0.0).astype(BF16)

        def body(c, run):
            kc = key_scr[c]
            eqf = jnp.where(kc == thr, 1.0, 0.0)
            rank = run + jnp.dot(tri, eqf.astype(BF16), preferred_element_type=F32)
            drop = jnp.logical_and(jnp.logical_and(tie, kc == thr), rank >= need)
            key_scr[c] = jnp.where(drop, KEY_NEG_INF, kc)
            return run + jnp.sum(eqf, axis=0, keepdims=True)

        lax.fori_loop(0, nkc, body, jnp.zeros((1, tq), F32))

    nl = kb // LANES
    m_scr[...] = jnp.full(m_scr.shape, MASK_BIAS, F32)
    l_scr[...] = jnp.zeros(l_scr.shape, F32)
    acc_scr[...] = jnp.zeros(acc_scr.shape, F32)

    def sweep1(c, carry):
        bias = jnp.where(key_scr[c] >= thr, 0.0, MASK_BIAS).T
        for g in range(N_KV_HEADS):
            s2 = lax.dot_general(q_ref[0, :, g * LANES:(g + 1) * LANES], kbd_scr[g, c], nt,
                                 preferred_element_type=F32)
            for r in range(GROUP):
                h = GROUP * g + r
                s = s2[:, r * kb:(r + 1) * kb] + bias
                s_scr[h, c] = s
                mx = m_scr[h]
                for i in range(nl):
                    mx = jnp.maximum(mx, s[:, i * LANES:(i + 1) * LANES])
                m_scr[h] = mx
        return carry

    lax.fori_loop(0, nkc, sweep1, 0)
    for h in range(N_HEADS):
        m_scr[h] = jnp.broadcast_to(jnp.max(m_scr[h], axis=1, keepdims=True), (tq, LANES))

    def sweep2(c, carry):
        for g in range(N_KV_HEADS):
            p2 = []
            for r in range(GROUP):
                h = GROUP * g + r
                s = s_scr[h, c]
                mb = m_scr[h]
                ps = [jnp.exp(s[:, i * LANES:(i + 1) * LANES] - mb) for i in range(nl)]
                psum = ps[0]
                for i in range(1, nl):
                    psum = psum + ps[i]
                l_scr[h] += psum
                p2 += ps
            acc_scr[g] += jnp.dot(jnp.concatenate(p2, axis=1).astype(BF16), vbd_scr[g, c],
                                  preferred_element_type=F32)
        return carry

    lax.fori_loop(0, nkc, sweep2, 0)
    low = lax.broadcasted_iota(I32, (tq, LANES), 1) < HEAD_DIM
    for g in range(N_KV_HEADS):
        l0 = jnp.sum(l_scr[GROUP * g], axis=1, keepdims=True)
        l1 = jnp.sum(l_scr[GROUP * g + 1], axis=1, keepdims=True)
        o_ref[0, :, g * LANES:(g + 1) * LANES] = (acc_scr[g] / jnp.where(low, l0, l1)).astype(BF16)


def _attention(q, qi, kiw, k_all, v_all, ki_all, *, tq, pos0, l_valid, topk):
    b, s, _ = q.shape
    l_pad = k_all.shape[1]
    nkc_max = l_pad // KEY_BLOCK
    qspec = lambda w: pl.BlockSpec((1, tq, w), lambda bi, ji: (bi, ji, 0))
    kspec = lambda w: pl.BlockSpec((1, l_pad, w), lambda bi, ji: (bi, 0, 0))
    return pl.pallas_call(
        functools.partial(_attn_kernel, tq=tq, pos0=pos0, l_valid=l_valid, topk=topk),
        grid=(b, s // tq),
        in_specs=[qspec(D_Q), qspec(D_QI), qspec(LANES), kspec(D_KV), kspec(D_KV), kspec(LANES)],
        out_specs=qspec(D_Q),
        out_shape=jax.ShapeDtypeStruct((b, s, D_Q), BF16),
        scratch_shapes=[pltpu.VMEM((N_KV_HEADS, nkc_max, 2 * KEY_BLOCK, LANES), BF16),
                        pltpu.VMEM((N_KV_HEADS, nkc_max, 2 * KEY_BLOCK, LANES), BF16),
                        pltpu.VMEM((nkc_max, 2 * KEY_BLOCK, LANES), BF16),
                        pltpu.VMEM((nkc_max, KEY_BLOCK, tq), I32),
                        pltpu.VMEM((N_HEADS, nkc_max, tq, KEY_BLOCK), F32),
                        pltpu.VMEM((N_HEADS, tq, LANES), F32),
                        pltpu.VMEM((N_HEADS, tq, LANES), F32),
                        pltpu.VMEM((N_KV_HEADS, tq, LANES), F32)],
        compiler_params=_cparams(2), name="attention",
    )(q, qi, kiw, k_all, v_all, ki_all)


def _split3(a):
    hi = a.astype(BF16)
    r = a - hi.astype(F32)
    mid = r.astype(BF16)
    lo = (r - mid.astype(F32)).astype(BF16)
    return hi, mid, lo


def _merge_kernel(*refs, with_router):
    if with_router:
        (x_ref, ya_ref, ao_ref, sg_ref, mod_ref, wc_ref, wa_ref, wo_ref, g2_ref, rw_ref, rb_ref,
         x1_ref, h2_ref, lg_ref) = refs
    else:
        (x_ref, ya_ref, ao_ref, sg_ref, mod_ref, wc_ref, wa_ref, wo_ref, g2_ref,
         x1_ref, h2_ref) = refs
    a = jnp.dot(ya_ref[0], wc_ref[...], preferred_element_type=F32)
    b = jnp.dot(ao_ref[0], wa_ref[...], preferred_element_type=F32)
    merged = sg_ref[0, :, 0:D_MODEL].astype(F32) * a + sg_ref[0, :, D_MODEL:2 * D_MODEL].astype(F32) * b
    mix = jnp.dot(merged.astype(BF16), wo_ref[...], preferred_element_type=F32)
    x1 = x_ref[0] + mod_ref[0, 2] * mix
    x1_ref[0] = x1
    h2 = _rmsnorm(x1, g2_ref[...]) * (1.0 + mod_ref[0, 4]) + mod_ref[0, 3]
    h2_ref[0] = h2.astype(h2_ref.dtype)
    if with_router:
        hs = _split3(h2)
        acc = jnp.zeros((h2.shape[0], LANES), F32)
        for i in range(3):
            for jj in range(3 - i):
                acc = acc + jnp.dot(hs[i], rw_ref[jj], preferred_element_type=F32)
        lane = lax.broadcasted_iota(I32, acc.shape, 1)
        lg = jnp.where(lane < N_EXPERTS, acc + rb_ref[...], -jnp.inf)
        m1 = jnp.max(lg, axis=1, keepdims=True)
        i1 = jnp.min(jnp.where(lg == m1, lane, LANES), axis=1, keepdims=True)
        rest = jnp.where(lane == i1, -jnp.inf, lg)
        m2 = jnp.max(rest, axis=1, keepdims=True)
        i2 = jnp.min(jnp.where(rest == m2, lane, LANES), axis=1, keepdims=True)
        e2 = jnp.exp(m2 - m1)
        den = 1.0 + e2
        lg_ref[0] = jnp.where(lane == 0, i1.astype(F32),
                              jnp.where(lane == 1, i2.astype(F32),
                                        jnp.where(lane == 2, 1.0 / den,
                                                  jnp.where(lane == 3, e2 / den, 0.0))))


def _merge(x, ya, ao, sg, mod, wc, wa, wo, g2, router=None, *, tm):
    b, s, d = x.shape
    r = mod.shape[2]
    rspec = lambda w: pl.BlockSpec((1, tm, w), lambda bi, ti: (bi, ti, 0))
    mspec = (pl.BlockSpec((1, 6, 1, d), lambda bi, ti: (bi, 0, 0, 0)) if r == 1 else
             pl.BlockSpec((1, 6, tm, d), lambda bi, ti: (bi, 0, ti, 0)))
    in_specs = [rspec(d), rspec(D_CONV), rspec(D_Q), rspec(2 * d), mspec,
                _resident((D_CONV, d)), _resident((D_Q, d)), _resident((d, d)), _resident((1, d))]
    args = [x, ya, ao, sg, mod, wc, wa, wo, g2]
    out_specs = [rspec(d), rspec(d)]
    out_shape = [jax.ShapeDtypeStruct((b, s, d), F32),
                 jax.ShapeDtypeStruct((b, s, d), BF16 if router is None else F32)]
    if router is not None:
        in_specs += [_resident((3, d, LANES)), _resident((1, LANES))]
        args += list(router)
        out_specs.append(rspec(LANES))
        out_shape.append(jax.ShapeDtypeStruct((b, s, LANES), F32))
    return pl.pallas_call(
        functools.partial(_merge_kernel, with_router=router is not None),
        grid=(b, s // tm),
        in_specs=in_specs, out_specs=out_specs, out_shape=out_shape,
        compiler_params=_cparams(2), name="merge",
    )(*args)


def _ffn_kernel(h2_ref, x1_ref, mod_ref, w13_ref, w2_ref, o_ref, acc_scr, *, nc, fc):
    h = h2_ref[0]
    acc_scr[...] = jnp.zeros(acc_scr.shape, F32)

    def body(c, carry):
        ab = jnp.dot(h, w13_ref[c], preferred_element_type=F32)
        a = ab[:, 0:fc]
        hm = (a * _sigmoid(a) * ab[:, fc:2 * fc]).astype(BF16)
        acc_scr[...] += jnp.dot(hm, w2_ref[c], preferred_element_type=F32)
        return carry

    lax.fori_loop(0, nc, body, 0)
    o_ref[0] = x1_ref[0] + mod_ref[0, 5] * acc_scr[...]


def _ffn(h2, x1, mod, w13, w2, *, tm):
    b, s, d = x1.shape
    nc, _, fc2 = w13.shape
    r = mod.shape[2]
    rspec = lambda w: pl.BlockSpec((1, tm, w), lambda bi, ti: (bi, ti, 0))
    mspec = (pl.BlockSpec((1, 6, 1, d), lambda bi, ti: (bi, 0, 0, 0)) if r == 1 else
             pl.BlockSpec((1, 6, tm, d), lambda bi, ti: (bi, 0, ti, 0)))
    return pl.pallas_call(
        functools.partial(_ffn_kernel, nc=nc, fc=fc2 // 2),
        grid=(b, s // tm),
        in_specs=[rspec(d), rspec(d), mspec, _resident(w13.shape), _resident(w2.shape)],
        out_specs=rspec(d),
        out_shape=jax.ShapeDtypeStruct((b, s, d), F32),
        scratch_shapes=[pltpu.VMEM((tm, d), F32)],
        compiler_params=_cparams(2), name="ffn",
    )(h2, x1, mod, w13, w2)


def _swiglu_chunk(xb, w13, w2, fc):
    ab = jnp.dot(xb, w13, preferred_element_type=F32)
    a = ab[:, 0:fc]
    hm = (a * _sigmoid(a) * ab[:, fc:2 * fc]).astype(BF16)
    return jnp.dot(hm, w2, preferred_element_type=F32)


def _moe_dense_kernel(h2_ref, x1_ref, rt_ref, mod_ref, w13_ref, w2_ref, fg_ref, o_ref,
                      eacc_scr, acc_scr, *, fc, final_norm):
    e = pl.program_id(2)
    c = pl.program_id(3)
    ne = pl.num_programs(2)
    nc = pl.num_programs(3)

    @pl.when(jnp.logical_and(e == 0, c == 0))
    def _():
        acc_scr[...] = jnp.zeros(acc_scr.shape, F32)

    @pl.when(c == 0)
    def _():
        eacc_scr[...] = jnp.zeros(eacc_scr.shape, F32)

    eacc_scr[...] += _swiglu_chunk(h2_ref[0].astype(BF16), w13_ref[0, 0], w2_ref[0, 0], fc)

    @pl.when(c == nc - 1)
    def _():
        rt = rt_ref[0]
        ef = e.astype(F32)
        gate = (jnp.where(rt[:, 0:1] == ef, rt[:, 2:3], 0.0)
                + jnp.where(rt[:, 1:2] == ef, rt[:, 3:4], 0.0))
        acc_scr[...] += gate * eacc_scr[...]

    @pl.when(jnp.logical_and(e == ne - 1, c == nc - 1))
    def _():
        x2 = x1_ref[0] + mod_ref[0, 5] * acc_scr[...]
        o_ref[0] = _rmsnorm(x2, fg_ref[...]) if final_norm else x2


def _moe_dense(h2, x1, route, mod, w13, w2, fg, *, tm, final_norm):
    b, s, d = x1.shape
    ne, nc, _, fc2 = w13.shape
    r = mod.shape[2]
    rspec = lambda w: pl.BlockSpec((1, tm, w), lambda bi, ti, ei, ci: (bi, ti, 0))
    mspec = (pl.BlockSpec((1, 6, 1, d), lambda bi, ti, ei, ci: (bi, 0, 0, 0)) if r == 1 else
             pl.BlockSpec((1, 6, tm, d), lambda bi, ti, ei, ci: (bi, 0, ti, 0)))
    return pl.pallas_call(
        functools.partial(_moe_dense_kernel, fc=fc2 // 2, final_norm=final_norm),
        grid=(b, s // tm, ne, nc),
        in_specs=[rspec(d), rspec(d), rspec(LANES), mspec,
                  pl.BlockSpec((1, 1, d, fc2), lambda bi, ti, ei, ci: (ei, ci, 0, 0)),
                  pl.BlockSpec((1, 1, fc2 // 2, d), lambda bi, ti, ei, ci: (ei, ci, 0, 0)),
                  _resident((1, d))],
        out_specs=rspec(d),
        out_shape=jax.ShapeDtypeStruct((b, s, d), F32),
        scratch_shapes=[pltpu.VMEM((tm, d), F32), pltpu.VMEM((tm, d), F32)],
        compiler_params=_cparams(4), name="moe_dense",
    )(h2, x1, route, mod, w13, w2, fg)


def _route_tables(route, tmr):
    n = route.shape[0]
    e1 = route[:, 0].astype(I32)
    e2 = route[:, 1].astype(I32)
    ar = jnp.arange(N_EXPERTS, dtype=I32)
    oh1 = e1[:, None] == ar
    oh2 = e2[:, None] == ar
    oh = oh1.astype(I32) + oh2.astype(I32)
    cum = jnp.cumsum(oh, axis=0)
    excl = cum - oh
    cnt = cum[-1]
    tiles_e = (cnt + tmr - 1) // tmr
    tile_end = jnp.cumsum(tiles_e)
    tile_start = tile_end - tiles_e
    row_off = tile_start * tmr
    d1 = row_off[e1] + jnp.sum(jnp.where(oh1, excl, 0), axis=1)
    d2 = row_off[e2] + jnp.sum(jnp.where(oh2, excl, 0), axis=1)
    nt = -(-2 * n // tmr) + N_EXPERTS
    tok = jnp.arange(n, dtype=I32)
    src = jnp.zeros((nt * tmr,), I32).at[d1].set(tok).at[d2].set(tok)
    dst = jnp.zeros((nt * tmr,), I32).at[d1].set(2 * tok).at[d2].set(2 * tok + 1)
    tile = jnp.arange(nt, dtype=I32)
    te = jnp.minimum(jnp.searchsorted(tile_end, tile, side="right"), N_EXPERTS - 1).astype(I32)
    nv = jnp.where(tile < tile_end[-1], jnp.clip(cnt[te] - (tile - tile_start[te]) * tmr, 0, tmr), 0)
    return (src.reshape(nt, 1, tmr), dst.reshape(nt, 1, tmr), te, nv.astype(I32),
            tile_end[-1:].astype(I32))


def _moe_routed_kernel(te_ref, nv_ref, nu_ref, src0_ref, srcn_ref, dstp_ref, h2_hbm, w13_ref, w2_ref,
                       out_hbm, xbuf, xb_scr, ybuf, gsem, ssem, *, tmr, nc, fc):
    i = pl.program_id(0)
    c = pl.program_id(1)
    nu = nu_ref[0]
    slot = i % 2
    other = 1 - slot
    rpc = tmr // nc

    def gather(src_ref, cc, k, sl):
        return pltpu.make_async_copy(h2_hbm.at[pl.ds(src_ref[0, 0, cc * rpc + k], 1)],
                                     xbuf.at[sl, cc, pl.ds(k, 1)], gsem.at[sl])

    def scatter(cc, k, sl):
        return pltpu.make_async_copy(ybuf.at[sl, cc, pl.ds(k, 1)],
                                     out_hbm.at[pl.ds(dstp_ref[0, 0, cc * rpc + k], 1)], ssem.at[sl])

    @pl.when(jnp.logical_and(i == 0, c == 0))
    def _():
        def body(cc, carry):
            for k in range(rpc):
                gather(src0_ref, cc, k, 0).start()
            return carry
        lax.fori_loop(0, nc, body, 0)

    @pl.when(jnp.logical_and(c == 0, i < nu))
    def _():
        pltpu.make_async_copy(xbuf.at[slot], xbuf.at[slot], gsem.at[slot]).wait()
        xb_scr[...] = xbuf[slot].reshape(tmr, xb_scr.shape[1]).astype(BF16)

    drain = jnp.logical_and(c == 0, jnp.logical_and(i >= 2, i - 2 < nu))
    last = nv_ref.shape[0] - 1
    nv_drain = nv_ref[jnp.clip(i - 2, 0, last)]

    @pl.when(jnp.logical_and(drain, nv_drain == tmr))
    def _():
        pltpu.make_async_copy(ybuf.at[slot], ybuf.at[slot], ssem.at[slot]).wait()

    @pl.when(jnp.logical_and(drain, nv_drain < tmr))
    def _():
        def body(r, carry):
            scatter(0, 0, slot).wait()
            return carry
        lax.fori_loop(0, nv_drain, body, 0)

    @pl.when(i + 1 < nu)
    def _():
        for k in range(rpc):
            gather(srcn_ref, c, k, other).start()

    push = jnp.logical_and(i >= 1, i - 1 < nu)
    nv_push = nv_ref[jnp.clip(i - 1, 0, last)]

    @pl.when(jnp.logical_and(push, nv_push == tmr))
    def _():
        for k in range(rpc):
            scatter(c, k, other).start()

    @pl.when(jnp.logical_and(push, nv_push < tmr))
    def _():
        def body(k, carry):
            scatter(c, k, other).start()
            return carry
        lax.fori_loop(0, jnp.clip(nv_push - c * rpc, 0, rpc), body, 0)

    @pl.when(i < nu)
    def _():
        y = _swiglu_chunk(xb_scr[...], w13_ref[0, 0], w2_ref[0, 0], fc).reshape(nc, rpc, xb_scr.shape[1])

        @pl.when(c == 0)
        def _():
            ybuf[slot] = y

        @pl.when(c > 0)
        def _():
            ybuf[slot] += y


def _moe_routed(h2, route, w13, w2, *, tmr):
    n, d = h2.shape
    ne, nc, _, fc2 = w13.shape
    src, dst, te, nv, nu = _route_tables(route, tmr)
    nt = src.shape[0]
    last = nt - 1
    smem = lambda imap: pl.BlockSpec((1, 1, tmr), imap, memory_space=pltpu.SMEM)
    wmap = lambda i, c, te_r, nv_r, nu_r: (te_r[jnp.minimum(i, last)],
                                           jnp.where(i < nu_r[0], c, nc - 1), 0, 0)
    return pl.pallas_call(
        functools.partial(_moe_routed_kernel, tmr=tmr, nc=nc, fc=fc2 // 2),
        grid_spec=pltpu.PrefetchScalarGridSpec(
            num_scalar_prefetch=3,
            grid=(nt + 2, nc),
            in_specs=[smem(lambda i, c, te_r, nv_r, nu_r: (0, 0, 0)),
                      smem(lambda i, c, te_r, nv_r, nu_r: (jnp.minimum(i + 1, last), 0, 0)),
                      smem(lambda i, c, te_r, nv_r, nu_r: (jnp.clip(i - 1, 0, last), 0, 0)),
                      pl.BlockSpec(memory_space=pl.ANY),
                      pl.BlockSpec((1, 1, d, fc2), wmap),
                      pl.BlockSpec((1, 1, fc2 // 2, d), wmap)],
            out_specs=pl.BlockSpec(memory_space=pl.ANY),
            scratch_shapes=[pltpu.VMEM((2, nc, tmr // nc, d), F32), pltpu.VMEM((tmr, d), BF16),
                            pltpu.VMEM((2, nc, tmr // nc, d), F32),
                            pltpu.SemaphoreType.DMA((2,)), pltpu.SemaphoreType.DMA((2,))]),
        out_shape=jax.ShapeDtypeStruct((2 * n, d), F32),
        compiler_params=_cparams(2), name="moe_routed",
    )(te, nv, nu, src, src, dst, h2, w13, w2)


def _moe_combine_kernel(x1_ref, y2_ref, rt_ref, mod_ref, fg_ref, o_ref, *, final_norm):
    d = x1_ref.shape[-1]
    rt = rt_ref[0]
    f = rt[:, 2:3] * y2_ref[:, 0:d] + rt[:, 3:4] * y2_ref[:, d:2 * d]
    x2 = x1_ref[0] + mod_ref[0, 5] * f
    o_ref[0] = _rmsnorm(x2, fg_ref[...]) if final_norm else x2


def _moe_combine(x1, y2, route, mod, fg, *, tm, final_norm):
    b, s, d = x1.shape
    nt = s // tm
    rspec = lambda w: pl.BlockSpec((1, tm, w), lambda bi, ti: (bi, ti, 0))
    return pl.pallas_call(
        functools.partial(_moe_combine_kernel, final_norm=final_norm),
        grid=(b, nt),
        in_specs=[rspec(d), pl.BlockSpec((tm, 2 * d), lambda bi, ti: (bi * nt + ti, 0)), rspec(LANES),
                  pl.BlockSpec((1, 6, 1, d), lambda bi, ti: (bi, 0, 0, 0)), _resident((1, d))],
        out_specs=rspec(d),
        out_shape=jax.ShapeDtypeStruct((b, s, d), F32),
        compiler_params=_cparams(2), name="moe_combine",
    )(x1, y2, route, mod, fg)


def _pack_w_in(w):
    cut = C_KIW + IDX_DIM + N_IDX_HEADS
    pad = jnp.zeros((w.shape[0], C_GATE - cut), w.dtype)
    return jnp.concatenate([w[:, :cut], pad, w[:, cut:]], axis=1).astype(BF16)


def _chunk_w13(w1, w3, fc):
    *lead, d, f = w1.shape
    nc = f // fc
    a = w1.reshape(*lead, d, nc, fc)
    b = w3.reshape(*lead, d, nc, fc)
    ab = jnp.concatenate([a, b], axis=-1)
    return jnp.moveaxis(ab, -2, -3).astype(BF16)


def _chunk_w2(w2, fc):
    *lead, f, d = w2.shape
    return w2.reshape(*lead, f // fc, fc, d).astype(BF16)


def _router_terms(rw, rb):
    d = rw.shape[0]
    rwp = jnp.zeros((d, LANES), F32).at[:, :N_EXPERTS].set(rw)
    hi, mid, lo = _split3(rwp)
    rbp = jnp.zeros((1, LANES), F32).at[0, :N_EXPERTS].set(rb)
    return jnp.stack([hi, mid, lo]), rbp


def _trunk(x, mod_all, caches, p, *, tm, tq):
    b, t, d = x.shape
    depth = mod_all.shape[0]
    flat = caches is not None
    ks, vs, kis, tails = [], [], [], []
    for l in range(depth):
        mod = mod_all[l].reshape(b, 6, 1, d)
        if caches is None:
            conv_state = jnp.zeros((b, CONV_W - 1, D_CONV), F32)
        else:
            conv_state = caches[3][l]
        ya, q, k, v, qi, kiw, ki, sg, tail = _inproj(
            x, mod, p["norm1_g"][l], p["w_in"][l], p["conv_w"][l], conv_state, tm)
        if caches is None:
            k_all, v_all, ki_all = k, v, kiw
            pos0, l_valid = 0, t
            ao = _attention(q, qi, kiw, k_all, v_all, ki_all, tq=tq, pos0=0, l_valid=t,
                            topk=min(TOPK_MAX, t // 4))
        else:
            past = caches[0][l].shape[1]
            l_valid = past + t
            l_pad = -(-l_valid // KEY_BLOCK) * KEY_BLOCK
            cat = lambda old, new: jnp.concatenate(
                [old.reshape(b, past, -1), new, jnp.zeros((b, l_pad - l_valid, new.shape[-1]), F32)], axis=1)
            k_all, v_all = cat(caches[0][l], k), cat(caches[1][l], v)
            ki_all = jnp.pad(cat(caches[2][l], ki), ((0, 0), (0, 0), (0, LANES - IDX_DIM)))
            padq = lambda a: jnp.pad(a, ((0, 0), (0, tq - t), (0, 0)))
            ao = _attention(padq(q), padq(qi), padq(kiw), k_all, v_all, ki_all, tq=tq, pos0=past,
                            l_valid=l_valid, topk=min(TOPK_MAX, l_valid // 4))[:, :t]
        is_moe = l % 2 == 1
        router = (p["router_w"][l // 2], p["router_b"][l // 2]) if is_moe else None
        if flat:
            n = b * t
            fl = lambda a: a.reshape(1, n, a.shape[-1])
            modf = jnp.repeat(mod_all[l].reshape(b, 6, d), t, axis=0).transpose(1, 0, 2)[None]
            xs, yas, aos, sgs, tmf = fl(x), fl(ya), fl(ao), fl(sg), n
        else:
            modf, xs, yas, aos, sgs, tmf = mod, x, ya, ao, sg, tm
        outs = _merge(xs, yas, aos, sgs, modf, p["w_conv_out"][l], p["w_attn_out"][l], p["w_o"][l],
                      p["norm2_g"][l], router, tm=tmf)
        if is_moe and flat:
            x1, h2, route = outs
            x = _moe_dense(h2, x1, route, modf, p["moe_w13"][l // 2], p["moe_w2"][l // 2],
                           p["final_g"], tm=xs.shape[1], final_norm=l == depth - 1)
        elif is_moe:
            x1, h2, route = outs
            y2 = _moe_routed(h2.reshape(b * t, d), route.reshape(b * t, LANES),
                             p["moe_w13"][l // 2], p["moe_w2"][l // 2], tmr=MOE_ROWS)
            x = _moe_combine(x1, y2.reshape(-1, 2 * d), route, mod, p["final_g"], tm=tm,
                             final_norm=l == depth - 1)
        else:
            x1, h2 = outs
            x = _ffn(h2, x1, modf, p["ffn_w13"][l // 2], p["ffn_w2"][l // 2], tm=tmf)
        x = x.reshape(b, t, d)
        ks.append(k.reshape(b, t, N_KV_HEADS, HEAD_DIM))
        vs.append(v.reshape(b, t, N_KV_HEADS, HEAD_DIM))
        kis.append(ki)
        tails.append(tail)
    return x, jnp.stack(ks), jnp.stack(vs), jnp.stack(kis), jnp.stack(tails)


def kernel(x_prompt, x_sample, c_prompt, c_sample, cache_k, cache_v, cache_idx_k, state_conv, w_ada, b_ada, norm1_g, w_in, conv_w, w_conv_out, w_attn_out, w_o, norm2_g, ffn_w1, ffn_w3, ffn_w2, router_w, router_b, moe_w1, moe_w3, moe_w2, final_g):
    depth = w_in.shape[0]
    bp = x_prompt.shape[0]
    rterms = [_router_terms(router_w[i], router_b[i]) for i in range(router_w.shape[0])]
    p = {
        "norm1_g": norm1_g.reshape(depth, 1, -1),
        "norm2_g": norm2_g.reshape(depth, 1, -1),
        "final_g": final_g.reshape(1, -1),
        "w_in": jnp.stack([_pack_w_in(w_in[l]) for l in range(depth)]),
        "conv_w": conv_w,
        "w_conv_out": w_conv_out.astype(BF16),
        "w_attn_out": w_attn_out.astype(BF16),
        "w_o": w_o.astype(BF16),
        "ffn_w13": _chunk_w13(ffn_w1, ffn_w3, FFN_CHUNK),
        "ffn_w2": _chunk_w2(ffn_w2, FFN_CHUNK),
        "moe_w13": _chunk_w13(moe_w1, moe_w3, MOE_CHUNK),
        "moe_w2": _chunk_w2(moe_w2, MOE_CHUNK),
        "router_w": [r[0] for r in rterms],
        "router_b": [r[1] for r in rterms],
    }
    mod_all = _adaln(jnp.concatenate([c_prompt, c_sample], axis=0), w_ada.astype(BF16), b_ada)
    seq = x_prompt.shape[1]
    tm = min(512, seq)
    tq = min(256, seq)
    yp, kp, vp, kip, cp = _trunk(x_prompt, mod_all[:, :bp], None, p, tm=tm, tq=tq)
    ts = x_sample.shape[1]
    ys, ksm, vsm, kism, csm = _trunk(x_sample, mod_all[:, bp:], (cache_k, cache_v, cache_idx_k, state_conv), p,
                                     tm=ts, tq=-(-ts // LANES) * LANES)
    return (yp, ys, kp, vp, kip, cp, ksm, vsm, kism, csm)
```

```python
import functools

import jax
import jax.numpy as jnp
from jax import lax
from jax.experimental import pallas as pl
from jax.experimental.pallas import tpu as pltpu

F32 = jnp.float32
BF16 = jnp.bfloat16
I32 = jnp.int32
I16 = jnp.int16

D_MODEL = 1024
D_CONV = 512
CONV_W = 3
N_HEADS = 8
N_KV_HEADS = 4
HEAD_DIM = 64
GROUP = N_HEADS // N_KV_HEADS
N_IDX_HEADS = 16
IDX_DIM = 64
TOPK_MAX = 256
CHUNK = 64
D_FF = 2816
N_EXPERTS = 8
D_FF_EXPERT = 3584
EPS = 1e-6

D_Q = N_HEADS * HEAD_DIM
D_KV = N_KV_HEADS * HEAD_DIM
D_QI = N_IDX_HEADS * IDX_DIM
LANES = 128
C_CONV = 0
C_Q = 3 * D_CONV
C_KV = C_Q + D_Q
C_QI = C_KV + 2 * D_KV
C_KIW = C_QI + D_QI
C_GATE = C_KIW + LANES
D_PACK = C_GATE + 2 * D_MODEL
WI_SCALE = (IDX_DIM ** -0.5) * (N_IDX_HEADS ** -0.5)
Q_SCALE = HEAD_DIM ** -0.5

KEY_BLOCK = 256
FFN_CHUNK = 256
MOE_CHUNK = 512
MOE_ROWS = (D_FF_EXPERT // MOE_CHUNK) * 64
V7X_VMEM_LIMIT = 56 * 1024 * 1024

KEY_NEG_INF = -2139095041
KEY_MIN_FINITE = KEY_NEG_INF + 1
MASK_BIAS = -1e30


def _sigmoid(x):
    return 1.0 / (1.0 + jnp.exp(-x))


def _rmsnorm(x, g):
    ms = jnp.mean(x * x, axis=-1, keepdims=True)
    return x * lax.rsqrt(ms + EPS) * g


def _cparams(n_axes):
    return pltpu.CompilerParams(dimension_semantics=("arbitrary",) * n_axes,
                                vmem_limit_bytes=V7X_VMEM_LIMIT)


def _resident(shape):
    nd = len(shape)
    return pl.BlockSpec(shape, lambda *_: (0,) * nd, pipeline_mode=pl.Buffered(1))


def _adaln_kernel(c_ref, w_ref, b_ref, o_ref):
    c = c_ref[...]
    cond = (c * _sigmoid(c)).astype(BF16)
    o_ref[0] = jnp.dot(cond, w_ref[0], preferred_element_type=F32) + b_ref[0]


def _adaln(c_all, w_ada, b_ada):
    depth, d, n6 = w_ada.shape
    bc = c_all.shape[0]
    nblk = n6 // D_MODEL
    return pl.pallas_call(
        _adaln_kernel,
        grid=(depth, nblk),
        in_specs=[pl.BlockSpec((bc, d), lambda l, n: (0, 0)),
                  pl.BlockSpec((1, d, D_MODEL), lambda l, n: (l, 0, n)),
                  pl.BlockSpec((1, 1, D_MODEL), lambda l, n: (l, 0, n))],
        out_specs=pl.BlockSpec((1, bc, D_MODEL), lambda l, n: (l, 0, n)),
        out_shape=jax.ShapeDtypeStruct((depth, bc, n6), F32),
        compiler_params=_cparams(2), name="adaln",
    )(c_all, w_ada, b_ada.reshape(depth, 1, n6))


def _inproj_kernel(x_ref, mod_ref, g_ref, w_ref, cw_ref, st_ref,
                   ya_ref, q_ref, k_ref, v_ref, qi_ref, kiw_ref, ki_ref, sg_ref, tail_ref,
                   state_scr, *, tm):
    t = pl.program_id(1)

    @pl.when(t == 0)
    def _():
        state_scr[...] = st_ref[0]

    x = x_ref[0]
    h = _rmsnorm(x, g_ref[...]) * (1.0 + mod_ref[0, 1]) + mod_ref[0, 0]
    hb = h.astype(BF16)

    def seg(a, b):
        return jnp.dot(hb, w_ref[:, a:b], preferred_element_type=F32)

    bg = seg(C_CONV, C_CONV + D_CONV)
    u = seg(C_CONV + D_CONV, C_CONV + 2 * D_CONV) * seg(C_CONV + 2 * D_CONV, C_CONV + 3 * D_CONV)
    s0 = state_scr[0:1, :]
    s1 = state_scr[1:2, :]
    ri = lax.broadcasted_iota(I32, u.shape, 0)
    um1 = jnp.where(ri == 0, s1, pltpu.roll(u, 1, 0))
    um2 = jnp.where(ri == 0, s0, jnp.where(ri == 1, s1, pltpu.roll(u, 2, 0)))
    cw = cw_ref[...]
    y = cw[0:1] * um2
    y = y + cw[1:2] * um1
    y = y + cw[2:3] * u
    ya_ref[0] = (bg * y).astype(BF16)
    tail = u[tm - 2:tm, :]
    state_scr[...] = tail
    tail_ref[0] = tail

    q_ref[0] = (seg(C_Q, C_Q + D_Q) * Q_SCALE).astype(BF16)
    k_ref[0] = seg(C_KV, C_KV + D_KV)
    v_ref[0] = seg(C_KV + D_KV, C_KV + 2 * D_KV)
    half = D_QI // 2
    qi_ref[0, :, 0:half] = seg(C_QI, C_QI + half).astype(BF16)
    qi_ref[0, :, half:D_QI] = seg(C_QI + half, C_QI + D_QI).astype(BF16)
    kw = seg(C_KIW, C_KIW + LANES)
    ki_ref[0] = kw[:, 0:IDX_DIM]
    lane = lax.broadcasted_iota(I32, (1, LANES), 1)
    scale = jnp.where(lane < IDX_DIM, 1.0, jnp.where(lane < IDX_DIM + N_IDX_HEADS, WI_SCALE, 0.0))
    kiw_ref[0] = kw * scale

    for i in range(4):
        a = C_GATE + i * 512
        sg_ref[0, :, i * 512:(i + 1) * 512] = _sigmoid(seg(a, a + 512)).astype(BF16)


def _inproj(x, mod, g1, w_pack, conv_w, conv_state, tm):
    b, s, d = x.shape
    nt = s // tm
    row = lambda w, dt: jax.ShapeDtypeStruct((b, s, w), dt)
    rspec = lambda w: pl.BlockSpec((1, tm, w), lambda bi, ti: (bi, ti, 0))
    return pl.pallas_call(
        functools.partial(_inproj_kernel, tm=tm),
        grid=(b, nt),
        in_specs=[rspec(d),
                  pl.BlockSpec((1, 6, 1, d), lambda bi, ti: (bi, 0, 0, 0)),
                  _resident((1, d)),
                  _resident((d, D_PACK)),
                  _resident((CONV_W, D_CONV)),
                  pl.BlockSpec((1, CONV_W - 1, D_CONV), lambda bi, ti: (bi, 0, 0))],
        out_specs=[rspec(D_CONV), rspec(D_Q), rspec(D_KV), rspec(D_KV), rspec(D_QI),
                   rspec(LANES), rspec(IDX_DIM), rspec(2 * D_MODEL),
                   pl.BlockSpec((1, CONV_W - 1, D_CONV), lambda bi, ti: (bi, 0, 0))],
        out_shape=[row(D_CONV, BF16), row(D_Q, BF16), row(D_KV, F32), row(D_KV, F32),
                   row(D_QI, BF16), row(LANES, F32), row(IDX_DIM, F32), row(2 * D_MODEL, BF16),
                   jax.ShapeDtypeStruct((b, CONV_W - 1, D_CONV), F32)],
        scratch_shapes=[pltpu.VMEM((CONV_W - 1, D_CONV), F32)],
        compiler_params=_cparams(2), name="inproj",
    )(x, mod, g1, w_pack, conv_w, conv_state)


def _pair_blockdiag(tile, low_half):
    lane = lax.broadcasted_iota(I32, tile.shape, 1)
    lo = lane < HEAD_DIM
    swapped = pltpu.roll(tile, HEAD_DIM, 1)
    top = jnp.where(lo, tile if low_half else swapped, 0.0)
    bot = jnp.where(lo, 0.0, swapped if low_half else tile)
    return jnp.concatenate([top, bot], axis=0).astype(BF16)


def _attn_kernel(q_ref, qi_ref, kiw_ref, k_ref, v_ref, ki_ref, o_ref,
                 kbd_scr, vbd_scr, kibd_scr, key_scr, hi_scr, lo_scr, s_scr, m_scr, l_scr, acc_scr,
                 *, tq, pos0, l_valid, topk):
    kb = KEY_BLOCK
    j = pl.program_id(1)
    nkc_all = key_scr.shape[0]
    npair = N_IDX_HEADS // 2
    nt = (((1,), (1,)), ((), ()))

    @pl.when(j == 0)
    def _():
        def build(c, carry):
            rows = pl.ds(pl.multiple_of(c * kb, kb), kb)
            kibd_scr[c] = _pair_blockdiag(ki_ref[0, rows, :], True)
            for g in range(N_KV_HEADS):
                t = g // 2
                kbd_scr[g, c] = _pair_blockdiag(k_ref[0, rows, t * LANES:(t + 1) * LANES], g % 2 == 0)
                vbd_scr[g, c] = _pair_blockdiag(v_ref[0, rows, t * LANES:(t + 1) * LANES], g % 2 == 0)
            return carry

        lax.fori_loop(0, nkc_all, build, 0)

    w_t = kiw_ref[0].T

    row0 = pos0 + j * tq
    rows = row0 + lax.broadcasted_iota(I32, (1, tq), 1)
    lim = jnp.minimum(((rows >> 6) + 1) << 6, l_valid)
    last = row0 + tq - 1
    lim_max = jnp.minimum(((last >> 6) + 1) << 6, l_valid)
    nkc = (lim_max + kb - 1) >> 8

    def phase_a(c, carry):
        kbd = kibd_scr[c]
        acc = jnp.zeros((kb, tq), F32)
        for t in range(npair):
            d2 = lax.dot_general(kbd, qi_ref[0, :, t * LANES:(t + 1) * LANES], nt,
                                 preferred_element_type=F32)
            r = IDX_DIM + 2 * t
            acc = acc + w_t[r:r + 1, :] * jnp.maximum(d2[0:kb], 0.0)
            acc = acc + w_t[r + 1:r + 2, :] * jnp.maximum(d2[kb:2 * kb], 0.0)
        col = c * kb + lax.broadcasted_iota(I32, (kb, tq), 0)
        sc = jnp.where(col < lim, acc, -jnp.inf)
        bits = lax.bitcast_convert_type(sc, I32)
        key = bits ^ ((bits >> 31) & 0x7FFFFFFF)
        key_scr[c] = key
        hi_scr[c] = (key >> 16).astype(I16)
        lo_scr[c] = ((key & 0xFFFF) - 32768).astype(I16)
        return carry

    lax.fori_loop(0, nkc, phase_a, 0)

    def count_ge(cand):
        def body(c, cnt):
            m = jnp.where(key_scr[c] >= cand, 1.0, 0.0)
            parts = [m[8 * i:8 * (i + 1)] for i in range(kb // 8)]
            while len(parts) > 1:
                parts = [parts[i] + parts[i + 1] for i in range(0, len(parts), 2)]
            return cnt + parts[0]
        cnt = lax.fori_loop(0, nkc, body, jnp.zeros((8, tq), F32))
        return jnp.sum(cnt, axis=0, keepdims=True)

    def count16(ref, pred):
        def body(c, cnt):
            m = jnp.where(pred(ref[c]), jnp.int16(1), jnp.int16(0))
            parts = [m[16 * i:16 * (i + 1)] for i in range(kb // 16)]
            while len(parts) > 1:
                parts = [parts[i] + parts[i + 1] for i in range(0, len(parts), 2)]
            return cnt + parts[0]
        cnt = lax.fori_loop(0, nkc, body, jnp.zeros((16, tq), I16))
        return jnp.sum(cnt.astype(I32), axis=0, keepdims=True)

    def radix16(ref, need):
        t0 = jnp.where(count16(ref, lambda h: h >= jnp.int16(0)) >= need, 0, -32768).astype(I32)

        def bit_body(i, t):
            cand = t | lax.shift_left(jnp.int32(1), 14 - i)
            c16 = cand.astype(I16)
            return jnp.where(count16(ref, lambda h: h >= c16) >= need, cand, t)

        return lax.fori_loop(0, 15, bit_body, t0)

    t_hi = radix16(hi_scr, topk)
    h16 = t_hi.astype(I16)
    need_lo = topk - count16(hi_scr, lambda h: h > h16)

    def keep_bucket(c, carry):
        lo_scr[c] = jnp.where(hi_scr[c] == h16, lo_scr[c], jnp.int16(-32768))
        return carry

    lax.fori_loop(0, nkc, keep_bucket, 0)
    t_lo = radix16(lo_scr, need_lo)
    thr = (t_hi << 16) | ((t_lo + 32768) & 0xFFFF)
    kf = float(topk)
    few = lim <= topk
    thr = jnp.where(few, KEY_MIN_FINITE, thr)

    tie = jnp.logical_and(count_ge(thr) > kf, jnp.logical_not(few))
    any_tie = jnp.max(jnp.where(tie, 1.0, 0.0)) > 0.0

    @pl.when(any_tie)
    def _():
        need = kf - count_ge(thr + 1)
        r_i = lax.broadcasted_iota(I32, (kb, kb), 0)
        c_i = lax.broadcasted_iota(I32, (kb, kb), 1)
        tri = jnp.where(c_i < r_i, 1.0, 0.0).astype(BF16)

        def body(c, run):
            kc = key_scr[c]
            eqf = jnp.where(kc == thr, 1.0, 0.0)
            rank = run + jnp.dot(tri, eqf.astype(BF16), preferred_element_type=F32)
            drop = jnp.logical_and(jnp.logical_and(tie, kc == thr), rank >= need)
            key_scr[c] = jnp.where(drop, KEY_NEG_INF, kc)
            return run + jnp.sum(eqf, axis=0, keepdims=True)

        lax.fori_loop(0, nkc, body, jnp.zeros((1, tq), F32))

    nl = kb // LANES
    m_scr[...] = jnp.full(m_scr.shape, MASK_BIAS, F32)
    l_scr[...] = jnp.zeros(l_scr.shape, F32)
    acc_scr[...] = jnp.zeros(acc_scr.shape, F32)

    def sweep1(c, carry):
        bias = jnp.where(key_scr[c] >= thr, 0.0, MASK_BIAS).T
        for g in range(N_KV_HEADS):
            s2 = lax.dot_general(q_ref[0, :, g * LANES:(g + 1) * LANES], kbd_scr[g, c], nt,
                                 preferred_element_type=F32)
            for r in range(GROUP):
                h = GROUP * g + r
                s = s2[:, r * kb:(r + 1) * kb] + bias
                s_scr[h, c] = s
                mx = m_scr[h]
                for i in range(nl):
                    mx = jnp.maximum(mx, s[:, i * LANES:(i + 1) * LANES])
                m_scr[h] = mx
        return carry

    lax.fori_loop(0, nkc, sweep1, 0)
    for h in range(N_HEADS):
        m_scr[h] = jnp.broadcast_to(jnp.max(m_scr[h], axis=1, keepdims=True), (tq, LANES))

    def sweep2(c, carry):
        for g in range(N_KV_HEADS):
            p2 = []
            for r in range(GROUP):
                h = GROUP * g + r
                s = s_scr[h, c]
                mb = m_scr[h]
                ps = [jnp.exp(s[:, i * LANES:(i + 1) * LANES] - mb) for i in range(nl)]
                psum = ps[0]
                for i in range(1, nl):
                    psum = psum + ps[i]
                l_scr[h] += psum
                p2 += ps
            acc_scr[g] += jnp.dot(jnp.concatenate(p2, axis=1).astype(BF16), vbd_scr[g, c],
                                  preferred_element_type=F32)
        return carry

    lax.fori_loop(0, nkc, sweep2, 0)
    low = lax.broadcasted_iota(I32, (tq, LANES), 1) < HEAD_DIM
    for g in range(N_KV_HEADS):
        l0 = jnp.sum(l_scr[GROUP * g], axis=1, keepdims=True)
        l1 = jnp.sum(l_scr[GROUP * g + 1], axis=1, keepdims=True)
        o_ref[0, :, g * LANES:(g + 1) * LANES] = (acc_scr[g] / jnp.where(low, l0, l1)).astype(BF16)


def _attention(q, qi, kiw, k_all, v_all, ki_all, *, tq, pos0, l_valid, topk):
    b, s, _ = q.shape
    l_pad = k_all.shape[1]
    nkc_max = l_pad // KEY_BLOCK
    qspec = lambda w: pl.BlockSpec((1, tq, w), lambda bi, ji: (bi, ji, 0))
    kspec = lambda w: pl.BlockSpec((1, l_pad, w), lambda bi, ji: (bi, 0, 0))
    return pl.pallas_call(
        functools.partial(_attn_kernel, tq=tq, pos0=pos0, l_valid=l_valid, topk=topk),
        grid=(b, s // tq),
        in_specs=[qspec(D_Q), qspec(D_QI), qspec(LANES), kspec(D_KV), kspec(D_KV), kspec(LANES)],
        out_specs=qspec(D_Q),
        out_shape=jax.ShapeDtypeStruct((b, s, D_Q), BF16),
        scratch_shapes=[pltpu.VMEM((N_KV_HEADS, nkc_max, 2 * KEY_BLOCK, LANES), BF16),
                        pltpu.VMEM((N_KV_HEADS, nkc_max, 2 * KEY_BLOCK, LANES), BF16),
                        pltpu.VMEM((nkc_max, 2 * KEY_BLOCK, LANES), BF16),
                        pltpu.VMEM((nkc_max, KEY_BLOCK, tq), I32),
                        pltpu.VMEM((nkc_max, KEY_BLOCK, tq), I16),
                        pltpu.VMEM((nkc_max, KEY_BLOCK, tq), I16),
                        pltpu.VMEM((N_HEADS, nkc_max, tq, KEY_BLOCK), F32),
                        pltpu.VMEM((N_HEADS, tq, LANES), F32),
                        pltpu.VMEM((N_HEADS, tq, LANES), F32),
                        pltpu.VMEM((N_KV_HEADS, tq, LANES), F32)],
        compiler_params=_cparams(2), name="attention",
    )(q, qi, kiw, k_all, v_all, ki_all)


def _split3(a):
    hi = a.astype(BF16)
    r = a - hi.astype(F32)
    mid = r.astype(BF16)
    lo = (r - mid.astype(F32)).astype(BF16)
    return hi, mid, lo


def _merge_kernel(*refs, with_router):
    if with_router:
        (x_ref, ya_ref, ao_ref, sg_ref, mod_ref, wc_ref, wa_ref, wo_ref, g2_ref, rw_ref, rb_ref,
         x1_ref, h2_ref, lg_ref) = refs
    else:
        (x_ref, ya_ref, ao_ref, sg_ref, mod_ref, wc_ref, wa_ref, wo_ref, g2_ref,
         x1_ref, h2_ref) = refs
    a = jnp.dot(ya_ref[0], wc_ref[...], preferred_element_type=F32)
    b = jnp.dot(ao_ref[0], wa_ref[...], preferred_element_type=F32)
    merged = sg_ref[0, :, 0:D_MODEL].astype(F32) * a + sg_ref[0, :, D_MODEL:2 * D_MODEL].astype(F32) * b
    mix = jnp.dot(merged.astype(BF16), wo_ref[...], preferred_element_type=F32)
    x1 = x_ref[0] + mod_ref[0, 2] * mix
    x1_ref[0] = x1
    h2 = _rmsnorm(x1, g2_ref[...]) * (1.0 + mod_ref[0, 4]) + mod_ref[0, 3]
    h2_ref[0] = h2.astype(h2_ref.dtype)
    if with_router:
        hs = _split3(h2)
        acc = jnp.zeros((h2.shape[0], LANES), F32)
        for i in range(3):
            for jj in range(3 - i):
                acc = acc + jnp.dot(hs[i], rw_ref[jj], preferred_element_type=F32)
        lane = lax.broadcasted_iota(I32, acc.shape, 1)
        lg = jnp.where(lane < N_EXPERTS, acc + rb_ref[...], -jnp.inf)
        m1 = jnp.max(lg, axis=1, keepdims=True)
        i1 = jnp.min(jnp.where(lg == m1, lane, LANES), axis=1, keepdims=True)
        rest = jnp.where(lane == i1, -jnp.inf, lg)
        m2 = jnp.max(rest, axis=1, keepdims=True)
        i2 = jnp.min(jnp.where(rest == m2, lane, LANES), axis=1, keepdims=True)
        e2 = jnp.exp(m2 - m1)
        den = 1.0 + e2
        lg_ref[0] = jnp.where(lane == 0, i1.astype(F32),
                              jnp.where(lane == 1, i2.astype(F32),
                                        jnp.where(lane == 2, 1.0 / den,
                                                  jnp.where(lane == 3, e2 / den, 0.0))))


def _merge(x, ya, ao, sg, mod, wc, wa, wo, g2, router=None, *, tm):
    b, s, d = x.shape
    r = mod.shape[2]
    rspec = lambda w: pl.BlockSpec((1, tm, w), lambda bi, ti: (bi, ti, 0))
    mspec = (pl.BlockSpec((1, 6, 1, d), lambda bi, ti: (bi, 0, 0, 0)) if r == 1 else
             pl.BlockSpec((1, 6, tm, d), lambda bi, ti: (bi, 0, ti, 0)))
    in_specs = [rspec(d), rspec(D_CONV), rspec(D_Q), rspec(2 * d), mspec,
                _resident((D_CONV, d)), _resident((D_Q, d)), _resident((d, d)), _resident((1, d))]
    args = [x, ya, ao, sg, mod, wc, wa, wo, g2]
    out_specs = [rspec(d), rspec(d)]
    out_shape = [jax.ShapeDtypeStruct((b, s, d), F32),
                 jax.ShapeDtypeStruct((b, s, d), BF16 if router is None else F32)]
    if router is not None:
        in_specs += [_resident((3, d, LANES)), _resident((1, LANES))]
        args += list(router)
        out_specs.append(rspec(LANES))
        out_shape.append(jax.ShapeDtypeStruct((b, s, LANES), F32))
    return pl.pallas_call(
        functools.partial(_merge_kernel, with_router=router is not None),
        grid=(b, s // tm),
        in_specs=in_specs, out_specs=out_specs, out_shape=out_shape,
        compiler_params=_cparams(2), name="merge",
    )(*args)


def _ffn_kernel(h2_ref, x1_ref, mod_ref, w13_ref, w2_ref, o_ref, acc_scr, *, nc, fc):
    h = h2_ref[0]
    acc_scr[...] = jnp.zeros(acc_scr.shape, F32)

    def body(c, carry):
        ab = jnp.dot(h, w13_ref[c], preferred_element_type=F32)
        a = ab[:, 0:fc]
        hm = (a * _sigmoid(a) * ab[:, fc:2 * fc]).astype(BF16)
        acc_scr[...] += jnp.dot(hm, w2_ref[c], preferred_element_type=F32)
        return carry

    lax.fori_loop(0, nc, body, 0)
    o_ref[0] = x1_ref[0] + mod_ref[0, 5] * acc_scr[...]


def _ffn(h2, x1, mod, w13, w2, *, tm):
    b, s, d = x1.shape
    nc, _, fc2 = w13.shape
    r = mod.shape[2]
    rspec = lambda w: pl.BlockSpec((1, tm, w), lambda bi, ti: (bi, ti, 0))
    mspec = (pl.BlockSpec((1, 6, 1, d), lambda bi, ti: (bi, 0, 0, 0)) if r == 1 else
             pl.BlockSpec((1, 6, tm, d), lambda bi, ti: (bi, 0, ti, 0)))
    return pl.pallas_call(
        functools.partial(_ffn_kernel, nc=nc, fc=fc2 // 2),
        grid=(b, s // tm),
        in_specs=[rspec(d), rspec(d), mspec, _resident(w13.shape), _resident(w2.shape)],
        out_specs=rspec(d),
        out_shape=jax.ShapeDtypeStruct((b, s, d), F32),
        scratch_shapes=[pltpu.VMEM((tm, d), F32)],
        compiler_params=_cparams(2), name="ffn",
    )(h2, x1, mod, w13, w2)


def _swiglu_chunk(xb, w13, w2, fc):
    ab = jnp.dot(xb, w13, preferred_element_type=F32)
    a = ab[:, 0:fc]
    hm = (a * _sigmoid(a) * ab[:, fc:2 * fc]).astype(BF16)
    return jnp.dot(hm, w2, preferred_element_type=F32)


def _moe_dense_kernel(h2_ref, x1_ref, rt_ref, mod_ref, w13_ref, w2_ref, fg_ref, o_ref,
                      eacc_scr, acc_scr, *, fc, final_norm):
    e = pl.program_id(2)
    c = pl.program_id(3)
    ne = pl.num_programs(2)
    nc = pl.num_programs(3)

    @pl.when(jnp.logical_and(e == 0, c == 0))
    def _():
        acc_scr[...] = jnp.zeros(acc_scr.shape, F32)

    @pl.when(c == 0)
    def _():
        eacc_scr[...] = jnp.zeros(eacc_scr.shape, F32)

    eacc_scr[...] += _swiglu_chunk(h2_ref[0].astype(BF16), w13_ref[0, 0], w2_ref[0, 0], fc)

    @pl.when(c == nc - 1)
    def _():
        rt = rt_ref[0]
        ef = e.astype(F32)
        gate = (jnp.where(rt[:, 0:1] == ef, rt[:, 2:3], 0.0)
                + jnp.where(rt[:, 1:2] == ef, rt[:, 3:4], 0.0))
        acc_scr[...] += gate * eacc_scr[...]

    @pl.when(jnp.logical_and(e == ne - 1, c == nc - 1))
    def _():
        x2 = x1_ref[0] + mod_ref[0, 5] * acc_scr[...]
        o_ref[0] = _rmsnorm(x2, fg_ref[...]) if final_norm else x2


def _moe_dense(h2, x1, route, mod, w13, w2, fg, *, tm, final_norm):
    b, s, d = x1.shape
    ne, nc, _, fc2 = w13.shape
    r = mod.shape[2]
    rspec = lambda w: pl.BlockSpec((1, tm, w), lambda bi, ti, ei, ci: (bi, ti, 0))
    mspec = (pl.BlockSpec((1, 6, 1, d), lambda bi, ti, ei, ci: (bi, 0, 0, 0)) if r == 1 else
             pl.BlockSpec((1, 6, tm, d), lambda bi, ti, ei, ci: (bi, 0, ti, 0)))
    return pl.pallas_call(
        functools.partial(_moe_dense_kernel, fc=fc2 // 2, final_norm=final_norm),
        grid=(b, s // tm, ne, nc),
        in_specs=[rspec(d), rspec(d), rspec(LANES), mspec,
                  pl.BlockSpec((1, 1, d, fc2), lambda bi, ti, ei, ci: (ei, ci, 0, 0)),
                  pl.BlockSpec((1, 1, fc2 // 2, d), lambda bi, ti, ei, ci: (ei, ci, 0, 0)),
                  _resident((1, d))],
        out_specs=rspec(d),
        out_shape=jax.ShapeDtypeStruct((b, s, d), F32),
        scratch_shapes=[pltpu.VMEM((tm, d), F32), pltpu.VMEM((tm, d), F32)],
        compiler_params=_cparams(4), name="moe_dense",
    )(h2, x1, route, mod, w13, w2, fg)


def _route_tables(route, tmr):
    n = route.shape[0]
    e1 = route[:, 0].astype(I32)
    e2 = route[:, 1].astype(I32)
    ar = jnp.arange(N_EXPERTS, dtype=I32)
    oh1 = e1[:, None] == ar
    oh2 = e2[:, None] == ar
    oh = oh1.astype(I32) + oh2.astype(I32)
    cum = jnp.cumsum(oh, axis=0)
    excl = cum - oh
    cnt = cum[-1]
    tiles_e = (cnt + tmr - 1) // tmr
    tile_end = jnp.cumsum(tiles_e)
    tile_start = tile_end - tiles_e
    row_off = tile_start * tmr
    d1 = row_off[e1] + jnp.sum(jnp.where(oh1, excl, 0), axis=1)
    d2 = row_off[e2] + jnp.sum(jnp.where(oh2, excl, 0), axis=1)
    nt = -(-2 * n // tmr) + N_EXPERTS
    tok = jnp.arange(n, dtype=I32)
    pair = jnp.zeros((nt * tmr,), I32).at[jnp.concatenate([d1, d2])].set(
        jnp.concatenate([2 * tok, 2 * tok + 1]), unique_indices=True)
    src = pair >> 1
    dst = (pair & 1) * n + src
    tile = jnp.arange(nt, dtype=I32)
    te = jnp.minimum(jnp.searchsorted(tile_end, tile, side="right"), N_EXPERTS - 1).astype(I32)
    nv = jnp.where(tile < tile_end[-1], jnp.clip(cnt[te] - (tile - tile_start[te]) * tmr, 0, tmr), 0)
    return (src.reshape(nt, 1, tmr), dst.reshape(nt, 1, tmr), te, nv.astype(I32),
            tile_end[-1:].astype(I32))


def _moe_routed_kernel(te_ref, nv_ref, nu_ref, src0_ref, srcn_ref, dstp_ref, h2_hbm, w13_ref, w2_ref,
                       out_hbm, xbuf, xb_scr, ybuf, gsem, ssem, *, tmr, nc, fc):
    i = pl.program_id(0)
    c = pl.program_id(1)
    nu = nu_ref[0]
    slot = i % 2
    other = 1 - slot
    rpc = tmr // nc

    def gather(src_ref, cc, k, sl):
        return pltpu.make_async_copy(h2_hbm.at[pl.ds(src_ref[0, 0, cc * rpc + k], 1)],
                                     xbuf.at[sl, cc, pl.ds(k, 1)], gsem.at[sl])

    def scatter(cc, k, sl):
        return pltpu.make_async_copy(ybuf.at[sl, cc, pl.ds(k, 1)],
                                     out_hbm.at[pl.ds(dstp_ref[0, 0, cc * rpc + k], 1)], ssem.at[sl])

    @pl.when(jnp.logical_and(i == 0, c == 0))
    def _():
        def body(cc, carry):
            for k in range(rpc):
                gather(src0_ref, cc, k, 0).start()
            return carry
        lax.fori_loop(0, nc, body, 0)

    @pl.when(jnp.logical_and(c == 0, i < nu))
    def _():
        pltpu.make_async_copy(xbuf.at[slot], xbuf.at[slot], gsem.at[slot]).wait()
        xb_scr[...] = xbuf[slot].reshape(tmr, xb_scr.shape[1]).astype(BF16)

    drain = jnp.logical_and(c == 0, jnp.logical_and(i >= 2, i - 2 < nu))
    last = nv_ref.shape[0] - 1
    nv_drain = nv_ref[jnp.clip(i - 2, 0, last)]

    @pl.when(jnp.logical_and(drain, nv_drain == tmr))
    def _():
        pltpu.make_async_copy(ybuf.at[slot], ybuf.at[slot], ssem.at[slot]).wait()

    @pl.when(jnp.logical_and(drain, nv_drain < tmr))
    def _():
        def body(r, carry):
            scatter(0, 0, slot).wait()
            return carry
        lax.fori_loop(0, nv_drain, body, 0)

    @pl.when(i + 1 < nu)
    def _():
        for k in range(rpc):
            gather(srcn_ref, c, k, other).start()

    push = jnp.logical_and(i >= 1, i - 1 < nu)
    nv_push = nv_ref[jnp.clip(i - 1, 0, last)]

    @pl.when(jnp.logical_and(push, nv_push == tmr))
    def _():
        for k in range(rpc):
            scatter(c, k, other).start()

    @pl.when(jnp.logical_and(push, nv_push < tmr))
    def _():
        def body(k, carry):
            scatter(c, k, other).start()
            return carry
        lax.fori_loop(0, jnp.clip(nv_push - c * rpc, 0, rpc), body, 0)

    @pl.when(i < nu)
    def _():
        y = _swiglu_chunk(xb_scr[...], w13_ref[0, 0], w2_ref[0, 0], fc).reshape(nc, rpc, xb_scr.shape[1])

        @pl.when(c == 0)
        def _():
            ybuf[slot] = y

        @pl.when(c > 0)
        def _():
            ybuf[slot] += y


def _moe_routed(h2, route, w13, w2, *, tmr):
    n, d = h2.shape
    ne, nc, _, fc2 = w13.shape
    src, dst, te, nv, nu = _route_tables(route, tmr)
    nt = src.shape[0]
    last = nt - 1
    smem = lambda imap: pl.BlockSpec((1, 1, tmr), imap, memory_space=pltpu.SMEM)
    wmap = lambda i, c, te_r, nv_r, nu_r: (te_r[jnp.minimum(i, last)],
                                           jnp.where(i < nu_r[0], c, nc - 1), 0, 0)
    return pl.pallas_call(
        functools.partial(_moe_routed_kernel, tmr=tmr, nc=nc, fc=fc2 // 2),
        grid_spec=pltpu.PrefetchScalarGridSpec(
            num_scalar_prefetch=3,
            grid=(nt + 2, nc),
            in_specs=[smem(lambda i, c, te_r, nv_r, nu_r: (0, 0, 0)),
                      smem(lambda i, c, te_r, nv_r, nu_r: (jnp.minimum(i + 1, last), 0, 0)),
                      smem(lambda i, c, te_r, nv_r, nu_r: (jnp.clip(i - 1, 0, last), 0, 0)),
                      pl.BlockSpec(memory_space=pl.ANY),
                      pl.BlockSpec((1, 1, d, fc2), wmap),
                      pl.BlockSpec((1, 1, fc2 // 2, d), wmap)],
            out_specs=pl.BlockSpec(memory_space=pl.ANY),
            scratch_shapes=[pltpu.VMEM((2, nc, tmr // nc, d), F32), pltpu.VMEM((tmr, d), BF16),
                            pltpu.VMEM((2, nc, tmr // nc, d), F32),
                            pltpu.SemaphoreType.DMA((2,)), pltpu.SemaphoreType.DMA((2,))]),
        out_shape=jax.ShapeDtypeStruct((2 * n, d), F32),
        compiler_params=_cparams(2), name="moe_routed",
    )(te, nv, nu, src, src, dst, h2, w13, w2)


def _moe_combine_kernel(x1_ref, ya_ref, yb_ref, rt_ref, mod_ref, fg_ref, o_ref, *, final_norm):
    rt = rt_ref[0]
    f = rt[:, 2:3] * ya_ref[...] + rt[:, 3:4] * yb_ref[...]
    x2 = x1_ref[0] + mod_ref[0, 5] * f
    o_ref[0] = _rmsnorm(x2, fg_ref[...]) if final_norm else x2


def _moe_combine(x1, y2, route, mod, fg, *, tm, final_norm):
    b, s, d = x1.shape
    nt = s // tm
    rspec = lambda w: pl.BlockSpec((1, tm, w), lambda bi, ti: (bi, ti, 0))
    return pl.pallas_call(
        functools.partial(_moe_combine_kernel, final_norm=final_norm),
        grid=(b, nt),
        in_specs=[rspec(d),
                  pl.BlockSpec((tm, d), lambda bi, ti: (bi * nt + ti, 0)),
                  pl.BlockSpec((tm, d), lambda bi, ti: (b * nt + bi * nt + ti, 0)),
                  rspec(LANES),
                  pl.BlockSpec((1, 6, 1, d), lambda bi, ti: (bi, 0, 0, 0)), _resident((1, d))],
        out_specs=rspec(d),
        out_shape=jax.ShapeDtypeStruct((b, s, d), F32),
        compiler_params=_cparams(2), name="moe_combine",
    )(x1, y2, y2, route, mod, fg)


def _pack_w_in(w):
    cut = C_KIW + IDX_DIM + N_IDX_HEADS
    pad = jnp.zeros((w.shape[0], C_GATE - cut), w.dtype)
    return jnp.concatenate([w[:, :cut], pad, w[:, cut:]], axis=1).astype(BF16)


def _chunk_w13(w1, w3, fc):
    *lead, d, f = w1.shape
    nc = f // fc
    a = w1.reshape(*lead, d, nc, fc)
    b = w3.reshape(*lead, d, nc, fc)
    ab = jnp.concatenate([a, b], axis=-1)
    return jnp.moveaxis(ab, -2, -3).astype(BF16)


def _chunk_w2(w2, fc):
    *lead, f, d = w2.shape
    return w2.reshape(*lead, f // fc, fc, d).astype(BF16)


def _router_terms(rw, rb):
    d = rw.shape[0]
    rwp = jnp.zeros((d, LANES), F32).at[:, :N_EXPERTS].set(rw)
    hi, mid, lo = _split3(rwp)
    rbp = jnp.zeros((1, LANES), F32).at[0, :N_EXPERTS].set(rb)
    return jnp.stack([hi, mid, lo]), rbp


def _trunk(x, mod_all, caches, p, *, tm, tq):
    b, t, d = x.shape
    depth = mod_all.shape[0]
    flat = caches is not None
    ks, vs, kis, tails = [], [], [], []
    for l in range(depth):
        mod = mod_all[l].reshape(b, 6, 1, d)
        if caches is None:
            conv_state = jnp.zeros((b, CONV_W - 1, D_CONV), F32)
        else:
            conv_state = caches[3][l]
        ya, q, k, v, qi, kiw, ki, sg, tail = _inproj(
            x, mod, p["norm1_g"][l], p["w_in"][l], p["conv_w"][l], conv_state, tm)
        if caches is None:
            k_all, v_all, ki_all = k, v, kiw
            pos0, l_valid = 0, t
            ao = _attention(q, qi, kiw, k_all, v_all, ki_all, tq=tq, pos0=0, l_valid=t,
                            topk=min(TOPK_MAX, t // 4))
        else:
            past = caches[0][l].shape[1]
            l_valid = past + t
            l_pad = -(-l_valid // KEY_BLOCK) * KEY_BLOCK
            cat = lambda old, new: jnp.concatenate(
                [old.reshape(b, past, -1), new, jnp.zeros((b, l_pad - l_valid, new.shape[-1]), F32)], axis=1)
            k_all, v_all = cat(caches[0][l], k), cat(caches[1][l], v)
            ki_all = jnp.pad(cat(caches[2][l], ki), ((0, 0), (0, 0), (0, LANES - IDX_DIM)))
            padq = lambda a: jnp.pad(a, ((0, 0), (0, tq - t), (0, 0)))
            ao = _attention(padq(q), padq(qi), padq(kiw), k_all, v_all, ki_all, tq=tq, pos0=past,
                            l_valid=l_valid, topk=min(TOPK_MAX, l_valid // 4))[:, :t]
        is_moe = l % 2 == 1
        router = (p["router_w"][l // 2], p["router_b"][l // 2]) if is_moe else None
        if flat:
            n = b * t
            fl = lambda a: a.reshape(1, n, a.shape[-1])
            modf = jnp.repeat(mod_all[l].reshape(b, 6, d), t, axis=0).transpose(1, 0, 2)[None]
            xs, yas, aos, sgs, tmf = fl(x), fl(ya), fl(ao), fl(sg), n
        else:
            modf, xs, yas, aos, sgs, tmf = mod, x, ya, ao, sg, tm
        outs = _merge(xs, yas, aos, sgs, modf, p["w_conv_out"][l], p["w_attn_out"][l], p["w_o"][l],
                      p["norm2_g"][l], router, tm=tmf)
        if is_moe and flat:
            x1, h2, route = outs
            x = _moe_dense(h2, x1, route, modf, p["moe_w13"][l // 2], p["moe_w2"][l // 2],
                           p["final_g"], tm=xs.shape[1], final_norm=l == depth - 1)
        elif is_moe:
            x1, h2, route = outs
            y2 = _moe_routed(h2.reshape(b * t, d), route.reshape(b * t, LANES),
                             p["moe_w13"][l // 2], p["moe_w2"][l // 2], tmr=MOE_ROWS)
            x = _moe_combine(x1, y2, route, mod, p["final_g"], tm=tm,
                             final_norm=l == depth - 1)
        else:
            x1, h2 = outs
            x = _ffn(h2, x1, modf, p["ffn_w13"][l // 2], p["ffn_w2"][l // 2], tm=tmf)
        x = x.reshape(b, t, d)
        ks.append(k.reshape(b, t, N_KV_HEADS, HEAD_DIM))
        vs.append(v.reshape(b, t, N_KV_HEADS, HEAD_DIM))
        kis.append(ki)
        tails.append(tail)
    return x, jnp.stack(ks), jnp.stack(vs), jnp.stack(kis), jnp.stack(tails)


def kernel(x_prompt, x_sample, c_prompt, c_sample, cache_k, cache_v, cache_idx_k, state_conv, w_ada, b_ada, norm1_g, w_in, conv_w, w_conv_out, w_attn_out, w_o, norm2_g, ffn_w1, ffn_w3, ffn_w2, router_w, router_b, moe_w1, moe_w3, moe_w2, final_g):
    depth = w_in.shape[0]
    bp = x_prompt.shape[0]
    rterms = [_router_terms(router_w[i], router_b[i]) for i in range(router_w.shape[0])]
    p = {
        "norm1_g": norm1_g.reshape(depth, 1, -1),
        "norm2_g": norm2_g.reshape(depth, 1, -1),
        "final_g": final_g.reshape(1, -1),
        "w_in": jnp.stack([_pack_w_in(w_in[l]) for l in range(depth)]),
        "conv_w": conv_w,
        "w_conv_out": w_conv_out.astype(BF16),
        "w_attn_out": w_attn_out.astype(BF16),
        "w_o": w_o.astype(BF16),
        "ffn_w13": _chunk_w13(ffn_w1, ffn_w3, FFN_CHUNK),
        "ffn_w2": _chunk_w2(ffn_w2, FFN_CHUNK),
        "moe_w13": _chunk_w13(moe_w1, moe_w3, MOE_CHUNK),
        "moe_w2": _chunk_w2(moe_w2, MOE_CHUNK),
        "router_w": [r[0] for r in rterms],
        "router_b": [r[1] for r in rterms],
    }
    mod_all = _adaln(jnp.concatenate([c_prompt, c_sample], axis=0), w_ada.astype(BF16), b_ada)
    seq = x_prompt.shape[1]
    tm = min(512, seq)
    tq = min(256, seq)
    yp, kp, vp, kip, cp = _trunk(x_prompt, mod_all[:, :bp], None, p, tm=tm, tq=tq)
    ts = x_sample.shape[1]
    ys, ksm, vsm, kism, csm = _trunk(x_sample, mod_all[:, bp:], (cache_k, cache_v, cache_idx_k, state_conv), p,
                                     tm=ts, tq=-(-ts // LANES) * LANES)
    return (yp, ys, kp, vp, kip, cp, ksm, vsm, kism, csm)
```

```python
import functools

import jax
import jax.numpy as jnp
from jax import lax
from jax.experimental import pallas as pl
from jax.experimental.pallas import tpu as pltpu

F32 = jnp.float32
BF16 = jnp.bfloat16
I32 = jnp.int32
I16 = jnp.int16

D_MODEL = 1024
D_CONV = 512
CONV_W = 3
N_HEADS = 8
N_KV_HEADS = 4
HEAD_DIM = 64
GROUP = N_HEADS // N_KV_HEADS
N_IDX_HEADS = 16
IDX_DIM = 64
TOPK_MAX = 256
CHUNK = 64
D_FF = 2816
N_EXPERTS = 8
D_FF_EXPERT = 3584
EPS = 1e-6

D_Q = N_HEADS * HEAD_DIM
D_KV = N_KV_HEADS * HEAD_DIM
D_QI = N_IDX_HEADS * IDX_DIM
LANES = 128
C_CONV = 0
C_Q = 3 * D_CONV
C_KV = C_Q + D_Q
C_QI = C_KV + 2 * D_KV
C_KIW = C_QI + D_QI
C_GATE = C_KIW + LANES
D_PACK = C_GATE + 2 * D_MODEL
WI_SCALE = (IDX_DIM ** -0.5) * (N_IDX_HEADS ** -0.5)
Q_SCALE = HEAD_DIM ** -0.5

KEY_BLOCK = 256
FFN_CHUNK = 256
MOE_CHUNK = 512
MOE_ROWS = (D_FF_EXPERT // MOE_CHUNK) * 128
V7X_VMEM_LIMIT = 56 * 1024 * 1024

KEY_NEG_INF = -2139095041
KEY_MIN_FINITE = KEY_NEG_INF + 1
MASK_BIAS = -1e30


def _sigmoid(x):
    return 1.0 / (1.0 + jnp.exp(-x))


def _rmsnorm(x, g):
    ms = jnp.mean(x * x, axis=-1, keepdims=True)
    return x * lax.rsqrt(ms + EPS) * g


def _cparams(n_axes):
    return pltpu.CompilerParams(dimension_semantics=("arbitrary",) * n_axes,
                                vmem_limit_bytes=V7X_VMEM_LIMIT)


def _resident(shape):
    nd = len(shape)
    return pl.BlockSpec(shape, lambda *_: (0,) * nd, pipeline_mode=pl.Buffered(1))


def _adaln_kernel(c_ref, w_ref, b_ref, o_ref):
    c = c_ref[...]
    cond = (c * _sigmoid(c)).astype(BF16)
    o_ref[0] = jnp.dot(cond, w_ref[0], preferred_element_type=F32) + b_ref[0]


def _adaln(c_all, w_ada, b_ada):
    depth, d, n6 = w_ada.shape
    bc = c_all.shape[0]
    nblk = n6 // D_MODEL
    return pl.pallas_call(
        _adaln_kernel,
        grid=(depth, nblk),
        in_specs=[pl.BlockSpec((bc, d), lambda l, n: (0, 0)),
                  pl.BlockSpec((1, d, D_MODEL), lambda l, n: (l, 0, n)),
                  pl.BlockSpec((1, 1, D_MODEL), lambda l, n: (l, 0, n))],
        out_specs=pl.BlockSpec((1, bc, D_MODEL), lambda l, n: (l, 0, n)),
        out_shape=jax.ShapeDtypeStruct((depth, bc, n6), F32),
        compiler_params=_cparams(2), name="adaln",
    )(c_all, w_ada, b_ada.reshape(depth, 1, n6))


def _inproj_kernel(x_ref, mod_ref, g_ref, w_ref, cw_ref, st_ref,
                   ya_ref, q_ref, k_ref, v_ref, qi_ref, kiw_ref, ki_ref, sg_ref, tail_ref,
                   state_scr, *, tm):
    t = pl.program_id(1)

    @pl.when(t == 0)
    def _():
        state_scr[...] = st_ref[0]

    x = x_ref[0]
    h = _rmsnorm(x, g_ref[...]) * (1.0 + mod_ref[0, 1]) + mod_ref[0, 0]
    hb = h.astype(BF16)

    def seg(a, b):
        return jnp.dot(hb, w_ref[:, a:b], preferred_element_type=F32)

    bg = seg(C_CONV, C_CONV + D_CONV)
    u = seg(C_CONV + D_CONV, C_CONV + 2 * D_CONV) * seg(C_CONV + 2 * D_CONV, C_CONV + 3 * D_CONV)
    s0 = state_scr[0:1, :]
    s1 = state_scr[1:2, :]
    ri = lax.broadcasted_iota(I32, u.shape, 0)
    um1 = jnp.where(ri == 0, s1, pltpu.roll(u, 1, 0))
    um2 = jnp.where(ri == 0, s0, jnp.where(ri == 1, s1, pltpu.roll(u, 2, 0)))
    cw = cw_ref[...]
    y = cw[0:1] * um2
    y = y + cw[1:2] * um1
    y = y + cw[2:3] * u
    ya_ref[0] = (bg * y).astype(BF16)
    tail = u[tm - 2:tm, :]
    state_scr[...] = tail
    tail_ref[0] = tail

    q_ref[0] = (seg(C_Q, C_Q + D_Q) * Q_SCALE).astype(BF16)
    k_ref[0] = seg(C_KV, C_KV + D_KV)
    v_ref[0] = seg(C_KV + D_KV, C_KV + 2 * D_KV)
    half = D_QI // 2
    qi_ref[0, :, 0:half] = seg(C_QI, C_QI + half).astype(BF16)
    qi_ref[0, :, half:D_QI] = seg(C_QI + half, C_QI + D_QI).astype(BF16)
    kw = seg(C_KIW, C_KIW + LANES)
    ki_ref[0] = kw[:, 0:IDX_DIM]
    lane = lax.broadcasted_iota(I32, (1, LANES), 1)
    scale = jnp.where(lane < IDX_DIM, 1.0, jnp.where(lane < IDX_DIM + N_IDX_HEADS, WI_SCALE, 0.0))
    kiw_ref[0] = kw * scale

    for i in range(4):
        a = C_GATE + i * 512
        sg_ref[0, :, i * 512:(i + 1) * 512] = _sigmoid(seg(a, a + 512)).astype(BF16)


def _inproj(x, mod, g1, w_pack, conv_w, conv_state, tm):
    b, s, d = x.shape
    nt = s // tm
    row = lambda w, dt: jax.ShapeDtypeStruct((b, s, w), dt)
    rspec = lambda w: pl.BlockSpec((1, tm, w), lambda bi, ti: (bi, ti, 0))
    return pl.pallas_call(
        functools.partial(_inproj_kernel, tm=tm),
        grid=(b, nt),
        in_specs=[rspec(d),
                  pl.BlockSpec((1, 6, 1, d), lambda bi, ti: (bi, 0, 0, 0)),
                  _resident((1, d)),
                  _resident((d, D_PACK)),
                  _resident((CONV_W, D_CONV)),
                  pl.BlockSpec((1, CONV_W - 1, D_CONV), lambda bi, ti: (bi, 0, 0))],
        out_specs=[rspec(D_CONV), rspec(D_Q), rspec(D_KV), rspec(D_KV), rspec(D_QI),
                   rspec(LANES), rspec(IDX_DIM), rspec(2 * D_MODEL),
                   pl.BlockSpec((1, CONV_W - 1, D_CONV), lambda bi, ti: (bi, 0, 0))],
        out_shape=[row(D_CONV, BF16), row(D_Q, BF16), row(D_KV, F32), row(D_KV, F32),
                   row(D_QI, BF16), row(LANES, F32), row(IDX_DIM, F32), row(2 * D_MODEL, BF16),
                   jax.ShapeDtypeStruct((b, CONV_W - 1, D_CONV), F32)],
        scratch_shapes=[pltpu.VMEM((CONV_W - 1, D_CONV), F32)],
        compiler_params=_cparams(2), name="inproj",
    )(x, mod, g1, w_pack, conv_w, conv_state)


def _pair_blockdiag(tile, low_half):
    lane = lax.broadcasted_iota(I32, tile.shape, 1)
    lo = lane < HEAD_DIM
    swapped = pltpu.roll(tile, HEAD_DIM, 1)
    top = jnp.where(lo, tile if low_half else swapped, 0.0)
    bot = jnp.where(lo, 0.0, swapped if low_half else tile)
    return jnp.concatenate([top, bot], axis=0).astype(BF16)


def _attn_kernel(q_ref, qi_ref, kiw_ref, k_ref, v_ref, ki_ref, o_ref,
                 kbd_scr, vbd_scr, kibd_scr, key_scr, hi_scr, lo_scr, s_scr, m_scr, l_scr, acc_scr,
                 *, tq, pos0, l_valid, topk):
    kb = KEY_BLOCK
    j = pl.program_id(1)
    nkc_all = key_scr.shape[0]
    npair = N_IDX_HEADS // 2
    nt = (((1,), (1,)), ((), ()))

    @pl.when(j == 0)
    def _():
        def build(c, carry):
            rows = pl.ds(pl.multiple_of(c * kb, kb), kb)
            kibd_scr[c] = _pair_blockdiag(ki_ref[0, rows, :], True)
            for g in range(N_KV_HEADS):
                t = g // 2
                kbd_scr[g, c] = _pair_blockdiag(k_ref[0, rows, t * LANES:(t + 1) * LANES], g % 2 == 0)
                vbd_scr[g, c] = _pair_blockdiag(v_ref[0, rows, t * LANES:(t + 1) * LANES], g % 2 == 0)
            return carry

        lax.fori_loop(0, nkc_all, build, 0)

    w_t = kiw_ref[0].T

    row0 = pos0 + j * tq
    rows = row0 + lax.broadcasted_iota(I32, (1, tq), 1)
    lim = jnp.minimum(((rows >> 6) + 1) << 6, l_valid)
    last = row0 + tq - 1
    lim_max = jnp.minimum(((last >> 6) + 1) << 6, l_valid)
    nkc = (lim_max + kb - 1) >> 8

    def phase_a(c, carry):
        kbd = kibd_scr[c]
        acc = jnp.zeros((kb, tq), F32)
        for t in range(npair):
            d2 = lax.dot_general(kbd, qi_ref[0, :, t * LANES:(t + 1) * LANES], nt,
                                 preferred_element_type=F32)
            r = IDX_DIM + 2 * t
            acc = acc + w_t[r:r + 1, :] * jnp.maximum(d2[0:kb], 0.0)
            acc = acc + w_t[r + 1:r + 2, :] * jnp.maximum(d2[kb:2 * kb], 0.0)
        col = c * kb + lax.broadcasted_iota(I32, (kb, tq), 0)
        sc = jnp.where(col < lim, acc, -jnp.inf)
        bits = lax.bitcast_convert_type(sc, I32)
        key = bits ^ ((bits >> 31) & 0x7FFFFFFF)
        key_scr[c] = key
        hi_scr[c] = (key >> 16).astype(I16)
        lo_scr[c] = ((key & 0xFFFF) - 32768).astype(I16)
        return carry

    lax.fori_loop(0, nkc, phase_a, 0)

    def count_ge(cand):
        def body(c, cnt):
            m = jnp.where(key_scr[c] >= cand, 1.0, 0.0)
            parts = [m[8 * i:8 * (i + 1)] for i in range(kb // 8)]
            while len(parts) > 1:
                parts = [parts[i] + parts[i + 1] for i in range(0, len(parts), 2)]
            return cnt + parts[0]
        cnt = lax.fori_loop(0, nkc, body, jnp.zeros((8, tq), F32))
        return jnp.sum(cnt, axis=0, keepdims=True)

    def count16(ref, pred):
        def body(c, cnt):
            m = jnp.where(pred(ref[c]), jnp.int16(1), jnp.int16(0))
            parts = [m[16 * i:16 * (i + 1)] for i in range(kb // 16)]
            while len(parts) > 1:
                parts = [parts[i] + parts[i + 1] for i in range(0, len(parts), 2)]
            return cnt + parts[0]
        cnt = lax.fori_loop(0, nkc, body, jnp.zeros((16, tq), I16))
        return jnp.sum(cnt.astype(I32), axis=0, keepdims=True)

    def radix16(ref, need):
        t0 = jnp.where(count16(ref, lambda h: h >= jnp.int16(0)) >= need, 0, -32768).astype(I32)

        def bit_body(i, t):
            cand = t | lax.shift_left(jnp.int32(1), 14 - i)
            c16 = cand.astype(I16)
            return jnp.where(count16(ref, lambda h: h >= c16) >= need, cand, t)

        return lax.fori_loop(0, 15, bit_body, t0)

    t_hi = radix16(hi_scr, topk)
    h16 = t_hi.astype(I16)
    need_lo = topk - count16(hi_scr, lambda h: h > h16)

    def keep_bucket(c, carry):
        lo_scr[c] = jnp.where(hi_scr[c] == h16, lo_scr[c], jnp.int16(-32768))
        return carry

    lax.fori_loop(0, nkc, keep_bucket, 0)
    t_lo = radix16(lo_scr, need_lo)
    thr = (t_hi << 16) | ((t_lo + 32768) & 0xFFFF)
    kf = float(topk)
    few = lim <= topk
    thr = jnp.where(few, KEY_MIN_FINITE, thr)

    tie = jnp.logical_and(count_ge(thr) > kf, jnp.logical_not(few))
    any_tie = jnp.max(jnp.where(tie, 1.0, 0.0)) > 0.0

    @pl.when(any_tie)
    def _():
        need = kf - count_ge(thr + 1)
        r_i = lax.broadcasted_iota(I32, (kb, kb), 0)
        c_i = lax.broadcasted_iota(I32, (kb, kb), 1)
        tri = jnp.where(c_i < r_i, 1.0, 0.0).astype(BF16)

        def body(c, run):
            kc = key_scr[c]
            eqf = jnp.where(kc == thr, 1.0, 0.0)
            rank = run + jnp.dot(tri, eqf.astype(BF16), preferred_element_type=F32)
            drop = jnp.logical_and(jnp.logical_and(tie, kc == thr), rank >= need)
            key_scr[c] = jnp.where(drop, KEY_NEG_INF, kc)
            return run + jnp.sum(eqf, axis=0, keepdims=True)

        lax.fori_loop(0, nkc, body, jnp.zeros((1, tq), F32))

    nl = kb // LANES
    m_scr[...] = jnp.full(m_scr.shape, MASK_BIAS, F32)
    l_scr[...] = jnp.zeros(l_scr.shape, F32)
    acc_scr[...] = jnp.zeros(acc_scr.shape, F32)

    def sweep1(c, carry):
        bias = jnp.where(key_scr[c] >= thr, 0.0, MASK_BIAS).T
        for g in range(N_KV_HEADS):
            s2 = lax.dot_general(q_ref[0, :, g * LANES:(g + 1) * LANES], kbd_scr[g, c], nt,
                                 preferred_element_type=F32)
            for r in range(GROUP):
                h = GROUP * g + r
                s = s2[:, r * kb:(r + 1) * kb] + bias
                s_scr[h, c] = s
                mx = m_scr[h]
                for i in range(nl):
                    mx = jnp.maximum(mx, s[:, i * LANES:(i + 1) * LANES])
                m_scr[h] = mx
        return carry

    lax.fori_loop(0, nkc, sweep1, 0)
    for h in range(N_HEADS):
        m_scr[h] = jnp.broadcast_to(jnp.max(m_scr[h], axis=1, keepdims=True), (tq, LANES))

    def sweep2(c, carry):
        for g in range(N_KV_HEADS):
            p2 = []
            for r in range(GROUP):
                h = GROUP * g + r
                s = s_scr[h, c]
                mb = m_scr[h]
                ps = [jnp.exp(s[:, i * LANES:(i + 1) * LANES] - mb) for i in range(nl)]
                psum = ps[0]
                for i in range(1, nl):
                    psum = psum + ps[i]
                l_scr[h] += psum
                p2 += ps
            acc_scr[g] += jnp.dot(jnp.concatenate(p2, axis=1).astype(BF16), vbd_scr[g, c],
                                  preferred_element_type=F32)
        return carry

    lax.fori_loop(0, nkc, sweep2, 0)
    low = lax.broadcasted_iota(I32, (tq, LANES), 1) < HEAD_DIM
    for g in range(N_KV_HEADS):
        l0 = jnp.sum(l_scr[GROUP * g], axis=1, keepdims=True)
        l1 = jnp.sum(l_scr[GROUP * g + 1], axis=1, keepdims=True)
        o_ref[0, :, g * LANES:(g + 1) * LANES] = (acc_scr[g] / jnp.where(low, l0, l1)).astype(BF16)


def _attention(q, qi, kiw, k_all, v_all, ki_all, *, tq, pos0, l_valid, topk):
    b, s, _ = q.shape
    l_pad = k_all.shape[1]
    nkc_max = l_pad // KEY_BLOCK
    qspec = lambda w: pl.BlockSpec((1, tq, w), lambda bi, ji: (bi, ji, 0))
    kspec = lambda w: pl.BlockSpec((1, l_pad, w), lambda bi, ji: (bi, 0, 0))
    return pl.pallas_call(
        functools.partial(_attn_kernel, tq=tq, pos0=pos0, l_valid=l_valid, topk=topk),
        grid=(b, s // tq),
        in_specs=[qspec(D_Q), qspec(D_QI), qspec(LANES), kspec(D_KV), kspec(D_KV), kspec(LANES)],
        out_specs=qspec(D_Q),
        out_shape=jax.ShapeDtypeStruct((b, s, D_Q), BF16),
        scratch_shapes=[pltpu.VMEM((N_KV_HEADS, nkc_max, 2 * KEY_BLOCK, LANES), BF16),
                        pltpu.VMEM((N_KV_HEADS, nkc_max, 2 * KEY_BLOCK, LANES), BF16),
                        pltpu.VMEM((nkc_max, 2 * KEY_BLOCK, LANES), BF16),
                        pltpu.VMEM((nkc_max, KEY_BLOCK, tq), I32),
                        pltpu.VMEM((nkc_max, KEY_BLOCK, tq), I16),
                        pltpu.VMEM((nkc_max, KEY_BLOCK, tq), I16),
                        pltpu.VMEM((N_HEADS, nkc_max, tq, KEY_BLOCK), F32),
                        pltpu.VMEM((N_HEADS, tq, LANES), F32),
                        pltpu.VMEM((N_HEADS, tq, LANES), F32),
                        pltpu.VMEM((N_KV_HEADS, tq, LANES), F32)],
        compiler_params=_cparams(2), name="attention",
    )(q, qi, kiw, k_all, v_all, ki_all)


def _split3(a):
    hi = a.astype(BF16)
    r = a - hi.astype(F32)
    mid = r.astype(BF16)
    lo = (r - mid.astype(F32)).astype(BF16)
    return hi, mid, lo


def _merge_kernel(*refs, with_router):
    if with_router:
        (x_ref, ya_ref, ao_ref, sg_ref, mod_ref, wc_ref, wa_ref, wo_ref, g2_ref, rw_ref, rb_ref,
         x1_ref, h2_ref, lg_ref) = refs
    else:
        (x_ref, ya_ref, ao_ref, sg_ref, mod_ref, wc_ref, wa_ref, wo_ref, g2_ref,
         x1_ref, h2_ref) = refs
    a = jnp.dot(ya_ref[0], wc_ref[...], preferred_element_type=F32)
    b = jnp.dot(ao_ref[0], wa_ref[...], preferred_element_type=F32)
    merged = sg_ref[0, :, 0:D_MODEL].astype(F32) * a + sg_ref[0, :, D_MODEL:2 * D_MODEL].astype(F32) * b
    mix = jnp.dot(merged.astype(BF16), wo_ref[...], preferred_element_type=F32)
    x1 = x_ref[0] + mod_ref[0, 2] * mix
    x1_ref[0] = x1
    h2 = _rmsnorm(x1, g2_ref[...]) * (1.0 + mod_ref[0, 4]) + mod_ref[0, 3]
    h2_ref[0] = h2.astype(h2_ref.dtype)
    if with_router:
        hs = _split3(h2)
        acc = jnp.zeros((h2.shape[0], LANES), F32)
        for i, jj in ((0, 0), (0, 1), (1, 0)):
            acc = acc + jnp.dot(hs[i], rw_ref[jj], preferred_element_type=F32)
        lane = lax.broadcasted_iota(I32, acc.shape, 1)
        lg = jnp.where(lane < N_EXPERTS, acc + rb_ref[...], -jnp.inf)
        m1 = jnp.max(lg, axis=1, keepdims=True)
        i1 = jnp.min(jnp.where(lg == m1, lane, LANES), axis=1, keepdims=True)
        rest = jnp.where(lane == i1, -jnp.inf, lg)
        m2 = jnp.max(rest, axis=1, keepdims=True)
        i2 = jnp.min(jnp.where(rest == m2, lane, LANES), axis=1, keepdims=True)
        e2 = jnp.exp(m2 - m1)
        den = 1.0 + e2
        lg_ref[0] = jnp.where(lane == 0, i1.astype(F32),
                              jnp.where(lane == 1, i2.astype(F32),
                                        jnp.where(lane == 2, 1.0 / den,
                                                  jnp.where(lane == 3, e2 / den, 0.0))))


def _merge(x, ya, ao, sg, mod, wc, wa, wo, g2, router=None, *, tm):
    b, s, d = x.shape
    r = mod.shape[2]
    rspec = lambda w: pl.BlockSpec((1, tm, w), lambda bi, ti: (bi, ti, 0))
    mspec = (pl.BlockSpec((1, 6, 1, d), lambda bi, ti: (bi, 0, 0, 0)) if r == 1 else
             pl.BlockSpec((1, 6, tm, d), lambda bi, ti: (bi, 0, ti, 0)))
    in_specs = [rspec(d), rspec(D_CONV), rspec(D_Q), rspec(2 * d), mspec,
                _resident((D_CONV, d)), _resident((D_Q, d)), _resident((d, d)), _resident((1, d))]
    args = [x, ya, ao, sg, mod, wc, wa, wo, g2]
    out_specs = [rspec(d), rspec(d)]
    out_shape = [jax.ShapeDtypeStruct((b, s, d), F32),
                 jax.ShapeDtypeStruct((b, s, d), BF16 if router is None else F32)]
    if router is not None:
        in_specs += [_resident((3, d, LANES)), _resident((1, LANES))]
        args += list(router)
        out_specs.append(rspec(LANES))
        out_shape.append(jax.ShapeDtypeStruct((b, s, LANES), F32))
    return pl.pallas_call(
        functools.partial(_merge_kernel, with_router=router is not None),
        grid=(b, s // tm),
        in_specs=in_specs, out_specs=out_specs, out_shape=out_shape,
        compiler_params=_cparams(2), name="merge",
    )(*args)


def _ffn_kernel(h2_ref, x1_ref, mod_ref, w13_ref, w2_ref, o_ref, acc_scr, *, nc, fc):
    h = h2_ref[0]
    acc_scr[...] = _swiglu_chunk(h, w13_ref[0], w2_ref[0], fc)

    def body(c, carry):
        acc_scr[...] += _swiglu_chunk(h, w13_ref[c], w2_ref[c], fc)
        return carry

    lax.fori_loop(1, nc, body, 0)
    o_ref[0] = x1_ref[0] + mod_ref[0, 5] * acc_scr[...]


def _ffn(h2, x1, mod, w13, w2, *, tm):
    b, s, d = x1.shape
    nc, _, fc2 = w13.shape
    r = mod.shape[2]
    rspec = lambda w: pl.BlockSpec((1, tm, w), lambda bi, ti: (bi, ti, 0))
    mspec = (pl.BlockSpec((1, 6, 1, d), lambda bi, ti: (bi, 0, 0, 0)) if r == 1 else
             pl.BlockSpec((1, 6, tm, d), lambda bi, ti: (bi, 0, ti, 0)))
    return pl.pallas_call(
        functools.partial(_ffn_kernel, nc=nc, fc=fc2 // 2),
        grid=(b, s // tm),
        in_specs=[rspec(d), rspec(d), mspec, _resident(w13.shape), _resident(w2.shape)],
        out_specs=rspec(d),
        out_shape=jax.ShapeDtypeStruct((b, s, d), F32),
        scratch_shapes=[pltpu.VMEM((tm, d), F32)],
        compiler_params=_cparams(2), name="ffn",
    )(h2, x1, mod, w13, w2)


def _swiglu_chunk(xb, w13, w2, fc):
    ab = jnp.dot(xb, w13, preferred_element_type=F32)
    a = ab[:, 0:fc]
    hm = (a * _sigmoid(a) * ab[:, fc:2 * fc]).astype(BF16)
    return jnp.dot(hm, w2, preferred_element_type=F32)


def _moe_dense_kernel(h2_ref, x1_ref, rt_ref, mod_ref, w13_ref, w2_ref, fg_ref, o_ref,
                      eacc_scr, acc_scr, *, fc, final_norm):
    e = pl.program_id(2)
    c = pl.program_id(3)
    ne = pl.num_programs(2)
    nc = pl.num_programs(3)

    @pl.when(jnp.logical_and(e == 0, c == 0))
    def _():
        acc_scr[...] = jnp.zeros(acc_scr.shape, F32)

    @pl.when(c == 0)
    def _():
        eacc_scr[...] = jnp.zeros(eacc_scr.shape, F32)

    eacc_scr[...] += _swiglu_chunk(h2_ref[0].astype(BF16), w13_ref[0, 0], w2_ref[0, 0], fc)

    @pl.when(c == nc - 1)
    def _():
        rt = rt_ref[0]
        ef = e.astype(F32)
        gate = (jnp.where(rt[:, 0:1] == ef, rt[:, 2:3], 0.0)
                + jnp.where(rt[:, 1:2] == ef, rt[:, 3:4], 0.0))
        acc_scr[...] += gate * eacc_scr[...]

    @pl.when(jnp.logical_and(e == ne - 1, c == nc - 1))
    def _():
        x2 = x1_ref[0] + mod_ref[0, 5] * acc_scr[...]
        o_ref[0] = _rmsnorm(x2, fg_ref[...]) if final_norm else x2


def _moe_dense(h2, x1, route, mod, w13, w2, fg, *, tm, final_norm):
    b, s, d = x1.shape
    ne, nc, _, fc2 = w13.shape
    r = mod.shape[2]
    rspec = lambda w: pl.BlockSpec((1, tm, w), lambda bi, ti, ei, ci: (bi, ti, 0))
    mspec = (pl.BlockSpec((1, 6, 1, d), lambda bi, ti, ei, ci: (bi, 0, 0, 0)) if r == 1 else
             pl.BlockSpec((1, 6, tm, d), lambda bi, ti, ei, ci: (bi, 0, ti, 0)))
    return pl.pallas_call(
        functools.partial(_moe_dense_kernel, fc=fc2 // 2, final_norm=final_norm),
        grid=(b, s // tm, ne, nc),
        in_specs=[rspec(d), rspec(d), rspec(LANES), mspec,
                  pl.BlockSpec((1, 1, d, fc2), lambda bi, ti, ei, ci: (ei, ci, 0, 0)),
                  pl.BlockSpec((1, 1, fc2 // 2, d), lambda bi, ti, ei, ci: (ei, ci, 0, 0)),
                  _resident((1, d))],
        out_specs=rspec(d),
        out_shape=jax.ShapeDtypeStruct((b, s, d), F32),
        scratch_shapes=[pltpu.VMEM((tm, d), F32), pltpu.VMEM((tm, d), F32)],
        compiler_params=_cparams(4), name="moe_dense",
    )(h2, x1, route, mod, w13, w2, fg)


def _route_tables(route, tmr):
    n = route.shape[0]
    e1 = route[:, 0].astype(I32)
    e2 = route[:, 1].astype(I32)
    ar = jnp.arange(N_EXPERTS, dtype=I32)
    oh1 = e1[:, None] == ar
    oh2 = e2[:, None] == ar
    oh = oh1.astype(I32) + oh2.astype(I32)
    cum = jnp.cumsum(oh, axis=0)
    excl = cum - oh
    cnt = cum[-1]
    tiles_e = (cnt + tmr - 1) // tmr
    tile_end = jnp.cumsum(tiles_e)
    tile_start = tile_end - tiles_e
    row_off = tile_start * tmr
    d1 = row_off[e1] + jnp.sum(jnp.where(oh1, excl, 0), axis=1)
    d2 = row_off[e2] + jnp.sum(jnp.where(oh2, excl, 0), axis=1)
    nt = -(-2 * n // tmr) + N_EXPERTS
    tok = jnp.arange(n, dtype=I32)
    pair = jnp.zeros((nt * tmr,), I32).at[jnp.concatenate([d1, d2])].set(
        jnp.concatenate([2 * tok, 2 * tok + 1]), unique_indices=True)
    src = pair >> 1
    dst = (pair & 1) * n + src
    tile = jnp.arange(nt, dtype=I32)
    te = jnp.minimum(jnp.searchsorted(tile_end, tile, side="right"), N_EXPERTS - 1).astype(I32)
    nv = jnp.where(tile < tile_end[-1], jnp.clip(cnt[te] - (tile - tile_start[te]) * tmr, 0, tmr), 0)
    return (src.reshape(nt, 1, tmr), dst.reshape(nt, 1, tmr), te, nv.astype(I32),
            tile_end[-1:].astype(I32))


def _moe_routed_kernel(te_ref, nv_ref, nu_ref, src0_ref, srcn_ref, dstp_ref, h2_hbm, w13_ref, w2_ref,
                       out_hbm, xbuf, xb_scr, ybuf, gsem, ssem, *, tmr, nc, fc):
    i = pl.program_id(0)
    c = pl.program_id(1)
    nu = nu_ref[0]
    slot = i % 2
    other = 1 - slot
    rpc = tmr // nc

    def gather(src_ref, cc, k, sl):
        return pltpu.make_async_copy(h2_hbm.at[pl.ds(src_ref[0, 0, cc * rpc + k], 1)],
                                     xbuf.at[sl, cc, pl.ds(k, 1)], gsem.at[sl])

    def scatter(cc, k, sl):
        return pltpu.make_async_copy(ybuf.at[sl, cc, pl.ds(k, 1)],
                                     out_hbm.at[pl.ds(dstp_ref[0, 0, cc * rpc + k], 1)], ssem.at[sl])

    @pl.when(jnp.logical_and(i == 0, c == 0))
    def _():
        def body(cc, carry):
            for k in range(rpc):
                gather(src0_ref, cc, k, 0).start()
            return carry
        lax.fori_loop(0, nc, body, 0)

    @pl.when(jnp.logical_and(c == 0, i < nu))
    def _():
        pltpu.make_async_copy(xbuf.at[slot], xbuf.at[slot], gsem.at[slot]).wait()
        xb_scr[...] = xbuf[slot].reshape(tmr, xb_scr.shape[1]).astype(BF16)

    drain = jnp.logical_and(c == 0, jnp.logical_and(i >= 2, i - 2 < nu))
    last = nv_ref.shape[0] - 1
    nv_drain = nv_ref[jnp.clip(i - 2, 0, last)]

    @pl.when(jnp.logical_and(drain, nv_drain == tmr))
    def _():
        pltpu.make_async_copy(ybuf.at[slot], ybuf.at[slot], ssem.at[slot]).wait()

    @pl.when(jnp.logical_and(drain, nv_drain < tmr))
    def _():
        def body(r, carry):
            scatter(0, 0, slot).wait()
            return carry
        lax.fori_loop(0, nv_drain, body, 0)

    push = jnp.logical_and(i >= 1, i - 1 < nu)
    nv_push = nv_ref[jnp.clip(i - 1, 0, last)]
    fast = jnp.logical_and(i >= 1, jnp.logical_and(i + 1 < nu, nv_push == tmr))
    slow = jnp.logical_not(fast)

    @pl.when(fast)
    def _():
        ab = jnp.dot(xb_scr[...], w13_ref[0, 0], preferred_element_type=F32)
        for k in range(rpc):
            gather(srcn_ref, c, k, other).start()
            scatter(c, k, other).start()
        a = ab[:, 0:fc]
        hm = (a * _sigmoid(a) * ab[:, fc:2 * fc]).astype(BF16)
        y = jnp.dot(hm, w2_ref[0, 0], preferred_element_type=F32).reshape(nc, rpc, xb_scr.shape[1])
        ybuf[slot] = jnp.where(c == 0, y, ybuf[slot] + y)

    @pl.when(jnp.logical_and(slow, i + 1 < nu))
    def _():
        for k in range(rpc):
            gather(srcn_ref, c, k, other).start()

    @pl.when(jnp.logical_and(slow, jnp.logical_and(push, nv_push == tmr)))
    def _():
        for k in range(rpc):
            scatter(c, k, other).start()

    @pl.when(jnp.logical_and(slow, jnp.logical_and(push, nv_push < tmr)))
    def _():
        def body(k, carry):
            scatter(c, k, other).start()
            return carry
        lax.fori_loop(0, jnp.clip(nv_push - c * rpc, 0, rpc), body, 0)

    @pl.when(jnp.logical_and(slow, i < nu))
    def _():
        y = _swiglu_chunk(xb_scr[...], w13_ref[0, 0], w2_ref[0, 0], fc).reshape(nc, rpc, xb_scr.shape[1])

        @pl.when(c == 0)
        def _():
            ybuf[slot] = y

        @pl.when(c > 0)
        def _():
            ybuf[slot] += y


def _moe_routed(h2, route, w13, w2, *, tmr):
    n, d = h2.shape
    ne, nc, _, fc2 = w13.shape
    src, dst, te, nv, nu = _route_tables(route, tmr)
    nt = src.shape[0]
    last = nt - 1
    smem = lambda imap: pl.BlockSpec((1, 1, tmr), imap, memory_space=pltpu.SMEM)
    wmap = lambda i, c, te_r, nv_r, nu_r: (te_r[jnp.minimum(i, last)],
                                           jnp.where(i < nu_r[0], c, nc - 1), 0, 0)
    return pl.pallas_call(
        functools.partial(_moe_routed_kernel, tmr=tmr, nc=nc, fc=fc2 // 2),
        grid_spec=pltpu.PrefetchScalarGridSpec(
            num_scalar_prefetch=3,
            grid=(nt + 2, nc),
            in_specs=[smem(lambda i, c, te_r, nv_r, nu_r: (0, 0, 0)),
                      smem(lambda i, c, te_r, nv_r, nu_r: (jnp.minimum(i + 1, last), 0, 0)),
                      smem(lambda i, c, te_r, nv_r, nu_r: (jnp.clip(i - 1, 0, last), 0, 0)),
                      pl.BlockSpec(memory_space=pl.ANY),
                      pl.BlockSpec((1, 1, d, fc2), wmap),
                      pl.BlockSpec((1, 1, fc2 // 2, d), wmap)],
            out_specs=pl.BlockSpec(memory_space=pl.ANY),
            scratch_shapes=[pltpu.VMEM((2, nc, tmr // nc, d), F32), pltpu.VMEM((tmr, d), BF16),
                            pltpu.VMEM((2, nc, tmr // nc, d), F32),
                            pltpu.SemaphoreType.DMA((2,)), pltpu.SemaphoreType.DMA((2,))]),
        out_shape=jax.ShapeDtypeStruct((2 * n, d), F32),
        compiler_params=_cparams(2), name="moe_routed",
    )(te, nv, nu, src, src, dst, h2, w13, w2)


def _moe_combine_kernel(x1_ref, ya_ref, yb_ref, rt_ref, mod_ref, fg_ref, o_ref, *, final_norm):
    rt = rt_ref[0]
    f = rt[:, 2:3] * ya_ref[...] + rt[:, 3:4] * yb_ref[...]
    x2 = x1_ref[0] + mod_ref[0, 5] * f
    o_ref[0] = _rmsnorm(x2, fg_ref[...]) if final_norm else x2


def _moe_combine(x1, y2, route, mod, fg, *, tm, final_norm):
    b, s, d = x1.shape
    nt = s // tm
    rspec = lambda w: pl.BlockSpec((1, tm, w), lambda bi, ti: (bi, ti, 0))
    return pl.pallas_call(
        functools.partial(_moe_combine_kernel, final_norm=final_norm),
        grid=(b, nt),
        in_specs=[rspec(d),
                  pl.BlockSpec((tm, d), lambda bi, ti: (bi * nt + ti, 0)),
                  pl.BlockSpec((tm, d), lambda bi, ti: (b * nt + bi * nt + ti, 0)),
                  rspec(LANES),
                  pl.BlockSpec((1, 6, 1, d), lambda bi, ti: (bi, 0, 0, 0)), _resident((1, d))],
        out_specs=rspec(d),
        out_shape=jax.ShapeDtypeStruct((b, s, d), F32),
        compiler_params=_cparams(2), name="moe_combine",
    )(x1, y2, y2, route, mod, fg)


def _pack_w_in(w):
    cut = C_KIW + IDX_DIM + N_IDX_HEADS
    pad = jnp.zeros((w.shape[0], C_GATE - cut), w.dtype)
    return jnp.concatenate([w[:, :cut], pad, w[:, cut:]], axis=1).astype(BF16)


def _chunk_w13(w1, w3, fc):
    *lead, d, f = w1.shape
    nc = f // fc
    a = w1.reshape(*lead, d, nc, fc)
    b = w3.reshape(*lead, d, nc, fc)
    ab = jnp.concatenate([a, b], axis=-1)
    return jnp.moveaxis(ab, -2, -3).astype(BF16)


def _chunk_w2(w2, fc):
    *lead, f, d = w2.shape
    return w2.reshape(*lead, f // fc, fc, d).astype(BF16)


def _router_terms(rw, rb):
    d = rw.shape[0]
    rwp = jnp.zeros((d, LANES), F32).at[:, :N_EXPERTS].set(rw)
    hi, mid, lo = _split3(rwp)
    rbp = jnp.zeros((1, LANES), F32).at[0, :N_EXPERTS].set(rb)
    return jnp.stack([hi, mid, lo]), rbp


def _trunk(x, mod_all, caches, p, *, tm, tq):
    b, t, d = x.shape
    depth = mod_all.shape[0]
    flat = caches is not None
    ks, vs, kis, tails = [], [], [], []
    for l in range(depth):
        mod = mod_all[l].reshape(b, 6, 1, d)
        if caches is None:
            conv_state = jnp.zeros((b, CONV_W - 1, D_CONV), F32)
        else:
            conv_state = caches[3][l]
        ya, q, k, v, qi, kiw, ki, sg, tail = _inproj(
            x, mod, p["norm1_g"][l], p["w_in"][l], p["conv_w"][l], conv_state, tm)
        if caches is None:
            k_all, v_all, ki_all = k, v, kiw
            pos0, l_valid = 0, t
            ao = _attention(q, qi, kiw, k_all, v_all, ki_all, tq=tq, pos0=0, l_valid=t,
                            topk=min(TOPK_MAX, t // 4))
        else:
            past = caches[0][l].shape[1]
            l_valid = past + t
            l_pad = -(-l_valid // KEY_BLOCK) * KEY_BLOCK
            cat = lambda old, new: jnp.concatenate(
                [old.reshape(b, past, -1), new, jnp.zeros((b, l_pad - l_valid, new.shape[-1]), F32)], axis=1)
            k_all, v_all = cat(caches[0][l], k), cat(caches[1][l], v)
            ki_all = jnp.pad(cat(caches[2][l], ki), ((0, 0), (0, 0), (0, LANES - IDX_DIM)))
            padq = lambda a: jnp.pad(a, ((0, 0), (0, tq - t), (0, 0)))
            ao = _attention(padq(q), padq(qi), padq(kiw), k_all, v_all, ki_all, tq=tq, pos0=past,
                            l_valid=l_valid, topk=min(TOPK_MAX, l_valid // 4))[:, :t]
        is_moe = l % 2 == 1
        router = (p["router_w"][l // 2], p["router_b"][l // 2]) if is_moe else None
        if flat:
            n = b * t
            fl = lambda a: a.reshape(1, n, a.shape[-1])
            modf = jnp.repeat(mod_all[l].reshape(b, 6, d), t, axis=0).transpose(1, 0, 2)[None]
            xs, yas, aos, sgs, tmf = fl(x), fl(ya), fl(ao), fl(sg), n
        else:
            modf, xs, yas, aos, sgs, tmf = mod, x, ya, ao, sg, tm
        outs = _merge(xs, yas, aos, sgs, modf, p["w_conv_out"][l], p["w_attn_out"][l], p["w_o"][l],
                      p["norm2_g"][l], router, tm=tmf)
        if is_moe and flat:
            x1, h2, route = outs
            x = _moe_dense(h2, x1, route, modf, p["moe_w13"][l // 2], p["moe_w2"][l // 2],
                           p["final_g"], tm=xs.shape[1], final_norm=l == depth - 1)
        elif is_moe:
            x1, h2, route = outs
            y2 = _moe_routed(h2.reshape(b * t, d), route.reshape(b * t, LANES),
                             p["moe_w13"][l // 2], p["moe_w2"][l // 2], tmr=MOE_ROWS)
            x = _moe_combine(x1, y2, route, mod, p["final_g"], tm=tm,
                             final_norm=l == depth - 1)
        else:
            x1, h2 = outs
            x = _ffn(h2, x1, modf, p["ffn_w13"][l // 2], p["ffn_w2"][l // 2],
                     tm=tmf if flat else min(2 * tm, t))
        x = x.reshape(b, t, d)
        ks.append(k.reshape(b, t, N_KV_HEADS, HEAD_DIM))
        vs.append(v.reshape(b, t, N_KV_HEADS, HEAD_DIM))
        kis.append(ki)
        tails.append(tail)
    return x, jnp.stack(ks), jnp.stack(vs), jnp.stack(kis), jnp.stack(tails)


def kernel(x_prompt, x_sample, c_prompt, c_sample, cache_k, cache_v, cache_idx_k, state_conv, w_ada, b_ada, norm1_g, w_in, conv_w, w_conv_out, w_attn_out, w_o, norm2_g, ffn_w1, ffn_w3, ffn_w2, router_w, router_b, moe_w1, moe_w3, moe_w2, final_g):
    depth = w_in.shape[0]
    bp = x_prompt.shape[0]
    rterms = [_router_terms(router_w[i], router_b[i]) for i in range(router_w.shape[0])]
    p = {
        "norm1_g": norm1_g.reshape(depth, 1, -1),
        "norm2_g": norm2_g.reshape(depth, 1, -1),
        "final_g": final_g.reshape(1, -1),
        "w_in": jnp.stack([_pack_w_in(w_in[l]) for l in range(depth)]),
        "conv_w": conv_w,
        "w_conv_out": w_conv_out.astype(BF16),
        "w_attn_out": w_attn_out.astype(BF16),
        "w_o": w_o.astype(BF16),
        "ffn_w13": _chunk_w13(ffn_w1, ffn_w3, FFN_CHUNK),
        "ffn_w2": _chunk_w2(ffn_w2, FFN_CHUNK),
        "moe_w13": _chunk_w13(moe_w1, moe_w3, MOE_CHUNK),
        "moe_w2": _chunk_w2(moe_w2, MOE_CHUNK),
        "router_w": [r[0] for r in rterms],
        "router_b": [r[1] for r in rterms],
    }
    mod_all = _adaln(jnp.concatenate([c_prompt, c_sample], axis=0), w_ada.astype(BF16), b_ada)
    seq = x_prompt.shape[1]
    tm = min(512, seq)
    tq = min(256, seq)
    yp, kp, vp, kip, cp = _trunk(x_prompt, mod_all[:, :bp], None, p, tm=tm, tq=tq)
    ts = x_sample.shape[1]
    ys, ksm, vsm, kism, csm = _trunk(x_sample, mod_all[:, bp:], (cache_k, cache_v, cache_idx_k, state_conv), p,
                                     tm=ts, tq=-(-ts // LANES) * LANES)
    return (yp, ys, kp, vp, kip, cp, ksm, vsm, kism, csm)
```

```python
import functools

import jax
import jax.numpy as jnp
from jax import lax
from jax.experimental import pallas as pl
from jax.experimental.pallas import tpu as pltpu

F32 = jnp.float32
BF16 = jnp.bfloat16
I32 = jnp.int32
I16 = jnp.int16

D_MODEL = 1024
D_CONV = 512
CONV_W = 3
N_HEADS = 8
N_KV_HEADS = 4
HEAD_DIM = 64
GROUP = N_HEADS // N_KV_HEADS
N_IDX_HEADS = 16
IDX_DIM = 64
TOPK_MAX = 256
CHUNK = 64
D_FF = 2816
N_EXPERTS = 8
D_FF_EXPERT = 3584
EPS = 1e-6

D_Q = N_HEADS * HEAD_DIM
D_KV = N_KV_HEADS * HEAD_DIM
D_QI = N_IDX_HEADS * IDX_DIM
LANES = 128
C_CONV = 0
C_Q = 3 * D_CONV
C_KV = C_Q + D_Q
C_QI = C_KV + 2 * D_KV
C_KIW = C_QI + D_QI
C_GATE = C_KIW + LANES
D_PACK = C_GATE + 2 * D_MODEL
WI_SCALE = (IDX_DIM ** -0.5) * (N_IDX_HEADS ** -0.5)
Q_SCALE = HEAD_DIM ** -0.5 * 1.4426950408889634

KEY_BLOCK = 256
FFN_CHUNK = 256
MOE_CHUNK = 512
MOE_ROWS = (D_FF_EXPERT // MOE_CHUNK) * 128
MOE_GATHER_STEPS = D_FF_EXPERT // MOE_CHUNK
V7X_VMEM_LIMIT = 56 * 1024 * 1024

KEY_NEG_INF = -2139095041
KEY_MIN_FINITE = KEY_NEG_INF + 1
MASK_BIAS = -1e30


def _sigmoid(x):
    return 1.0 / (1.0 + jnp.exp(-x))


def _rmsnorm(x, g):
    ms = jnp.mean(x * x, axis=-1, keepdims=True)
    return x * lax.rsqrt(ms + EPS) * g


def _cparams(n_axes):
    return pltpu.CompilerParams(dimension_semantics=("arbitrary",) * n_axes,
                                vmem_limit_bytes=V7X_VMEM_LIMIT)


def _resident(shape):
    nd = len(shape)
    return pl.BlockSpec(shape, lambda *_: (0,) * nd, pipeline_mode=pl.Buffered(1))


def _adaln_kernel(c_ref, w_ref, b_ref, o_ref):
    c = c_ref[...]
    cond = (c * _sigmoid(c)).astype(BF16)
    o_ref[0] = jnp.dot(cond, w_ref[0], preferred_element_type=F32) + b_ref[0]


def _adaln(c_all, w_ada, b_ada):
    depth, d, n6 = w_ada.shape
    bc = c_all.shape[0]
    nblk = n6 // D_MODEL
    return pl.pallas_call(
        _adaln_kernel,
        grid=(depth, nblk),
        in_specs=[pl.BlockSpec((bc, d), lambda l, n: (0, 0)),
                  pl.BlockSpec((1, d, D_MODEL), lambda l, n: (l, 0, n)),
                  pl.BlockSpec((1, 1, D_MODEL), lambda l, n: (l, 0, n))],
        out_specs=pl.BlockSpec((1, bc, D_MODEL), lambda l, n: (l, 0, n)),
        out_shape=jax.ShapeDtypeStruct((depth, bc, n6), F32),
        compiler_params=_cparams(2), name="adaln",
    )(c_all, w_ada, b_ada.reshape(depth, 1, n6))


def _inproj_kernel(x_ref, mod_ref, g_ref, w_ref, cw_ref, st_ref,
                   ya_ref, q_ref, k_ref, v_ref, qi_ref, kiw_ref, ki_ref, sg_ref, tail_ref,
                   state_scr, *, tm):
    t = pl.program_id(1)

    @pl.when(t == 0)
    def _():
        state_scr[...] = st_ref[0]

    x = x_ref[0]
    h = _rmsnorm(x, g_ref[...]) * (1.0 + mod_ref[0, 1]) + mod_ref[0, 0]
    hb = h.astype(BF16)

    def seg(a, b):
        return jnp.dot(hb, w_ref[:, a:b], preferred_element_type=F32)

    bg = seg(C_CONV, C_CONV + D_CONV)
    u = seg(C_CONV + D_CONV, C_CONV + 2 * D_CONV) * seg(C_CONV + 2 * D_CONV, C_CONV + 3 * D_CONV)
    s0 = state_scr[0:1, :]
    s1 = state_scr[1:2, :]
    ri = lax.broadcasted_iota(I32, u.shape, 0)
    um1 = jnp.where(ri == 0, s1, pltpu.roll(u, 1, 0))
    um2 = jnp.where(ri == 0, s0, jnp.where(ri == 1, s1, pltpu.roll(u, 2, 0)))
    cw = cw_ref[...]
    y = cw[0:1] * um2
    y = y + cw[1:2] * um1
    y = y + cw[2:3] * u
    ya_ref[0] = (bg * y).astype(BF16)
    tail = u[tm - 2:tm, :]
    state_scr[...] = tail
    tail_ref[0] = tail

    q_ref[0] = (seg(C_Q, C_Q + D_Q) * Q_SCALE).astype(BF16)
    k_ref[0] = seg(C_KV, C_KV + D_KV)
    v_ref[0] = seg(C_KV + D_KV, C_KV + 2 * D_KV)
    half = D_QI // 2
    qi_ref[0, :, 0:half] = seg(C_QI, C_QI + half).astype(BF16)
    qi_ref[0, :, half:D_QI] = seg(C_QI + half, C_QI + D_QI).astype(BF16)
    kw = seg(C_KIW, C_KIW + LANES)
    ki_ref[0] = kw[:, 0:IDX_DIM]
    lane = lax.broadcasted_iota(I32, (1, LANES), 1)
    scale = jnp.where(lane < IDX_DIM, 1.0, jnp.where(lane < IDX_DIM + N_IDX_HEADS, WI_SCALE, 0.0))
    kiw_ref[0] = kw * scale

    for i in range(4):
        a = C_GATE + i * 512
        sg_ref[0, :, i * 512:(i + 1) * 512] = _sigmoid(seg(a, a + 512)).astype(BF16)


def _inproj(x, mod, g1, w_pack, conv_w, conv_state, tm):
    b, s, d = x.shape
    nt = s // tm
    row = lambda w, dt: jax.ShapeDtypeStruct((b, s, w), dt)
    rspec = lambda w: pl.BlockSpec((1, tm, w), lambda bi, ti: (bi, ti, 0))
    return pl.pallas_call(
        functools.partial(_inproj_kernel, tm=tm),
        grid=(b, nt),
        in_specs=[rspec(d),
                  pl.BlockSpec((1, 6, 1, d), lambda bi, ti: (bi, 0, 0, 0)),
                  _resident((1, d)),
                  _resident((d, D_PACK)),
                  _resident((CONV_W, D_CONV)),
                  pl.BlockSpec((1, CONV_W - 1, D_CONV), lambda bi, ti: (bi, 0, 0))],
        out_specs=[rspec(D_CONV), rspec(D_Q), rspec(D_KV), rspec(D_KV), rspec(D_QI),
                   rspec(LANES), rspec(IDX_DIM), rspec(2 * D_MODEL),
                   pl.BlockSpec((1, CONV_W - 1, D_CONV), lambda bi, ti: (bi, 0, 0))],
        out_shape=[row(D_CONV, BF16), row(D_Q, BF16), row(D_KV, F32), row(D_KV, F32),
                   row(D_QI, BF16), row(LANES, F32), row(IDX_DIM, F32), row(2 * D_MODEL, BF16),
                   jax.ShapeDtypeStruct((b, CONV_W - 1, D_CONV), F32)],
        scratch_shapes=[pltpu.VMEM((CONV_W - 1, D_CONV), F32)],
        compiler_params=_cparams(2), name="inproj",
    )(x, mod, g1, w_pack, conv_w, conv_state)


def _pair_blockdiag(tile, low_half):
    lane = lax.broadcasted_iota(I32, tile.shape, 1)
    lo = lane < HEAD_DIM
    swapped = pltpu.roll(tile, HEAD_DIM, 1)
    top = jnp.where(lo, tile if low_half else swapped, 0.0)
    bot = jnp.where(lo, 0.0, swapped if low_half else tile)
    return jnp.concatenate([top, bot], axis=0).astype(BF16)


def _attn_kernel(q_ref, qi_ref, kiw_ref, k_ref, v_ref, ki_ref, o_ref,
                 kbd_scr, vbd_scr, kibd_scr, key_scr, hi_scr, lo_scr, s_scr, m_scr, l_scr, acc_scr,
                 *, tq, pos0, l_valid, topk):
    kb = KEY_BLOCK
    j = pl.program_id(1)
    nkc_all = key_scr.shape[0]
    npair = N_IDX_HEADS // 2
    nt = (((1,), (1,)), ((), ()))

    @pl.when(j == 0)
    def _():
        def build(c, carry):
            rows = pl.ds(pl.multiple_of(c * kb, kb), kb)
            kibd_scr[c] = _pair_blockdiag(ki_ref[0, rows, :], True)
            for g in range(N_KV_HEADS):
                t = g // 2
                kbd_scr[g, c] = _pair_blockdiag(k_ref[0, rows, t * LANES:(t + 1) * LANES], g % 2 == 0)
                vbd_scr[g, c] = _pair_blockdiag(v_ref[0, rows, t * LANES:(t + 1) * LANES], g % 2 == 0)
            return carry

        lax.fori_loop(0, nkc_all, build, 0)

    w_t = kiw_ref[0].T

    row0 = pos0 + j * tq
    rows = row0 + lax.broadcasted_iota(I32, (1, tq), 1)
    lim = jnp.minimum(((rows >> 6) + 1) << 6, l_valid)
    last = row0 + tq - 1
    lim_max = jnp.minimum(((last >> 6) + 1) << 6, l_valid)
    nkc = (lim_max + kb - 1) >> 8

    def phase_a(c, carry):
        kbd = kibd_scr[c]
        acc = jnp.zeros((kb, tq), F32)
        for t in range(npair):
            d2 = lax.dot_general(kbd, qi_ref[0, :, t * LANES:(t + 1) * LANES], nt,
                                 preferred_element_type=F32)
            r = IDX_DIM + 2 * t
            acc = acc + w_t[r:r + 1, :] * jnp.maximum(d2[0:kb], 0.0)
            acc = acc + w_t[r + 1:r + 2, :] * jnp.maximum(d2[kb:2 * kb], 0.0)
        col = c * kb + lax.broadcasted_iota(I32, (kb, tq), 0)
        sc = jnp.where(col < lim, acc, -jnp.inf)
        bits = lax.bitcast_convert_type(sc, I32)
        key = bits ^ ((bits >> 31) & 0x7FFFFFFF)
        key_scr[c] = key
        hi_scr[c] = (key >> 16).astype(I16)
        lo_scr[c] = ((key & 0xFFFF) - 32768).astype(I16)
        return carry

    lax.fori_loop(0, nkc, phase_a, 0)

    def count_ge(cand):
        def body(c, cnt):
            m = jnp.where(key_scr[c] >= cand, 1.0, 0.0)
            parts = [m[8 * i:8 * (i + 1)] for i in range(kb // 8)]
            while len(parts) > 1:
                parts = [parts[i] + parts[i + 1] for i in range(0, len(parts), 2)]
            return cnt + parts[0]
        cnt = lax.fori_loop(0, nkc, body, jnp.zeros((8, tq), F32))
        return jnp.sum(cnt, axis=0, keepdims=True)

    def count16(ref, pred):
        def body(c, cnt):
            m = jnp.where(pred(ref[c]), jnp.int16(1), jnp.int16(0))
            parts = [m[16 * i:16 * (i + 1)] for i in range(kb // 16)]
            while len(parts) > 1:
                parts = [parts[i] + parts[i + 1] for i in range(0, len(parts), 2)]
            return cnt + parts[0]
        cnt = lax.fori_loop(0, nkc, body, jnp.zeros((16, tq), I16))
        return jnp.sum(cnt.astype(I32), axis=0, keepdims=True)

    def radix16(ref, need):
        t0 = jnp.where(count16(ref, lambda h: h >= jnp.int16(0)) >= need, 0, -32768).astype(I32)

        def bit_body(i, t):
            cand = t | lax.shift_left(jnp.int32(1), 14 - i)
            c16 = cand.astype(I16)
            return jnp.where(count16(ref, lambda h: h >= c16) >= need, cand, t)

        return lax.fori_loop(0, 15, bit_body, t0)

    t_hi = radix16(hi_scr, topk)
    h16 = t_hi.astype(I16)
    need_lo = topk - count16(hi_scr, lambda h: h > h16)

    def keep_bucket(c, carry):
        lo_scr[c] = jnp.where(hi_scr[c] == h16, lo_scr[c], jnp.int16(-32768))
        return carry

    lax.fori_loop(0, nkc, keep_bucket, 0)
    t_lo = radix16(lo_scr, need_lo)
    thr = (t_hi << 16) | ((t_lo + 32768) & 0xFFFF)
    kf = float(topk)
    few = lim <= topk
    thr = jnp.where(few, KEY_MIN_FINITE, thr)

    tie = jnp.logical_and(count_ge(thr) > kf, jnp.logical_not(few))
    any_tie = jnp.max(jnp.where(tie, 1.0, 0.0)) > 0.0

    @pl.when(any_tie)
    def _():
        need = kf - count_ge(thr + 1)
        r_i = lax.broadcasted_iota(I32, (kb, kb), 0)
        c_i = lax.broadcasted_iota(I32, (kb, kb), 1)
        tri = jnp.where(c_i < r_i, 1.0, 0.0).astype(BF16)

        def body(c, run):
            kc = key_scr[c]
            eqf = jnp.where(kc == thr, 1.0, 0.0)
            rank = run + jnp.dot(tri, eqf.astype(BF16), preferred_element_type=F32)
            drop = jnp.logical_and(jnp.logical_and(tie, kc == thr), rank >= need)
            key_scr[c] = jnp.where(drop, KEY_NEG_INF, kc)
            return run + jnp.sum(eqf, axis=0, keepdims=True)

        lax.fori_loop(0, nkc, body, jnp.zeros((1, tq), F32))

    nl = kb // LANES
    m_scr[...] = jnp.full(m_scr.shape, MASK_BIAS, F32)
    l_scr[...] = jnp.zeros(l_scr.shape, F32)
    acc_scr[...] = jnp.zeros(acc_scr.shape, F32)

    def sweep1(c, carry):
        bias = jnp.where(key_scr[c] >= thr, 0.0, MASK_BIAS).T
        for g in range(N_KV_HEADS):
            s2 = lax.dot_general(q_ref[0, :, g * LANES:(g + 1) * LANES], kbd_scr[g, c], nt,
                                 preferred_element_type=F32)
            for r in range(GROUP):
                h = GROUP * g + r
                s = s2[:, r * kb:(r + 1) * kb] + bias
                s_scr[h, c] = s
                mx = m_scr[h]
                for i in range(nl):
                    mx = jnp.maximum(mx, s[:, i * LANES:(i + 1) * LANES])
                m_scr[h] = mx
        return carry

    lax.fori_loop(0, nkc, sweep1, 0)
    for h in range(N_HEADS):
        m_scr[h] = jnp.broadcast_to(jnp.max(m_scr[h], axis=1, keepdims=True), (tq, LANES))

    def sweep2(c, carry):
        for g in range(N_KV_HEADS):
            p2 = []
            for r in range(GROUP):
                h = GROUP * g + r
                s = s_scr[h, c]
                mb = m_scr[h]
                ps = [jnp.exp2(s[:, i * LANES:(i + 1) * LANES] - mb) for i in range(nl)]
                psum = ps[0]
                for i in range(1, nl):
                    psum = psum + ps[i]
                l_scr[h] += psum
                p2 += ps
            acc_scr[g] += jnp.dot(jnp.concatenate(p2, axis=1).astype(BF16), vbd_scr[g, c],
                                  preferred_element_type=F32)
        return carry

    lax.fori_loop(0, nkc, sweep2, 0)
    low = lax.broadcasted_iota(I32, (tq, LANES), 1) < HEAD_DIM
    for g in range(N_KV_HEADS):
        l0 = jnp.sum(l_scr[GROUP * g], axis=1, keepdims=True)
        l1 = jnp.sum(l_scr[GROUP * g + 1], axis=1, keepdims=True)
        o_ref[0, :, g * LANES:(g + 1) * LANES] = (acc_scr[g] / jnp.where(low, l0, l1)).astype(BF16)


def _attention(q, qi, kiw, k_all, v_all, ki_all, *, tq, pos0, l_valid, topk):
    b, s, _ = q.shape
    l_pad = k_all.shape[1]
    nkc_max = l_pad // KEY_BLOCK
    qspec = lambda w: pl.BlockSpec((1, tq, w), lambda bi, ji: (bi, ji, 0))
    kspec = lambda w: pl.BlockSpec((1, l_pad, w), lambda bi, ji: (bi, 0, 0))
    return pl.pallas_call(
        functools.partial(_attn_kernel, tq=tq, pos0=pos0, l_valid=l_valid, topk=topk),
        grid=(b, s // tq),
        in_specs=[qspec(D_Q), qspec(D_QI), qspec(LANES), kspec(D_KV), kspec(D_KV), kspec(LANES)],
        out_specs=qspec(D_Q),
        out_shape=jax.ShapeDtypeStruct((b, s, D_Q), BF16),
        scratch_shapes=[pltpu.VMEM((N_KV_HEADS, nkc_max, 2 * KEY_BLOCK, LANES), BF16),
                        pltpu.VMEM((N_KV_HEADS, nkc_max, 2 * KEY_BLOCK, LANES), BF16),
                        pltpu.VMEM((nkc_max, 2 * KEY_BLOCK, LANES), BF16),
                        pltpu.VMEM((nkc_max, KEY_BLOCK, tq), I32),
                        pltpu.VMEM((nkc_max, KEY_BLOCK, tq), I16),
                        pltpu.VMEM((nkc_max, KEY_BLOCK, tq), I16),
                        pltpu.VMEM((N_HEADS, nkc_max, tq, KEY_BLOCK), F32),
                        pltpu.VMEM((N_HEADS, tq, LANES), F32),
                        pltpu.VMEM((N_HEADS, tq, LANES), F32),
                        pltpu.VMEM((N_KV_HEADS, tq, LANES), F32)],
        compiler_params=_cparams(2), name="attention",
    )(q, qi, kiw, k_all, v_all, ki_all)


def _split3(a):
    hi = a.astype(BF16)
    r = a - hi.astype(F32)
    mid = r.astype(BF16)
    lo = (r - mid.astype(F32)).astype(BF16)
    return hi, mid, lo


def _merge_kernel(*refs, with_router):
    if with_router:
        (x_ref, ya_ref, ao_ref, sg_ref, mod_ref, wc_ref, wa_ref, wo_ref, g2_ref, rw_ref, rb_ref,
         x1_ref, h2_ref, lg_ref) = refs
    else:
        (x_ref, ya_ref, ao_ref, sg_ref, mod_ref, wc_ref, wa_ref, wo_ref, g2_ref,
         x1_ref, h2_ref) = refs
    a = jnp.dot(ya_ref[0], wc_ref[...], preferred_element_type=F32)
    b = jnp.dot(ao_ref[0], wa_ref[...], preferred_element_type=F32)
    merged = sg_ref[0, :, 0:D_MODEL].astype(F32) * a + sg_ref[0, :, D_MODEL:2 * D_MODEL].astype(F32) * b
    mix = jnp.dot(merged.astype(BF16), wo_ref[...], preferred_element_type=F32)
    x1 = x_ref[0] + mod_ref[0, 2] * mix
    x1_ref[0] = x1
    h2 = _rmsnorm(x1, g2_ref[...]) * (1.0 + mod_ref[0, 4]) + mod_ref[0, 3]
    h2_ref[0] = h2.astype(h2_ref.dtype)
    if with_router:
        hs = _split3(h2)
        acc = jnp.zeros((h2.shape[0], LANES), F32)
        for i, jj in ((0, 0), (0, 1), (1, 0)):
            acc = acc + jnp.dot(hs[i], rw_ref[jj], preferred_element_type=F32)
        lane = lax.broadcasted_iota(I32, acc.shape, 1)
        lg = jnp.where(lane < N_EXPERTS, acc + rb_ref[...], -jnp.inf)
        m1 = jnp.max(lg, axis=1, keepdims=True)
        i1 = jnp.min(jnp.where(lg == m1, lane, LANES), axis=1, keepdims=True)
        rest = jnp.where(lane == i1, -jnp.inf, lg)
        m2 = jnp.max(rest, axis=1, keepdims=True)
        i2 = jnp.min(jnp.where(rest == m2, lane, LANES), axis=1, keepdims=True)
        e2 = jnp.exp(m2 - m1)
        den = 1.0 + e2
        lg_ref[0] = jnp.where(lane == 0, i1.astype(F32),
                              jnp.where(lane == 1, i2.astype(F32),
                                        jnp.where(lane == 2, 1.0 / den,
                                                  jnp.where(lane == 3, e2 / den, 0.0))))


def _merge(x, ya, ao, sg, mod, wc, wa, wo, g2, router=None, *, tm):
    b, s, d = x.shape
    r = mod.shape[2]
    rspec = lambda w: pl.BlockSpec((1, tm, w), lambda bi, ti: (bi, ti, 0))
    mspec = (pl.BlockSpec((1, 6, 1, d), lambda bi, ti: (bi, 0, 0, 0)) if r == 1 else
             pl.BlockSpec((1, 6, tm, d), lambda bi, ti: (bi, 0, ti, 0)))
    in_specs = [rspec(d), rspec(D_CONV), rspec(D_Q), rspec(2 * d), mspec,
                _resident((D_CONV, d)), _resident((D_Q, d)), _resident((d, d)), _resident((1, d))]
    args = [x, ya, ao, sg, mod, wc, wa, wo, g2]
    out_specs = [rspec(d), rspec(d)]
    out_shape = [jax.ShapeDtypeStruct((b, s, d), F32),
                 jax.ShapeDtypeStruct((b, s, d), BF16 if router is None else F32)]
    if router is not None:
        in_specs += [_resident((3, d, LANES)), _resident((1, LANES))]
        args += list(router)
        out_specs.append(rspec(LANES))
        out_shape.append(jax.ShapeDtypeStruct((b, s, LANES), F32))
    return pl.pallas_call(
        functools.partial(_merge_kernel, with_router=router is not None),
        grid=(b, s // tm),
        in_specs=in_specs, out_specs=out_specs, out_shape=out_shape,
        compiler_params=_cparams(2), name="merge",
    )(*args)


def _ffn_kernel(h2_ref, x1_ref, mod_ref, w1_ref, w3_ref, w2_ref, o_ref, acc_scr, *, nc):
    h = h2_ref[0]
    acc_scr[...] = _swiglu_chunk(h, w1_ref[0], w3_ref[0], w2_ref[0])

    def body(c, carry):
        acc_scr[...] += _swiglu_chunk(h, w1_ref[c], w3_ref[c], w2_ref[c])
        return carry

    lax.fori_loop(1, nc, body, 0)
    o_ref[0] = x1_ref[0] + mod_ref[0, 5] * acc_scr[...]


def _ffn(h2, x1, mod, w1, w3, w2, *, tm):
    b, s, d = x1.shape
    nc = w1.shape[0]
    r = mod.shape[2]
    rspec = lambda w: pl.BlockSpec((1, tm, w), lambda bi, ti: (bi, ti, 0))
    mspec = (pl.BlockSpec((1, 6, 1, d), lambda bi, ti: (bi, 0, 0, 0)) if r == 1 else
             pl.BlockSpec((1, 6, tm, d), lambda bi, ti: (bi, 0, ti, 0)))
    return pl.pallas_call(
        functools.partial(_ffn_kernel, nc=nc),
        grid=(b, s // tm),
        in_specs=[rspec(d), rspec(d), mspec, _resident(w1.shape), _resident(w3.shape), _resident(w2.shape)],
        out_specs=rspec(d),
        out_shape=jax.ShapeDtypeStruct((b, s, d), F32),
        scratch_shapes=[pltpu.VMEM((tm, d), F32)],
        compiler_params=_cparams(2), name="ffn",
    )(h2, x1, mod, w1, w3, w2)


def _swiglu_chunk(xb, w1, w3, w2):
    a = jnp.dot(xb, w1, preferred_element_type=F32)
    hm = (a * _sigmoid(a) * jnp.dot(xb, w3, preferred_element_type=F32)).astype(BF16)
    return jnp.dot(hm, w2, preferred_element_type=F32)


def _moe_dense_kernel(h2_ref, x1_ref, rt_ref, mod_ref, w1_ref, w3_ref, w2_ref, fg_ref, o_ref,
                      eacc_scr, acc_scr, *, final_norm):
    e = pl.program_id(2)
    c = pl.program_id(3)
    ne = pl.num_programs(2)
    nc = pl.num_programs(3)

    @pl.when(jnp.logical_and(e == 0, c == 0))
    def _():
        acc_scr[...] = jnp.zeros(acc_scr.shape, F32)

    @pl.when(c == 0)
    def _():
        eacc_scr[...] = jnp.zeros(eacc_scr.shape, F32)

    eacc_scr[...] += _swiglu_chunk(h2_ref[0].astype(BF16), w1_ref[0], w3_ref[0], w2_ref[0])

    @pl.when(c == nc - 1)
    def _():
        rt = rt_ref[0]
        ef = e.astype(F32)
        gate = (jnp.where(rt[:, 0:1] == ef, rt[:, 2:3], 0.0)
                + jnp.where(rt[:, 1:2] == ef, rt[:, 3:4], 0.0))
        acc_scr[...] += gate * eacc_scr[...]

    @pl.when(jnp.logical_and(e == ne - 1, c == nc - 1))
    def _():
        x2 = x1_ref[0] + mod_ref[0, 5] * acc_scr[...]
        o_ref[0] = _rmsnorm(x2, fg_ref[...]) if final_norm else x2


def _moe_dense(h2, x1, route, mod, w1, w3, w2, fg, *, tm, final_norm):
    b, s, d = x1.shape
    ne, _, f = w1.shape
    fc = MOE_CHUNK
    r = mod.shape[2]
    rspec = lambda w: pl.BlockSpec((1, tm, w), lambda bi, ti, ei, ci: (bi, ti, 0))
    mspec = (pl.BlockSpec((1, 6, 1, d), lambda bi, ti, ei, ci: (bi, 0, 0, 0)) if r == 1 else
             pl.BlockSpec((1, 6, tm, d), lambda bi, ti, ei, ci: (bi, 0, ti, 0)))
    wcol = pl.BlockSpec((1, d, fc), lambda bi, ti, ei, ci: (ei, 0, ci))
    return pl.pallas_call(
        functools.partial(_moe_dense_kernel, final_norm=final_norm),
        grid=(b, s // tm, ne, f // fc),
        in_specs=[rspec(d), rspec(d), rspec(LANES), mspec, wcol, wcol,
                  pl.BlockSpec((1, fc, d), lambda bi, ti, ei, ci: (ei, ci, 0)),
                  _resident((1, d))],
        out_specs=rspec(d),
        out_shape=jax.ShapeDtypeStruct((b, s, d), F32),
        scratch_shapes=[pltpu.VMEM((tm, d), F32), pltpu.VMEM((tm, d), F32)],
        compiler_params=_cparams(4), name="moe_dense",
    )(h2, x1, route, mod, w1, w3, w2, fg)


def _route_tables(route, tmr):
    n = route.shape[0]
    e1 = route[:, 0].astype(I32)
    e2 = route[:, 1].astype(I32)
    ar = jnp.arange(N_EXPERTS, dtype=I32)
    oh1 = e1[:, None] == ar
    oh2 = e2[:, None] == ar
    oh = oh1.astype(I32) + oh2.astype(I32)
    cum = jnp.cumsum(oh, axis=0)
    excl = cum - oh
    cnt = cum[-1]
    tiles_e = (cnt + tmr - 1) // tmr
    tile_end = jnp.cumsum(tiles_e)
    tile_start = tile_end - tiles_e
    row_off = tile_start * tmr
    d1 = row_off[e1] + jnp.sum(jnp.where(oh1, excl, 0), axis=1)
    d2 = row_off[e2] + jnp.sum(jnp.where(oh2, excl, 0), axis=1)
    nt = -(-2 * n // tmr) + N_EXPERTS
    tok = jnp.arange(n, dtype=I32)
    pair = jnp.zeros((nt * tmr,), I32).at[jnp.concatenate([d1, d2])].set(
        jnp.concatenate([2 * tok, 2 * tok + 1]), unique_indices=True)
    src = pair >> 1
    dst = (pair & 1) * n + src
    tile = jnp.arange(nt, dtype=I32)
    te = jnp.minimum(jnp.searchsorted(tile_end, tile, side="right"), N_EXPERTS - 1).astype(I32)
    nv = jnp.where(tile < tile_end[-1], jnp.clip(cnt[te] - (tile - tile_start[te]) * tmr, 0, tmr), 0)
    return (src.reshape(nt, 1, tmr), dst.reshape(nt, 1, tmr), te, nv.astype(I32),
            tile_end[-1:].astype(I32))


def _moe_routed_kernel(te_ref, nv_ref, nu_ref, src0_ref, srcn_ref, dstp_ref, h2_hbm, w1_ref, w3_ref, w2_ref,
                       out_hbm, xbuf, xb_scr, ybuf, gsem, ssem, *, tmr, nc, gs):
    i = pl.program_id(0)
    c = pl.program_id(1)
    nu = nu_ref[0]
    slot = i % 2
    other = 1 - slot
    rpc = tmr // nc
    rpg = tmr // gs

    def gather(src_ref, cc, k, sl):
        return pltpu.make_async_copy(h2_hbm.at[pl.ds(src_ref[0, 0, cc * rpg + k], 1)],
                                     xbuf.at[sl, cc, pl.ds(k, 1)], gsem.at[sl])

    def scatter(cc, k, sl):
        return pltpu.make_async_copy(ybuf.at[sl, cc, pl.ds(k, 1)],
                                     out_hbm.at[pl.ds(dstp_ref[0, 0, cc * rpc + k], 1)], ssem.at[sl])

    @pl.when(jnp.logical_and(i == 0, c == 0))
    def _():
        def body(cc, carry):
            for k in range(rpg):
                gather(src0_ref, cc, k, 0).start()
            return carry
        lax.fori_loop(0, gs, body, 0)

    @pl.when(jnp.logical_and(c == 0, i < nu))
    def _():
        pltpu.make_async_copy(xbuf.at[slot], xbuf.at[slot], gsem.at[slot]).wait()
        xb_scr[...] = xbuf[slot].reshape(tmr, xb_scr.shape[1]).astype(BF16)

    drain = jnp.logical_and(c == 0, jnp.logical_and(i >= 2, i - 2 < nu))
    last = nv_ref.shape[0] - 1
    nv_drain = nv_ref[jnp.clip(i - 2, 0, last)]

    @pl.when(jnp.logical_and(drain, nv_drain == tmr))
    def _():
        pltpu.make_async_copy(ybuf.at[slot], ybuf.at[slot], ssem.at[slot]).wait()

    @pl.when(jnp.logical_and(drain, nv_drain < tmr))
    def _():
        def body(r, carry):
            scatter(0, 0, slot).wait()
            return carry
        lax.fori_loop(0, nv_drain, body, 0)

    push = jnp.logical_and(i >= 1, i - 1 < nu)
    nv_push = nv_ref[jnp.clip(i - 1, 0, last)]
    fast = jnp.logical_and(i >= 1, jnp.logical_and(i + 1 < nu, nv_push == tmr))
    slow = jnp.logical_not(fast)

    def fast_step(with_gather):
        xb = xb_scr[...]
        a = jnp.dot(xb, w1_ref[0], preferred_element_type=F32)
        b = jnp.dot(xb, w3_ref[0], preferred_element_type=F32)
        if with_gather:
            for k in range(rpg):
                gather(srcn_ref, c, k, other).start()
        for k in range(rpc):
            scatter(c, k, other).start()
        hm = (a * _sigmoid(a) * b).astype(BF16)
        y = jnp.dot(hm, w2_ref[0], preferred_element_type=F32).reshape(nc, rpc, xb_scr.shape[1])
        ybuf[slot] = jnp.where(c == 0, y, ybuf[slot] + y)

    @pl.when(jnp.logical_and(fast, c < gs))
    def _():
        fast_step(True)

    @pl.when(jnp.logical_and(fast, c >= gs))
    def _():
        fast_step(False)

    @pl.when(jnp.logical_and(slow, jnp.logical_and(i + 1 < nu, c < gs)))
    def _():
        for k in range(rpg):
            gather(srcn_ref, c, k, other).start()

    @pl.when(jnp.logical_and(slow, jnp.logical_and(push, nv_push == tmr)))
    def _():
        for k in range(rpc):
            scatter(c, k, other).start()

    @pl.when(jnp.logical_and(slow, jnp.logical_and(push, nv_push < tmr)))
    def _():
        def body(k, carry):
            scatter(c, k, other).start()
            return carry
        lax.fori_loop(0, jnp.clip(nv_push - c * rpc, 0, rpc), body, 0)

    @pl.when(jnp.logical_and(slow, i < nu))
    def _():
        y = _swiglu_chunk(xb_scr[...], w1_ref[0], w3_ref[0], w2_ref[0]).reshape(nc, rpc, xb_scr.shape[1])

        @pl.when(c == 0)
        def _():
            ybuf[slot] = y

        @pl.when(c > 0)
        def _():
            ybuf[slot] += y


def _moe_routed(h2, route, w1, w3, w2, *, tmr):
    n, d = h2.shape
    f = w1.shape[2]
    fc = MOE_CHUNK
    nc = f // fc
    gs = MOE_GATHER_STEPS
    src, dst, te, nv, nu = _route_tables(route, tmr)
    nt = src.shape[0]
    last = nt - 1
    smem = lambda imap: pl.BlockSpec((1, 1, tmr), imap, memory_space=pltpu.SMEM)
    chunk = lambda i, c, nu_r: jnp.where(i < nu_r[0], c, nc - 1)
    wcol = pl.BlockSpec((1, d, fc), lambda i, c, te_r, nv_r, nu_r: (te_r[jnp.minimum(i, last)], 0, chunk(i, c, nu_r)))
    wrow = pl.BlockSpec((1, fc, d), lambda i, c, te_r, nv_r, nu_r: (te_r[jnp.minimum(i, last)], chunk(i, c, nu_r), 0))
    return pl.pallas_call(
        functools.partial(_moe_routed_kernel, tmr=tmr, nc=nc, gs=gs),
        grid_spec=pltpu.PrefetchScalarGridSpec(
            num_scalar_prefetch=3,
            grid=(nt + 2, nc),
            in_specs=[smem(lambda i, c, te_r, nv_r, nu_r: (0, 0, 0)),
                      smem(lambda i, c, te_r, nv_r, nu_r: (jnp.minimum(i + 1, last), 0, 0)),
                      smem(lambda i, c, te_r, nv_r, nu_r: (jnp.clip(i - 1, 0, last), 0, 0)),
                      pl.BlockSpec(memory_space=pl.ANY), wcol, wcol, wrow],
            out_specs=pl.BlockSpec(memory_space=pl.ANY),
            scratch_shapes=[pltpu.VMEM((2, gs, tmr // gs, d), F32), pltpu.VMEM((tmr, d), BF16),
                            pltpu.VMEM((2, nc, tmr // nc, d), F32),
                            pltpu.SemaphoreType.DMA((2,)), pltpu.SemaphoreType.DMA((2,))]),
        out_shape=jax.ShapeDtypeStruct((2 * n, d), F32),
        compiler_params=_cparams(2), name="moe_routed",
    )(te, nv, nu, src, src, dst, h2, w1, w3, w2)


def _moe_combine_kernel(x1_ref, ya_ref, yb_ref, rt_ref, mod_ref, fg_ref, o_ref, *, final_norm):
    rt = rt_ref[0]
    f = rt[:, 2:3] * ya_ref[...] + rt[:, 3:4] * yb_ref[...]
    x2 = x1_ref[0] + mod_ref[0, 5] * f
    o_ref[0] = _rmsnorm(x2, fg_ref[...]) if final_norm else x2


def _moe_combine(x1, y2, route, mod, fg, *, tm, final_norm):
    b, s, d = x1.shape
    nt = s // tm
    rspec = lambda w: pl.BlockSpec((1, tm, w), lambda bi, ti: (bi, ti, 0))
    return pl.pallas_call(
        functools.partial(_moe_combine_kernel, final_norm=final_norm),
        grid=(b, nt),
        in_specs=[rspec(d),
                  pl.BlockSpec((tm, d), lambda bi, ti: (bi * nt + ti, 0)),
                  pl.BlockSpec((tm, d), lambda bi, ti: (b * nt + bi * nt + ti, 0)),
                  rspec(LANES),
                  pl.BlockSpec((1, 6, 1, d), lambda bi, ti: (bi, 0, 0, 0)), _resident((1, d))],
        out_specs=rspec(d),
        out_shape=jax.ShapeDtypeStruct((b, s, d), F32),
        compiler_params=_cparams(2), name="moe_combine",
    )(x1, y2, y2, route, mod, fg)


def _pack_w_in(w):
    cut = C_KIW + IDX_DIM + N_IDX_HEADS
    pad = jnp.zeros((w.shape[0], C_GATE - cut), w.dtype)
    return jnp.concatenate([w[:, :cut], pad, w[:, cut:]], axis=1).astype(BF16)


def _chunk_cols(w, fc):
    *lead, d, f = w.shape
    return jnp.moveaxis(w.astype(BF16).reshape(*lead, d, f // fc, fc), -2, -3)


def _chunk_w2(w2, fc):
    *lead, f, d = w2.shape
    return w2.reshape(*lead, f // fc, fc, d).astype(BF16)


def _router_terms(rw, rb):
    d = rw.shape[0]
    rwp = jnp.zeros((d, LANES), F32).at[:, :N_EXPERTS].set(rw)
    hi, mid, lo = _split3(rwp)
    rbp = jnp.zeros((1, LANES), F32).at[0, :N_EXPERTS].set(rb)
    return jnp.stack([hi, mid, lo]), rbp


def _trunk(x, mod_all, caches, p, *, tm, tq):
    b, t, d = x.shape
    depth = mod_all.shape[0]
    flat = caches is not None
    ks, vs, kis, tails = [], [], [], []
    for l in range(depth):
        mod = mod_all[l].reshape(b, 6, 1, d)
        if caches is None:
            conv_state = jnp.zeros((b, CONV_W - 1, D_CONV), F32)
        else:
            conv_state = caches[3][l]
        ya, q, k, v, qi, kiw, ki, sg, tail = _inproj(
            x, mod, p["norm1_g"][l], p["w_in"][l], p["conv_w"][l], conv_state, tm)
        if caches is None:
            k_all, v_all, ki_all = k, v, kiw
            pos0, l_valid = 0, t
            ao = _attention(q, qi, kiw, k_all, v_all, ki_all, tq=tq, pos0=0, l_valid=t,
                            topk=min(TOPK_MAX, t // 4))
        else:
            past = caches[0][l].shape[1]
            l_valid = past + t
            l_pad = -(-l_valid // KEY_BLOCK) * KEY_BLOCK
            cat = lambda old, new: jnp.concatenate(
                [old.reshape(b, past, -1), new, jnp.zeros((b, l_pad - l_valid, new.shape[-1]), F32)], axis=1)
            k_all, v_all = cat(caches[0][l], k), cat(caches[1][l], v)
            ki_all = jnp.pad(cat(caches[2][l], ki), ((0, 0), (0, 0), (0, LANES - IDX_DIM)))
            padq = lambda a: jnp.pad(a, ((0, 0), (0, tq - t), (0, 0)))
            ao = _attention(padq(q), padq(qi), padq(kiw), k_all, v_all, ki_all, tq=tq, pos0=past,
                            l_valid=l_valid, topk=min(TOPK_MAX, l_valid // 4))[:, :t]
        is_moe = l % 2 == 1
        router = (p["router_w"][l // 2], p["router_b"][l // 2]) if is_moe else None
        if flat:
            n = b * t
            fl = lambda a: a.reshape(1, n, a.shape[-1])
            modf = jnp.repeat(mod_all[l].reshape(b, 6, d), t, axis=0).transpose(1, 0, 2)[None]
            xs, yas, aos, sgs, tmf = fl(x), fl(ya), fl(ao), fl(sg), n
        else:
            modf, xs, yas, aos, sgs, tmf = mod, x, ya, ao, sg, tm
        outs = _merge(xs, yas, aos, sgs, modf, p["w_conv_out"][l], p["w_attn_out"][l], p["w_o"][l],
                      p["norm2_g"][l], router, tm=tmf)
        if is_moe and flat:
            x1, h2, route = outs
            x = _moe_dense(h2, x1, route, modf, p["moe_w1"][l // 2], p["moe_w3"][l // 2],
                           p["moe_w2"][l // 2], p["final_g"], tm=xs.shape[1], final_norm=l == depth - 1)
        elif is_moe:
            x1, h2, route = outs
            y2 = _moe_routed(h2.reshape(b * t, d), route.reshape(b * t, LANES), p["moe_w1"][l // 2],
                             p["moe_w3"][l // 2], p["moe_w2"][l // 2], tmr=MOE_ROWS)
            x = _moe_combine(x1, y2, route, mod, p["final_g"], tm=tm,
                             final_norm=l == depth - 1)
        else:
            x1, h2 = outs
            x = _ffn(h2, x1, modf, p["ffn_w1"][l // 2], p["ffn_w3"][l // 2], p["ffn_w2"][l // 2],
                     tm=tmf if flat else min(2 * tm, t))
        x = x.reshape(b, t, d)
        ks.append(k.reshape(b, t, N_KV_HEADS, HEAD_DIM))
        vs.append(v.reshape(b, t, N_KV_HEADS, HEAD_DIM))
        kis.append(ki)
        tails.append(tail)
    return x, jnp.stack(ks), jnp.stack(vs), jnp.stack(kis), jnp.stack(tails)


def kernel(x_prompt, x_sample, c_prompt, c_sample, cache_k, cache_v, cache_idx_k, state_conv, w_ada, b_ada, norm1_g, w_in, conv_w, w_conv_out, w_attn_out, w_o, norm2_g, ffn_w1, ffn_w3, ffn_w2, router_w, router_b, moe_w1, moe_w3, moe_w2, final_g):
    depth = w_in.shape[0]
    bp = x_prompt.shape[0]
    rterms = [_router_terms(router_w[i], router_b[i]) for i in range(router_w.shape[0])]
    p = {
        "norm1_g": norm1_g.reshape(depth, 1, -1),
        "norm2_g": norm2_g.reshape(depth, 1, -1),
        "final_g": final_g.reshape(1, -1),
        "w_in": jnp.stack([_pack_w_in(w_in[l]) for l in range(depth)]),
        "conv_w": conv_w,
        "w_conv_out": w_conv_out.astype(BF16),
        "w_attn_out": w_attn_out.astype(BF16),
        "w_o": w_o.astype(BF16),
        "ffn_w1": _chunk_cols(ffn_w1, FFN_CHUNK),
        "ffn_w3": _chunk_cols(ffn_w3, FFN_CHUNK),
        "ffn_w2": _chunk_w2(ffn_w2, FFN_CHUNK),
        "moe_w1": moe_w1.astype(BF16),
        "moe_w3": moe_w3.astype(BF16),
        "moe_w2": moe_w2.astype(BF16),
        "router_w": [r[0] for r in rterms],
        "router_b": [r[1] for r in rterms],
    }
    mod_all = _adaln(jnp.concatenate([c_prompt, c_sample], axis=0), w_ada.astype(BF16), b_ada)
    seq = x_prompt.shape[1]
    tm = min(512, seq)
    tq = min(256, seq)
    yp, kp, vp, kip, cp = _trunk(x_prompt, mod_all[:, :bp], None, p, tm=tm, tq=tq)
    ts = x_sample.shape[1]
    ys, ksm, vsm, kism, csm = _trunk(x_sample, mod_all[:, bp:], (cache_k, cache_v, cache_idx_k, state_conv), p,
                                     tm=ts, tq=-(-ts // LANES) * LANES)
    return (yp, ys, kp, vp, kip, cp, ksm, vsm, kism, csm)
```

```python
import functools

import jax
import jax.numpy as jnp
from jax import lax
from jax.experimental import pallas as pl
from jax.experimental.pallas import tpu as pltpu

F32 = jnp.float32
BF16 = jnp.bfloat16
I32 = jnp.int32
I16 = jnp.int16

D_MODEL = 1024
D_CONV = 512
CONV_W = 3
N_HEADS = 8
N_KV_HEADS = 4
HEAD_DIM = 64
GROUP = N_HEADS // N_KV_HEADS
N_IDX_HEADS = 16
IDX_DIM = 64
TOPK_MAX = 256
CHUNK = 64
D_FF = 2816
N_EXPERTS = 8
D_FF_EXPERT = 3584
EPS = 1e-6

D_Q = N_HEADS * HEAD_DIM
D_KV = N_KV_HEADS * HEAD_DIM
D_QI = N_IDX_HEADS * IDX_DIM
LANES = 128
C_CONV = 0
C_Q = 3 * D_CONV
C_KV = C_Q + D_Q
C_QI = C_KV + 2 * D_KV
C_KIW = C_QI + D_QI
C_GATE = C_KIW + LANES
D_PACK = C_GATE + 2 * D_MODEL
WI_SCALE = (IDX_DIM ** -0.5) * (N_IDX_HEADS ** -0.5)
Q_SCALE = HEAD_DIM ** -0.5 * 1.4426950408889634

KEY_BLOCK = 256
FFN_CHUNK = 256
MOE_CHUNK = 512
MOE_ROWS = (D_FF_EXPERT // MOE_CHUNK) * 128
MOE_GATHER_STEPS = D_FF_EXPERT // MOE_CHUNK
V7X_VMEM_LIMIT = 56 * 1024 * 1024

KEY_NEG_INF = -2139095041
KEY_MIN_FINITE = KEY_NEG_INF + 1
MASK_BIAS = -1e30


def _sigmoid(x):
    return 1.0 / (1.0 + jnp.exp(-x))


def _rmsnorm(x, g):
    ms = jnp.mean(x * x, axis=-1, keepdims=True)
    return x * lax.rsqrt(ms + EPS) * g


def _cparams(n_axes):
    return pltpu.CompilerParams(dimension_semantics=("arbitrary",) * n_axes,
                                vmem_limit_bytes=V7X_VMEM_LIMIT)


def _resident(shape):
    nd = len(shape)
    return pl.BlockSpec(shape, lambda *_: (0,) * nd, pipeline_mode=pl.Buffered(1))


def _adaln_kernel(c_ref, w_ref, b_ref, o_ref):
    c = c_ref[...]
    cond = (c * _sigmoid(c)).astype(BF16)
    o_ref[0] = jnp.dot(cond, w_ref[0], preferred_element_type=F32) + b_ref[0]


def _adaln(c_all, w_ada, b_ada):
    depth, d, n6 = w_ada.shape
    bc = c_all.shape[0]
    nblk = n6 // D_MODEL
    return pl.pallas_call(
        _adaln_kernel,
        grid=(depth, nblk),
        in_specs=[pl.BlockSpec((bc, d), lambda l, n: (0, 0)),
                  pl.BlockSpec((1, d, D_MODEL), lambda l, n: (l, 0, n)),
                  pl.BlockSpec((1, 1, D_MODEL), lambda l, n: (l, 0, n))],
        out_specs=pl.BlockSpec((1, bc, D_MODEL), lambda l, n: (l, 0, n)),
        out_shape=jax.ShapeDtypeStruct((depth, bc, n6), F32),
        compiler_params=_cparams(2), name="adaln",
    )(c_all, w_ada, b_ada.reshape(depth, 1, n6))


def _inproj_kernel(x_ref, mod_ref, g_ref, w_ref, cw_ref, st_ref,
                   ya_ref, q_ref, k_ref, v_ref, qi_ref, kiw_ref, ki_ref, sg_ref, tail_ref,
                   state_scr, *, tm):
    t = pl.program_id(1)

    @pl.when(t == 0)
    def _():
        state_scr[...] = st_ref[0]

    x = x_ref[0]
    h = _rmsnorm(x, g_ref[...]) * (1.0 + mod_ref[0, 1]) + mod_ref[0, 0]
    hb = h.astype(BF16)

    def seg(a, b):
        return jnp.dot(hb, w_ref[:, a:b], preferred_element_type=F32)

    bg = seg(C_CONV, C_CONV + D_CONV)
    u = seg(C_CONV + D_CONV, C_CONV + 2 * D_CONV) * seg(C_CONV + 2 * D_CONV, C_CONV + 3 * D_CONV)
    s0 = state_scr[0:1, :]
    s1 = state_scr[1:2, :]
    ri = lax.broadcasted_iota(I32, u.shape, 0)
    um1 = jnp.where(ri == 0, s1, pltpu.roll(u, 1, 0))
    um2 = jnp.where(ri == 0, s0, jnp.where(ri == 1, s1, pltpu.roll(u, 2, 0)))
    cw = cw_ref[...]
    y = cw[0:1] * um2
    y = y + cw[1:2] * um1
    y = y + cw[2:3] * u
    ya_ref[0] = (bg * y).astype(BF16)
    tail = u[tm - 2:tm, :]
    state_scr[...] = tail
    tail_ref[0] = tail

    q_ref[0] = (seg(C_Q, C_Q + D_Q) * Q_SCALE).astype(BF16)
    k_ref[0] = seg(C_KV, C_KV + D_KV)
    v_ref[0] = seg(C_KV + D_KV, C_KV + 2 * D_KV)
    half = D_QI // 2
    qi_ref[0, :, 0:half] = seg(C_QI, C_QI + half).astype(BF16)
    qi_ref[0, :, half:D_QI] = seg(C_QI + half, C_QI + D_QI).astype(BF16)
    kw = seg(C_KIW, C_KIW + LANES)
    ki_ref[0] = kw[:, 0:IDX_DIM]
    lane = lax.broadcasted_iota(I32, (1, LANES), 1)
    scale = jnp.where(lane < IDX_DIM, 1.0, jnp.where(lane < IDX_DIM + N_IDX_HEADS, WI_SCALE, 0.0))
    kiw_ref[0] = kw * scale

    for i in range(4):
        a = C_GATE + i * 512
        sg_ref[0, :, i * 512:(i + 1) * 512] = _sigmoid(seg(a, a + 512)).astype(BF16)


def _inproj(x, mod, g1, w_pack, conv_w, conv_state, tm):
    b, s, d = x.shape
    nt = s // tm
    row = lambda w, dt: jax.ShapeDtypeStruct((b, s, w), dt)
    rspec = lambda w: pl.BlockSpec((1, tm, w), lambda bi, ti: (bi, ti, 0))
    return pl.pallas_call(
        functools.partial(_inproj_kernel, tm=tm),
        grid=(b, nt),
        in_specs=[rspec(d),
                  pl.BlockSpec((1, 6, 1, d), lambda bi, ti: (bi, 0, 0, 0)),
                  _resident((1, d)),
                  _resident((d, D_PACK)),
                  _resident((CONV_W, D_CONV)),
                  pl.BlockSpec((1, CONV_W - 1, D_CONV), lambda bi, ti: (bi, 0, 0))],
        out_specs=[rspec(D_CONV), rspec(D_Q), rspec(D_KV), rspec(D_KV), rspec(D_QI),
                   rspec(LANES), rspec(IDX_DIM), rspec(2 * D_MODEL),
                   pl.BlockSpec((1, CONV_W - 1, D_CONV), lambda bi, ti: (bi, 0, 0))],
        out_shape=[row(D_CONV, BF16), row(D_Q, BF16), row(D_KV, F32), row(D_KV, F32),
                   row(D_QI, BF16), row(LANES, F32), row(IDX_DIM, F32), row(2 * D_MODEL, BF16),
                   jax.ShapeDtypeStruct((b, CONV_W - 1, D_CONV), F32)],
        scratch_shapes=[pltpu.VMEM((CONV_W - 1, D_CONV), F32)],
        compiler_params=_cparams(2), name="inproj",
    )(x, mod, g1, w_pack, conv_w, conv_state)


def _pair_blockdiag(tile, low_half):
    lane = lax.broadcasted_iota(I32, tile.shape, 1)
    lo = lane < HEAD_DIM
    swapped = pltpu.roll(tile, HEAD_DIM, 1)
    top = jnp.where(lo, tile if low_half else swapped, 0.0)
    bot = jnp.where(lo, 0.0, swapped if low_half else tile)
    return jnp.concatenate([top, bot], axis=0).astype(BF16)


def _attn_kernel(q_ref, qi_ref, kiw_ref, k_ref, v_ref, ki_ref, o_ref,
                 kbd_scr, vbd_scr, kibd_scr, key_scr, hi_scr, lo_scr, s_scr, m_scr, acc_scr,
                 *, tq, pos0, l_valid, topk):
    kb = KEY_BLOCK
    j = pl.program_id(1)
    nkc_all = key_scr.shape[0]
    npair = N_IDX_HEADS // 2
    nt = (((1,), (1,)), ((), ()))

    @pl.when(j == 0)
    def _():
        r_i = lax.broadcasted_iota(I32, (2 * kb, LANES), 0)
        c_i = lax.broadcasted_iota(I32, (2 * kb, LANES), 1)
        ones_bd = jnp.where((r_i < kb) == (c_i < HEAD_DIM), 1.0, 0.0).astype(BF16)

        def build(c, carry):
            rows = pl.ds(pl.multiple_of(c * kb, kb), kb)
            kibd_scr[c] = _pair_blockdiag(ki_ref[0, rows, :], True)
            for g in range(N_KV_HEADS):
                t = g // 2
                kbd_scr[g, c] = _pair_blockdiag(k_ref[0, rows, t * LANES:(t + 1) * LANES], g % 2 == 0)
                vbd_scr[g, c, :, 0:LANES] = _pair_blockdiag(v_ref[0, rows, t * LANES:(t + 1) * LANES],
                                                            g % 2 == 0)
                vbd_scr[g, c, :, LANES:2 * LANES] = ones_bd
            return carry

        lax.fori_loop(0, nkc_all, build, 0)

    w_t = kiw_ref[0].T

    row0 = pos0 + j * tq
    rows = row0 + lax.broadcasted_iota(I32, (1, tq), 1)
    lim = jnp.minimum(((rows >> 6) + 1) << 6, l_valid)
    last = row0 + tq - 1
    lim_max = jnp.minimum(((last >> 6) + 1) << 6, l_valid)
    nkc = (lim_max + kb - 1) >> 8

    def phase_a(c, carry):
        kbd = kibd_scr[c]
        acc = jnp.zeros((kb, tq), F32)
        for t in range(npair):
            d2 = lax.dot_general(kbd, qi_ref[0, :, t * LANES:(t + 1) * LANES], nt,
                                 preferred_element_type=F32)
            r = IDX_DIM + 2 * t
            acc = acc + w_t[r:r + 1, :] * jnp.maximum(d2[0:kb], 0.0)
            acc = acc + w_t[r + 1:r + 2, :] * jnp.maximum(d2[kb:2 * kb], 0.0)
        col = c * kb + lax.broadcasted_iota(I32, (kb, tq), 0)
        sc = jnp.where(col < lim, acc, -jnp.inf)
        bits = lax.bitcast_convert_type(sc, I32)
        key = bits ^ ((bits >> 31) & 0x7FFFFFFF)
        key_scr[c] = key
        hi_scr[c] = (key >> 16).astype(I16)
        lo_scr[c] = ((key & 0xFFFF) - 32768).astype(I16)
        return carry

    lax.fori_loop(0, nkc, phase_a, 0)

    def count_ge(cand):
        def body(c, cnt):
            m = jnp.where(key_scr[c] >= cand, 1.0, 0.0)
            parts = [m[8 * i:8 * (i + 1)] for i in range(kb // 8)]
            while len(parts) > 1:
                parts = [parts[i] + parts[i + 1] for i in range(0, len(parts), 2)]
            return cnt + parts[0]
        cnt = lax.fori_loop(0, nkc, body, jnp.zeros((8, tq), F32))
        return jnp.sum(cnt, axis=0, keepdims=True)

    def count16(ref, pred):
        def body(c, cnt):
            m = jnp.where(pred(ref[c]), jnp.int16(1), jnp.int16(0))
            parts = [m[16 * i:16 * (i + 1)] for i in range(kb // 16)]
            while len(parts) > 1:
                parts = [parts[i] + parts[i + 1] for i in range(0, len(parts), 2)]
            return cnt + parts[0]
        cnt = lax.fori_loop(0, nkc, body, jnp.zeros((16, tq), I16))
        return jnp.sum(cnt.astype(I32), axis=0, keepdims=True)

    def radix16(ref, need):
        t0 = jnp.where(count16(ref, lambda h: h >= jnp.int16(0)) >= need, 0, -32768).astype(I32)

        def bit_body(i, t):
            cand = t | lax.shift_left(jnp.int32(1), 14 - i)
            c16 = cand.astype(I16)
            return jnp.where(count16(ref, lambda h: h >= c16) >= need, cand, t)

        return lax.fori_loop(0, 15, bit_body, t0)

    t_hi = radix16(hi_scr, topk)
    h16 = t_hi.astype(I16)
    need_lo = topk - count16(hi_scr, lambda h: h > h16)

    def keep_bucket(c, carry):
        lo_scr[c] = jnp.where(hi_scr[c] == h16, lo_scr[c], jnp.int16(-32768))
        return carry

    lax.fori_loop(0, nkc, keep_bucket, 0)
    t_lo = radix16(lo_scr, need_lo)
    thr = (t_hi << 16) | ((t_lo + 32768) & 0xFFFF)
    kf = float(topk)
    few = lim <= topk
    thr = jnp.where(few, KEY_MIN_FINITE, thr)

    tie = jnp.logical_and(count_ge(thr) > kf, jnp.logical_not(few))
    any_tie = jnp.max(jnp.where(tie, 1.0, 0.0)) > 0.0

    @pl.when(any_tie)
    def _():
        need = kf - count_ge(thr + 1)
        r_i = lax.broadcasted_iota(I32, (kb, kb), 0)
        c_i = lax.broadcasted_iota(I32, (kb, kb), 1)
        tri = jnp.where(c_i < r_i, 1.0, 0.0).astype(BF16)

        def body(c, run):
            kc = key_scr[c]
            eqf = jnp.where(kc == thr, 1.0, 0.0)
            rank = run + jnp.dot(tri, eqf.astype(BF16), preferred_element_type=F32)
            drop = jnp.logical_and(jnp.logical_and(tie, kc == thr), rank >= need)
            key_scr[c] = jnp.where(drop, KEY_NEG_INF, kc)
            return run + jnp.sum(eqf, axis=0, keepdims=True)

        lax.fori_loop(0, nkc, body, jnp.zeros((1, tq), F32))

    nl = kb // LANES
    m_scr[...] = jnp.full(m_scr.shape, MASK_BIAS, F32)
    acc_scr[...] = jnp.zeros(acc_scr.shape, F32)

    def sweep1(c, carry):
        bias = jnp.where(key_scr[c] >= thr, 0.0, MASK_BIAS).T
        for g in range(N_KV_HEADS):
            s2 = lax.dot_general(q_ref[0, :, g * LANES:(g + 1) * LANES], kbd_scr[g, c], nt,
                                 preferred_element_type=F32)
            for r in range(GROUP):
                h = GROUP * g + r
                s = s2[:, r * kb:(r + 1) * kb] + bias
                s_scr[h, c] = s
                mx = m_scr[h]
                for i in range(nl):
                    mx = jnp.maximum(mx, s[:, i * LANES:(i + 1) * LANES])
                m_scr[h] = mx
        return carry

    lax.fori_loop(0, nkc, sweep1, 0)
    for h in range(N_HEADS):
        m_scr[h] = jnp.broadcast_to(jnp.max(m_scr[h], axis=1, keepdims=True), (tq, LANES))

    def sweep2(c, carry):
        for g in range(N_KV_HEADS):
            p2 = []
            for r in range(GROUP):
                h = GROUP * g + r
                s = s_scr[h, c]
                mb = m_scr[h]
                p2 += [jnp.exp2(s[:, i * LANES:(i + 1) * LANES] - mb) for i in range(nl)]
            pb = jnp.concatenate(p2, axis=1).astype(BF16)
            acc_scr[g] += jnp.dot(pb, vbd_scr[g, c], preferred_element_type=F32)
        return carry

    lax.fori_loop(0, nkc, sweep2, 0)
    for g in range(N_KV_HEADS):
        o_ref[0, :, g * LANES:(g + 1) * LANES] = (acc_scr[g, :, 0:LANES]
                                                  / acc_scr[g, :, LANES:2 * LANES]).astype(BF16)


def _attention(q, qi, kiw, k_all, v_all, ki_all, *, tq, pos0, l_valid, topk):
    b, s, _ = q.shape
    l_pad = k_all.shape[1]
    nkc_max = l_pad // KEY_BLOCK
    qspec = lambda w: pl.BlockSpec((1, tq, w), lambda bi, ji: (bi, ji, 0))
    kspec = lambda w: pl.BlockSpec((1, l_pad, w), lambda bi, ji: (bi, 0, 0))
    return pl.pallas_call(
        functools.partial(_attn_kernel, tq=tq, pos0=pos0, l_valid=l_valid, topk=topk),
        grid=(b, s // tq),
        in_specs=[qspec(D_Q), qspec(D_QI), qspec(LANES), kspec(D_KV), kspec(D_KV), kspec(LANES)],
        out_specs=qspec(D_Q),
        out_shape=jax.ShapeDtypeStruct((b, s, D_Q), BF16),
        scratch_shapes=[pltpu.VMEM((N_KV_HEADS, nkc_max, 2 * KEY_BLOCK, LANES), BF16),
                        pltpu.VMEM((N_KV_HEADS, nkc_max, 2 * KEY_BLOCK, 2 * LANES), BF16),
                        pltpu.VMEM((nkc_max, 2 * KEY_BLOCK, LANES), BF16),
                        pltpu.VMEM((nkc_max, KEY_BLOCK, tq), I32),
                        pltpu.VMEM((nkc_max, KEY_BLOCK, tq), I16),
                        pltpu.VMEM((nkc_max, KEY_BLOCK, tq), I16),
                        pltpu.VMEM((N_HEADS, nkc_max, tq, KEY_BLOCK), F32),
                        pltpu.VMEM((N_HEADS, tq, LANES), F32),
                        pltpu.VMEM((N_KV_HEADS, tq, 2 * LANES), F32)],
        compiler_params=_cparams(2), name="attention",
    )(q, qi, kiw, k_all, v_all, ki_all)


def _split3(a):
    hi = a.astype(BF16)
    r = a - hi.astype(F32)
    mid = r.astype(BF16)
    lo = (r - mid.astype(F32)).astype(BF16)
    return hi, mid, lo


def _merge_kernel(*refs, with_router):
    if with_router:
        (x_ref, ya_ref, ao_ref, sg_ref, mod_ref, wc_ref, wa_ref, wo_ref, g2_ref, rw_ref, rb_ref,
         x1_ref, h2_ref, lg_ref) = refs
    else:
        (x_ref, ya_ref, ao_ref, sg_ref, mod_ref, wc_ref, wa_ref, wo_ref, g2_ref,
         x1_ref, h2_ref) = refs
    a = jnp.dot(ya_ref[0], wc_ref[...], preferred_element_type=F32)
    b = jnp.dot(ao_ref[0], wa_ref[...], preferred_element_type=F32)
    merged = sg_ref[0, :, 0:D_MODEL].astype(F32) * a + sg_ref[0, :, D_MODEL:2 * D_MODEL].astype(F32) * b
    mix = jnp.dot(merged.astype(BF16), wo_ref[...], preferred_element_type=F32)
    x1 = x_ref[0] + mod_ref[0, 2] * mix
    x1_ref[0] = x1
    h2 = _rmsnorm(x1, g2_ref[...]) * (1.0 + mod_ref[0, 4]) + mod_ref[0, 3]
    h2_ref[0] = h2.astype(h2_ref.dtype)
    if with_router:
        hs = _split3(h2)
        acc = jnp.zeros((h2.shape[0], LANES), F32)
        for i, jj in ((0, 0), (0, 1), (1, 0)):
            acc = acc + jnp.dot(hs[i], rw_ref[jj], preferred_element_type=F32)
        lane = lax.broadcasted_iota(I32, acc.shape, 1)
        lg = jnp.where(lane < N_EXPERTS, acc + rb_ref[...], -jnp.inf)
        m1 = jnp.max(lg, axis=1, keepdims=True)
        i1 = jnp.min(jnp.where(lg == m1, lane, LANES), axis=1, keepdims=True)
        rest = jnp.where(lane == i1, -jnp.inf, lg)
        m2 = jnp.max(rest, axis=1, keepdims=True)
        i2 = jnp.min(jnp.where(rest == m2, lane, LANES), axis=1, keepdims=True)
        e2 = jnp.exp(m2 - m1)
        den = 1.0 + e2
        lg_ref[0] = jnp.where(lane == 0, i1.astype(F32),
                              jnp.where(lane == 1, i2.astype(F32),
                                        jnp.where(lane == 2, 1.0 / den,
                                                  jnp.where(lane == 3, e2 / den, 0.0))))


def _merge(x, ya, ao, sg, mod, wc, wa, wo, g2, router=None, *, tm):
    b, s, d = x.shape
    r = mod.shape[2]
    rspec = lambda w: pl.BlockSpec((1, tm, w), lambda bi, ti: (bi, ti, 0))
    mspec = (pl.BlockSpec((1, 6, 1, d), lambda bi, ti: (bi, 0, 0, 0)) if r == 1 else
             pl.BlockSpec((1, 6, tm, d), lambda bi, ti: (bi, 0, ti, 0)))
    in_specs = [rspec(d), rspec(D_CONV), rspec(D_Q), rspec(2 * d), mspec,
                _resident((D_CONV, d)), _resident((D_Q, d)), _resident((d, d)), _resident((1, d))]
    args = [x, ya, ao, sg, mod, wc, wa, wo, g2]
    out_specs = [rspec(d), rspec(d)]
    out_shape = [jax.ShapeDtypeStruct((b, s, d), F32),
                 jax.ShapeDtypeStruct((b, s, d), BF16 if router is None else F32)]
    if router is not None:
        in_specs += [_resident((3, d, LANES)), _resident((1, LANES))]
        args += list(router)
        out_specs.append(rspec(LANES))
        out_shape.append(jax.ShapeDtypeStruct((b, s, LANES), F32))
    return pl.pallas_call(
        functools.partial(_merge_kernel, with_router=router is not None),
        grid=(b, s // tm),
        in_specs=in_specs, out_specs=out_specs, out_shape=out_shape,
        compiler_params=_cparams(2), name="merge",
    )(*args)


def _ffn_kernel(h2_ref, x1_ref, mod_ref, w1_ref, w3_ref, w2_ref, o_ref, acc_scr, *, nc):
    h = h2_ref[0]
    acc_scr[...] = _swiglu_chunk(h, w1_ref[0], w3_ref[0], w2_ref[0])

    def body(c, carry):
        acc_scr[...] += _swiglu_chunk(h, w1_ref[c], w3_ref[c], w2_ref[c])
        return carry

    lax.fori_loop(1, nc, body, 0)
    o_ref[0] = x1_ref[0] + mod_ref[0, 5] * acc_scr[...]


def _ffn(h2, x1, mod, w1, w3, w2, *, tm):
    b, s, d = x1.shape
    nc = w1.shape[0]
    r = mod.shape[2]
    rspec = lambda w: pl.BlockSpec((1, tm, w), lambda bi, ti: (bi, ti, 0))
    mspec = (pl.BlockSpec((1, 6, 1, d), lambda bi, ti: (bi, 0, 0, 0)) if r == 1 else
             pl.BlockSpec((1, 6, tm, d), lambda bi, ti: (bi, 0, ti, 0)))
    return pl.pallas_call(
        functools.partial(_ffn_kernel, nc=nc),
        grid=(b, s // tm),
        in_specs=[rspec(d), rspec(d), mspec, _resident(w1.shape), _resident(w3.shape), _resident(w2.shape)],
        out_specs=rspec(d),
        out_shape=jax.ShapeDtypeStruct((b, s, d), F32),
        scratch_shapes=[pltpu.VMEM((tm, d), F32)],
        compiler_params=_cparams(2), name="ffn",
    )(h2, x1, mod, w1, w3, w2)


def _swiglu_chunk(xb, w1, w3, w2):
    a = jnp.dot(xb, w1, preferred_element_type=F32)
    hm = (a * _sigmoid(a) * jnp.dot(xb, w3, preferred_element_type=F32)).astype(BF16)
    return jnp.dot(hm, w2, preferred_element_type=F32)


def _moe_dense_kernel(h2_ref, x1_ref, rt_ref, mod_ref, w1_ref, w3_ref, w2_ref, fg_ref, o_ref,
                      eacc_scr, acc_scr, *, final_norm):
    e = pl.program_id(2)
    c = pl.program_id(3)
    ne = pl.num_programs(2)
    nc = pl.num_programs(3)

    @pl.when(jnp.logical_and(e == 0, c == 0))
    def _():
        acc_scr[...] = jnp.zeros(acc_scr.shape, F32)

    @pl.when(c == 0)
    def _():
        eacc_scr[...] = jnp.zeros(eacc_scr.shape, F32)

    eacc_scr[...] += _swiglu_chunk(h2_ref[0].astype(BF16), w1_ref[0], w3_ref[0], w2_ref[0])

    @pl.when(c == nc - 1)
    def _():
        rt = rt_ref[0]
        ef = e.astype(F32)
        gate = (jnp.where(rt[:, 0:1] == ef, rt[:, 2:3], 0.0)
                + jnp.where(rt[:, 1:2] == ef, rt[:, 3:4], 0.0))
        acc_scr[...] += gate * eacc_scr[...]

    @pl.when(jnp.logical_and(e == ne - 1, c == nc - 1))
    def _():
        x2 = x1_ref[0] + mod_ref[0, 5] * acc_scr[...]
        o_ref[0] = _rmsnorm(x2, fg_ref[...]) if final_norm else x2


def _moe_dense(h2, x1, route, mod, w1, w3, w2, fg, *, tm, final_norm):
    b, s, d = x1.shape
    ne, _, f = w1.shape
    fc = MOE_CHUNK
    r = mod.shape[2]
    rspec = lambda w: pl.BlockSpec((1, tm, w), lambda bi, ti, ei, ci: (bi, ti, 0))
    mspec = (pl.BlockSpec((1, 6, 1, d), lambda bi, ti, ei, ci: (bi, 0, 0, 0)) if r == 1 else
             pl.BlockSpec((1, 6, tm, d), lambda bi, ti, ei, ci: (bi, 0, ti, 0)))
    wcol = pl.BlockSpec((1, d, fc), lambda bi, ti, ei, ci: (ei, 0, ci))
    return pl.pallas_call(
        functools.partial(_moe_dense_kernel, final_norm=final_norm),
        grid=(b, s // tm, ne, f // fc),
        in_specs=[rspec(d), rspec(d), rspec(LANES), mspec, wcol, wcol,
                  pl.BlockSpec((1, fc, d), lambda bi, ti, ei, ci: (ei, ci, 0)),
                  _resident((1, d))],
        out_specs=rspec(d),
        out_shape=jax.ShapeDtypeStruct((b, s, d), F32),
        scratch_shapes=[pltpu.VMEM((tm, d), F32), pltpu.VMEM((tm, d), F32)],
        compiler_params=_cparams(4), name="moe_dense",
    )(h2, x1, route, mod, w1, w3, w2, fg)


def _route_tables(route, tmr):
    n = route.shape[0]
    e1 = route[:, 0].astype(I32)
    e2 = route[:, 1].astype(I32)
    ar = jnp.arange(N_EXPERTS, dtype=I32)
    oh1 = e1[:, None] == ar
    oh2 = e2[:, None] == ar
    oh = oh1.astype(I32) + oh2.astype(I32)
    cum = jnp.cumsum(oh, axis=0)
    excl = cum - oh
    cnt = cum[-1]
    tiles_e = (cnt + tmr - 1) // tmr
    tile_end = jnp.cumsum(tiles_e)
    tile_start = tile_end - tiles_e
    row_off = tile_start * tmr
    d1 = row_off[e1] + jnp.sum(jnp.where(oh1, excl, 0), axis=1)
    d2 = row_off[e2] + jnp.sum(jnp.where(oh2, excl, 0), axis=1)
    nt = -(-2 * n // tmr) + N_EXPERTS
    tok = jnp.arange(n, dtype=I32)
    pair = jnp.zeros((nt * tmr,), I32).at[jnp.concatenate([d1, d2])].set(
        jnp.concatenate([2 * tok, 2 * tok + 1]), unique_indices=True)
    src = pair >> 1
    dst = (pair & 1) * n + src
    tile = jnp.arange(nt, dtype=I32)
    te = jnp.minimum(jnp.searchsorted(tile_end, tile, side="right"), N_EXPERTS - 1).astype(I32)
    nv = jnp.where(tile < tile_end[-1], jnp.clip(cnt[te] - (tile - tile_start[te]) * tmr, 0, tmr), 0)
    return (src.reshape(nt, 1, tmr), dst.reshape(nt, 1, tmr), te, nv.astype(I32),
            tile_end[-1:].astype(I32))


def _moe_routed_kernel(te_ref, nv_ref, nu_ref, src0_ref, srcn_ref, dstp_ref, h2_hbm, w1_ref, w3_ref, w2_ref,
                       out_hbm, xbuf, xb_scr, ybuf, gsem, ssem, *, tmr, nc, gs):
    i = pl.program_id(0)
    c = pl.program_id(1)
    nu = nu_ref[0]
    slot = i % 2
    other = 1 - slot
    rpc = tmr // nc
    rpg = tmr // gs

    def gather(src_ref, cc, k, sl):
        return pltpu.make_async_copy(h2_hbm.at[pl.ds(src_ref[0, 0, cc * rpg + k], 1)],
                                     xbuf.at[sl, cc, pl.ds(k, 1)], gsem.at[sl])

    def scatter(cc, k, sl):
        return pltpu.make_async_copy(ybuf.at[sl, cc, pl.ds(k, 1)],
                                     out_hbm.at[pl.ds(dstp_ref[0, 0, cc * rpc + k], 1)], ssem.at[sl])

    @pl.when(jnp.logical_and(i == 0, c == 0))
    def _():
        def body(cc, carry):
            for k in range(rpg):
                gather(src0_ref, cc, k, 0).start()
            return carry
        lax.fori_loop(0, gs, body, 0)

    @pl.when(jnp.logical_and(c == 0, i < nu))
    def _():
        pltpu.make_async_copy(xbuf.at[slot], xbuf.at[slot], gsem.at[slot]).wait()
        xb_scr[...] = xbuf[slot].reshape(tmr, xb_scr.shape[1]).astype(BF16)

    last = nv_ref.shape[0] - 1
    nv_drain = nv_ref[jnp.clip(i - 2, 0, last)]
    push = jnp.logical_and(i >= 1, i - 1 < nu)
    nv_push = nv_ref[jnp.clip(i - 1, 0, last)]
    fast = jnp.logical_and(jnp.logical_and(i >= 2, i + 1 < nu),
                           jnp.logical_and(nv_push == tmr, nv_drain == tmr))
    slow = jnp.logical_not(fast)
    drain = jnp.logical_and(jnp.logical_and(slow, c == 0), jnp.logical_and(i >= 2, i - 2 < nu))

    def drain_all():
        pltpu.make_async_copy(ybuf.at[slot], ybuf.at[slot], ssem.at[slot]).wait()

    @pl.when(jnp.logical_and(drain, nv_drain == tmr))
    def _():
        drain_all()

    @pl.when(jnp.logical_and(drain, nv_drain < tmr))
    def _():
        def body(r, carry):
            scatter(0, 0, slot).wait()
            return carry
        lax.fori_loop(0, nv_drain, body, 0)

    def fast_step(first):
        xb = xb_scr[...]
        a = jnp.dot(xb, w1_ref[0], preferred_element_type=F32)
        b = jnp.dot(xb, w3_ref[0], preferred_element_type=F32)
        for k in range(rpg):
            gather(srcn_ref, c, k, other).start()
        for k in range(rpc):
            scatter(c, k, other).start()
        hm = (a * _sigmoid(a) * b).astype(BF16)
        y = jnp.dot(hm, w2_ref[0], preferred_element_type=F32).reshape(nc, rpc, xb_scr.shape[1])
        if first:
            drain_all()
            ybuf[slot] = y
        else:
            ybuf[slot] += y

    @pl.when(jnp.logical_and(fast, c == 0))
    def _():
        fast_step(True)

    @pl.when(jnp.logical_and(fast, c > 0))
    def _():
        fast_step(False)

    @pl.when(jnp.logical_and(slow, jnp.logical_and(i + 1 < nu, c < gs)))
    def _():
        for k in range(rpg):
            gather(srcn_ref, c, k, other).start()

    @pl.when(jnp.logical_and(slow, jnp.logical_and(push, nv_push == tmr)))
    def _():
        for k in range(rpc):
            scatter(c, k, other).start()

    @pl.when(jnp.logical_and(slow, jnp.logical_and(push, nv_push < tmr)))
    def _():
        def body(k, carry):
            scatter(c, k, other).start()
            return carry
        lax.fori_loop(0, jnp.clip(nv_push - c * rpc, 0, rpc), body, 0)

    @pl.when(jnp.logical_and(slow, i < nu))
    def _():
        y = _swiglu_chunk(xb_scr[...], w1_ref[0], w3_ref[0], w2_ref[0]).reshape(nc, rpc, xb_scr.shape[1])

        @pl.when(c == 0)
        def _():
            ybuf[slot] = y

        @pl.when(c > 0)
        def _():
            ybuf[slot] += y


def _moe_routed(h2, route, w1, w3, w2, *, tmr):
    n, d = h2.shape
    f = w1.shape[2]
    fc = MOE_CHUNK
    nc = f // fc
    gs = MOE_GATHER_STEPS
    src, dst, te, nv, nu = _route_tables(route, tmr)
    nt = src.shape[0]
    last = nt - 1
    smem = lambda imap: pl.BlockSpec((1, 1, tmr), imap, memory_space=pltpu.SMEM)
    chunk = lambda i, c, nu_r: jnp.where(i < nu_r[0], c, nc - 1)
    wcol = pl.BlockSpec((1, d, fc), lambda i, c, te_r, nv_r, nu_r: (te_r[jnp.minimum(i, last)], 0, chunk(i, c, nu_r)))
    wrow = pl.BlockSpec((1, fc, d), lambda i, c, te_r, nv_r, nu_r: (te_r[jnp.minimum(i, last)], chunk(i, c, nu_r), 0))
    return pl.pallas_call(
        functools.partial(_moe_routed_kernel, tmr=tmr, nc=nc, gs=gs),
        grid_spec=pltpu.PrefetchScalarGridSpec(
            num_scalar_prefetch=3,
            grid=(nt + 2, nc),
            in_specs=[smem(lambda i, c, te_r, nv_r, nu_r: (0, 0, 0)),
                      smem(lambda i, c, te_r, nv_r, nu_r: (jnp.minimum(i + 1, last), 0, 0)),
                      smem(lambda i, c, te_r, nv_r, nu_r: (jnp.clip(i - 1, 0, last), 0, 0)),
                      pl.BlockSpec(memory_space=pl.ANY), wcol, wcol, wrow],
            out_specs=pl.BlockSpec(memory_space=pl.ANY),
            scratch_shapes=[pltpu.VMEM((2, gs, tmr // gs, d), F32), pltpu.VMEM((tmr, d), BF16),
                            pltpu.VMEM((2, nc, tmr // nc, d), F32),
                            pltpu.SemaphoreType.DMA((2,)), pltpu.SemaphoreType.DMA((2,))]),
        out_shape=jax.ShapeDtypeStruct((2 * n, d), F32),
        compiler_params=_cparams(2), name="moe_routed",
    )(te, nv, nu, src, src, dst, h2, w1, w3, w2)


def _moe_combine_kernel(x1_ref, ya_ref, yb_ref, rt_ref, mod_ref, fg_ref, o_ref, *, final_norm):
    rt = rt_ref[0]
    f = rt[:, 2:3] * ya_ref[...] + rt[:, 3:4] * yb_ref[...]
    x2 = x1_ref[0] + mod_ref[0, 5] * f
    o_ref[0] = _rmsnorm(x2, fg_ref[...]) if final_norm else x2


def _moe_combine(x1, y2, route, mod, fg, *, tm, final_norm):
    b, s, d = x1.shape
    nt = s // tm
    rspec = lambda w: pl.BlockSpec((1, tm, w), lambda bi, ti: (bi, ti, 0))
    return pl.pallas_call(
        functools.partial(_moe_combine_kernel, final_norm=final_norm),
        grid=(b, nt),
        in_specs=[rspec(d),
                  pl.BlockSpec((tm, d), lambda bi, ti: (bi * nt + ti, 0)),
                  pl.BlockSpec((tm, d), lambda bi, ti: (b * nt + bi * nt + ti, 0)),
                  rspec(LANES),
                  pl.BlockSpec((1, 6, 1, d), lambda bi, ti: (bi, 0, 0, 0)), _resident((1, d))],
        out_specs=rspec(d),
        out_shape=jax.ShapeDtypeStruct((b, s, d), F32),
        compiler_params=_cparams(2), name="moe_combine",
    )(x1, y2, y2, route, mod, fg)


def _pack_w_in(w):
    cut = C_KIW + IDX_DIM + N_IDX_HEADS
    pad = jnp.zeros((w.shape[0], C_GATE - cut), w.dtype)
    return jnp.concatenate([w[:, :cut], pad, w[:, cut:]], axis=1).astype(BF16)


def _chunk_cols(w, fc):
    *lead, d, f = w.shape
    return jnp.moveaxis(w.astype(BF16).reshape(*lead, d, f // fc, fc), -2, -3)


def _chunk_w2(w2, fc):
    *lead, f, d = w2.shape
    return w2.reshape(*lead, f // fc, fc, d).astype(BF16)


def _router_terms(rw, rb):
    d = rw.shape[0]
    rwp = jnp.zeros((d, LANES), F32).at[:, :N_EXPERTS].set(rw)
    hi, mid, lo = _split3(rwp)
    rbp = jnp.zeros((1, LANES), F32).at[0, :N_EXPERTS].set(rb)
    return jnp.stack([hi, mid, lo]), rbp


def _trunk(x, mod_all, caches, p, *, tm, tq):
    b, t, d = x.shape
    depth = mod_all.shape[0]
    flat = caches is not None
    ks, vs, kis, tails = [], [], [], []
    for l in range(depth):
        mod = mod_all[l].reshape(b, 6, 1, d)
        if caches is None:
            conv_state = jnp.zeros((b, CONV_W - 1, D_CONV), F32)
        else:
            conv_state = caches[3][l]
        ya, q, k, v, qi, kiw, ki, sg, tail = _inproj(
            x, mod, p["norm1_g"][l], p["w_in"][l], p["conv_w"][l], conv_state, tm)
        if caches is None:
            k_all, v_all, ki_all = k, v, kiw
            pos0, l_valid = 0, t
            ao = _attention(q, qi, kiw, k_all, v_all, ki_all, tq=tq, pos0=0, l_valid=t,
                            topk=min(TOPK_MAX, t // 4))
        else:
            past = caches[0][l].shape[1]
            l_valid = past + t
            l_pad = -(-l_valid // KEY_BLOCK) * KEY_BLOCK
            cat = lambda old, new: jnp.concatenate(
                [old.reshape(b, past, -1), new, jnp.zeros((b, l_pad - l_valid, new.shape[-1]), F32)], axis=1)
            k_all, v_all = cat(caches[0][l], k), cat(caches[1][l], v)
            ki_all = jnp.pad(cat(caches[2][l], ki), ((0, 0), (0, 0), (0, LANES - IDX_DIM)))
            padq = lambda a: jnp.pad(a, ((0, 0), (0, tq - t), (0, 0)))
            ao = _attention(padq(q), padq(qi), padq(kiw), k_all, v_all, ki_all, tq=tq, pos0=past,
                            l_valid=l_valid, topk=min(TOPK_MAX, l_valid // 4))[:, :t]
        is_moe = l % 2 == 1
        router = (p["router_w"][l // 2], p["router_b"][l // 2]) if is_moe else None
        if flat:
            n = b * t
            fl = lambda a: a.reshape(1, n, a.shape[-1])
            modf = jnp.repeat(mod_all[l].reshape(b, 6, d), t, axis=0).transpose(1, 0, 2)[None]
            xs, yas, aos, sgs, tmf = fl(x), fl(ya), fl(ao), fl(sg), n
        else:
            modf, xs, yas, aos, sgs, tmf = mod, x, ya, ao, sg, tm
        outs = _merge(xs, yas, aos, sgs, modf, p["w_conv_out"][l], p["w_attn_out"][l], p["w_o"][l],
                      p["norm2_g"][l], router, tm=tmf)
        if is_moe and flat:
            x1, h2, route = outs
            x = _moe_dense(h2, x1, route, modf, p["moe_w1"][l // 2], p["moe_w3"][l // 2],
                           p["moe_w2"][l // 2], p["final_g"], tm=xs.shape[1], final_norm=l == depth - 1)
        elif is_moe:
            x1, h2, route = outs
            y2 = _moe_routed(h2.reshape(b * t, d), route.reshape(b * t, LANES), p["moe_w1"][l // 2],
                             p["moe_w3"][l // 2], p["moe_w2"][l // 2], tmr=MOE_ROWS)
            x = _moe_combine(x1, y2, route, mod, p["final_g"], tm=tm,
                             final_norm=l == depth - 1)
        else:
            x1, h2 = outs
            x = _ffn(h2, x1, modf, p["ffn_w1"][l // 2], p["ffn_w3"][l // 2], p["ffn_w2"][l // 2],
                     tm=tmf if flat else min(2 * tm, t))
        x = x.reshape(b, t, d)
        ks.append(k.reshape(b, t, N_KV_HEADS, HEAD_DIM))
        vs.append(v.reshape(b, t, N_KV_HEADS, HEAD_DIM))
        kis.append(ki)
        tails.append(tail)
    return x, jnp.stack(ks), jnp.stack(vs), jnp.stack(kis), jnp.stack(tails)


def kernel(x_prompt, x_sample, c_prompt, c_sample, cache_k, cache_v, cache_idx_k, state_conv, w_ada, b_ada, norm1_g, w_in, conv_w, w_conv_out, w_attn_out, w_o, norm2_g, ffn_w1, ffn_w3, ffn_w2, router_w, router_b, moe_w1, moe_w3, moe_w2, final_g):
    depth = w_in.shape[0]
    bp = x_prompt.shape[0]
    rterms = [_router_terms(router_w[i], router_b[i]) for i in range(router_w.shape[0])]
    p = {
        "norm1_g": norm1_g.reshape(depth, 1, -1),
        "norm2_g": norm2_g.reshape(depth, 1, -1),
        "final_g": final_g.reshape(1, -1),
        "w_in": jnp.stack([_pack_w_in(w_in[l]) for l in range(depth)]),
        "conv_w": conv_w,
        "w_conv_out": w_conv_out.astype(BF16),
        "w_attn_out": w_attn_out.astype(BF16),
        "w_o": w_o.astype(BF16),
        "ffn_w1": _chunk_cols(ffn_w1, FFN_CHUNK),
        "ffn_w3": _chunk_cols(ffn_w3, FFN_CHUNK),
        "ffn_w2": _chunk_w2(ffn_w2, FFN_CHUNK),
        "moe_w1": moe_w1.astype(BF16),
        "moe_w3": moe_w3.astype(BF16),
        "moe_w2": moe_w2.astype(BF16),
        "router_w": [r[0] for r in rterms],
        "router_b": [r[1] for r in rterms],
    }
    mod_all = _adaln(jnp.concatenate([c_prompt, c_sample], axis=0), w_ada.astype(BF16), b_ada)
    seq = x_prompt.shape[1]
    tm = min(512, seq)
    tq = min(256, seq)
    yp, kp, vp, kip, cp = _trunk(x_prompt, mod_all[:, :bp], None, p, tm=tm, tq=tq)
    ts = x_sample.shape[1]
    ys, ksm, vsm, kism, csm = _trunk(x_sample, mod_all[:, bp:], (cache_k, cache_v, cache_idx_k, state_conv), p,
                                     tm=ts, tq=-(-ts // LANES) * LANES)
    return (yp, ys, kp, vp, kip, cp, ksm, vsm, kism, csm)
```

```python
import functools

import jax
import jax.numpy as jnp
from jax import lax
from jax.experimental import pallas as pl
from jax.experimental.pallas import tpu as pltpu

F32 = jnp.float32
BF16 = jnp.bfloat16
I32 = jnp.int32
I16 = jnp.int16

D_MODEL = 1024
D_CONV = 512
CONV_W = 3
N_HEADS = 8
N_KV_HEADS = 4
HEAD_DIM = 64
GROUP = N_HEADS // N_KV_HEADS
N_IDX_HEADS = 16
IDX_DIM = 64
TOPK_MAX = 256
CHUNK = 64
D_FF = 2816
N_EXPERTS = 8
D_FF_EXPERT = 3584
EPS = 1e-6

D_Q = N_HEADS * HEAD_DIM
D_KV = N_KV_HEADS * HEAD_DIM
D_QI = N_IDX_HEADS * IDX_DIM
LANES = 128
C_CONV = 0
C_Q = 3 * D_CONV
C_KV = C_Q + D_Q
C_QI = C_KV + 2 * D_KV
C_KIW = C_QI + D_QI
C_GATE = C_KIW + LANES
D_PACK = C_GATE + 2 * D_MODEL
WI_SCALE = (IDX_DIM ** -0.5) * (N_IDX_HEADS ** -0.5)
Q_SCALE = HEAD_DIM ** -0.5 * 1.4426950408889634

KEY_BLOCK = 256
FFN_CHUNK = 256
MOE_CHUNK = 512
MOE_ROWS = (D_FF_EXPERT // MOE_CHUNK) * 128
MOE_GATHER_STEPS = D_FF_EXPERT // MOE_CHUNK
V7X_VMEM_LIMIT = 56 * 1024 * 1024

KEY_NEG_INF = -2139095041
KEY_MIN_FINITE = KEY_NEG_INF + 1
MASK_BIAS = -1e30


def _sigmoid(x):
    return 1.0 / (1.0 + jnp.exp(-x))


def _rmsnorm(x, g):
    ms = jnp.mean(x * x, axis=-1, keepdims=True)
    return x * lax.rsqrt(ms + EPS) * g


def _cparams(n_axes):
    return pltpu.CompilerParams(dimension_semantics=("arbitrary",) * n_axes,
                                vmem_limit_bytes=V7X_VMEM_LIMIT)


def _resident(shape):
    nd = len(shape)
    return pl.BlockSpec(shape, lambda *_: (0,) * nd, pipeline_mode=pl.Buffered(1))


def _adaln_kernel(c_ref, w_ref, b_ref, o_ref):
    c = c_ref[...]
    cond = (c * _sigmoid(c)).astype(BF16)
    o_ref[0] = jnp.dot(cond, w_ref[0], preferred_element_type=F32) + b_ref[0]


def _adaln(c_all, w_ada, b_ada):
    depth, d, n6 = w_ada.shape
    bc = c_all.shape[0]
    nblk = n6 // D_MODEL
    return pl.pallas_call(
        _adaln_kernel,
        grid=(depth, nblk),
        in_specs=[pl.BlockSpec((bc, d), lambda l, n: (0, 0)),
                  pl.BlockSpec((1, d, D_MODEL), lambda l, n: (l, 0, n)),
                  pl.BlockSpec((1, 1, D_MODEL), lambda l, n: (l, 0, n))],
        out_specs=pl.BlockSpec((1, bc, D_MODEL), lambda l, n: (l, 0, n)),
        out_shape=jax.ShapeDtypeStruct((depth, bc, n6), F32),
        compiler_params=_cparams(2), name="adaln",
    )(c_all, w_ada, b_ada.reshape(depth, 1, n6))


def _inproj_kernel(*refs, tm, layer):
    if layer == 0:
        (x_ref, mod_ref, g_ref, w_ref, cw_ref, st_ref,
         ya_ref, q_ref, k_ref, v_ref, qi_ref, kiw_ref, ki_ref, sg_ref, tail_ref, state_scr) = refs
    else:
        (x_ref, mod_ref, g_ref, w_ref, cw_ref, st_ref, _, _, _,
         ya_ref, q_ref, k_ref, v_ref, qi_ref, kiw_ref, ki_ref, sg_ref, tail_ref, state_scr) = refs

    def put(ref, val):
        if layer == 0:
            ref[0, 0] = val
            for other_layer in range(1, ref.shape[0]):
                ref[other_layer, 0] = jnp.zeros_like(val)
        else:
            ref[0, 0] = val

    t = pl.program_id(1)

    @pl.when(t == 0)
    def _():
        state_scr[...] = st_ref[0]

    x = x_ref[0]
    h = _rmsnorm(x, g_ref[...]) * (1.0 + mod_ref[0, 1]) + mod_ref[0, 0]
    hb = h.astype(BF16)

    def seg(a, b):
        return jnp.dot(hb, w_ref[:, a:b], preferred_element_type=F32)

    bg = seg(C_CONV, C_CONV + D_CONV)
    u = seg(C_CONV + D_CONV, C_CONV + 2 * D_CONV) * seg(C_CONV + 2 * D_CONV, C_CONV + 3 * D_CONV)
    s0 = state_scr[0:1, :]
    s1 = state_scr[1:2, :]
    ri = lax.broadcasted_iota(I32, u.shape, 0)
    um1 = jnp.where(ri == 0, s1, pltpu.roll(u, 1, 0))
    um2 = jnp.where(ri == 0, s0, jnp.where(ri == 1, s1, pltpu.roll(u, 2, 0)))
    cw = cw_ref[...]
    y = cw[0:1] * um2
    y = y + cw[1:2] * um1
    y = y + cw[2:3] * u
    ya_ref[0] = (bg * y).astype(BF16)
    tail = u[tm - 2:tm, :]
    state_scr[...] = tail
    tail_ref[0] = tail

    q_ref[0] = (seg(C_Q, C_Q + D_Q) * Q_SCALE).astype(BF16)
    put(k_ref, seg(C_KV, C_KV + D_KV))
    put(v_ref, seg(C_KV + D_KV, C_KV + 2 * D_KV))
    half = D_QI // 2
    qi_ref[0, :, 0:half] = seg(C_QI, C_QI + half).astype(BF16)
    qi_ref[0, :, half:D_QI] = seg(C_QI + half, C_QI + D_QI).astype(BF16)
    kw = seg(C_KIW, C_KIW + LANES)
    put(ki_ref, kw[:, 0:IDX_DIM])
    lane = lax.broadcasted_iota(I32, (1, LANES), 1)
    scale = jnp.where(lane < IDX_DIM, 1.0, jnp.where(lane < IDX_DIM + N_IDX_HEADS, WI_SCALE, 0.0))
    kiw_ref[0] = kw * scale

    for i in range(4):
        a = C_GATE + i * 512
        sg_ref[0, :, i * 512:(i + 1) * 512] = _sigmoid(seg(a, a + 512)).astype(BF16)


def _inproj(x, mod, g1, w_pack, conv_w, conv_state, tm, *, layer, depth, stacks):
    b, s, d = x.shape
    nt = s // tm
    row = lambda w, dt: jax.ShapeDtypeStruct((b, s, w), dt)
    stk = lambda w: jax.ShapeDtypeStruct((depth, b, s, w), F32)
    rspec = lambda w: pl.BlockSpec((1, tm, w), lambda bi, ti: (bi, ti, 0))
    if layer == 0:
        sspec = lambda w: pl.BlockSpec((depth, 1, tm, w), lambda bi, ti: (0, bi, ti, 0))
    else:
        sspec = lambda w: pl.BlockSpec((1, 1, tm, w), lambda bi, ti: (layer, bi, ti, 0))
    in_specs = [rspec(d),
                pl.BlockSpec((1, 6, 1, d), lambda bi, ti: (bi, 0, 0, 0)),
                _resident((1, d)),
                _resident((d, D_PACK)),
                _resident((CONV_W, D_CONV)),
                pl.BlockSpec((1, CONV_W - 1, D_CONV), lambda bi, ti: (bi, 0, 0))]
    args = [x, mod, g1, w_pack, conv_w, conv_state]
    aliases = {}
    if layer > 0:
        in_specs += [pl.BlockSpec(memory_space=pl.ANY)] * 3
        aliases = {len(args): 2, len(args) + 1: 3, len(args) + 2: 6}
        args += list(stacks)
    return pl.pallas_call(
        functools.partial(_inproj_kernel, tm=tm, layer=layer),
        grid=(b, nt),
        in_specs=in_specs,
        out_specs=[rspec(D_CONV), rspec(D_Q), sspec(D_KV), sspec(D_KV), rspec(D_QI),
                   rspec(LANES), sspec(IDX_DIM), rspec(2 * D_MODEL),
                   pl.BlockSpec((1, CONV_W - 1, D_CONV), lambda bi, ti: (bi, 0, 0))],
        out_shape=[row(D_CONV, BF16), row(D_Q, BF16), stk(D_KV), stk(D_KV),
                   row(D_QI, BF16), row(LANES, F32), stk(IDX_DIM), row(2 * D_MODEL, BF16),
                   jax.ShapeDtypeStruct((b, CONV_W - 1, D_CONV), F32)],
        scratch_shapes=[pltpu.VMEM((CONV_W - 1, D_CONV), F32)],
        input_output_aliases=aliases,
        compiler_params=_cparams(2), name="inproj",
    )(*args)


def _pair_blockdiag(tile, low_half):
    lane = lax.broadcasted_iota(I32, tile.shape, 1)
    lo = lane < HEAD_DIM
    swapped = pltpu.roll(tile, HEAD_DIM, 1)
    top = jnp.where(lo, tile if low_half else swapped, 0.0)
    bot = jnp.where(lo, 0.0, swapped if low_half else tile)
    return jnp.concatenate([top, bot], axis=0).astype(BF16)


def _attn_kernel(q_ref, qi_ref, kiw_ref, k_ref, v_ref, ki_ref, o_ref,
                 kbd_scr, vbd_scr, kibd_scr, key_scr, hi_scr, lo_scr, s_scr, m_scr, acc_scr,
                 *, tq, pos0, l_valid, topk):
    kb = KEY_BLOCK
    j = pl.program_id(1)
    nkc_all = key_scr.shape[0]
    npair = N_IDX_HEADS // 2
    nt = (((1,), (1,)), ((), ()))

    @pl.when(j == 0)
    def _():
        r_i = lax.broadcasted_iota(I32, (2 * kb, LANES), 0)
        c_i = lax.broadcasted_iota(I32, (2 * kb, LANES), 1)
        ones_bd = jnp.where((r_i < kb) == (c_i < HEAD_DIM), 1.0, 0.0).astype(BF16)

        def build(c, carry):
            rows = pl.ds(pl.multiple_of(c * kb, kb), kb)
            kibd_scr[c] = _pair_blockdiag(ki_ref[0, rows, :], True)
            for g in range(N_KV_HEADS):
                t = g // 2
                kbd_scr[g, c] = _pair_blockdiag(k_ref[0, 0, rows, t * LANES:(t + 1) * LANES], g % 2 == 0)
                vbd_scr[g, c, :, 0:LANES] = _pair_blockdiag(v_ref[0, 0, rows, t * LANES:(t + 1) * LANES],
                                                            g % 2 == 0)
                vbd_scr[g, c, :, LANES:2 * LANES] = ones_bd
            return carry

        lax.fori_loop(0, nkc_all, build, 0)

    w_t = kiw_ref[0].T

    row0 = pos0 + j * tq
    rows = row0 + lax.broadcasted_iota(I32, (1, tq), 1)
    lim = jnp.minimum(((rows >> 6) + 1) << 6, l_valid)
    last = row0 + tq - 1
    lim_max = jnp.minimum(((last >> 6) + 1) << 6, l_valid)
    nkc = (lim_max + kb - 1) >> 8

    def phase_a(c, carry):
        kbd = kibd_scr[c]
        acc = jnp.zeros((kb, tq), F32)
        for t in range(npair):
            d2 = lax.dot_general(kbd, qi_ref[0, :, t * LANES:(t + 1) * LANES], nt,
                                 preferred_element_type=F32)
            r = IDX_DIM + 2 * t
            acc = acc + w_t[r:r + 1, :] * jnp.maximum(d2[0:kb], 0.0)
            acc = acc + w_t[r + 1:r + 2, :] * jnp.maximum(d2[kb:2 * kb], 0.0)
        col = c * kb + lax.broadcasted_iota(I32, (kb, tq), 0)
        sc = jnp.where(col < lim, acc, -jnp.inf)
        bits = lax.bitcast_convert_type(sc, I32)
        key = bits ^ ((bits >> 31) & 0x7FFFFFFF)
        key_scr[c] = key
        hi_scr[c] = (key >> 16).astype(I16)
        lo_scr[c] = ((key & 0xFFFF) - 32768).astype(I16)
        return carry

    lax.fori_loop(0, nkc, phase_a, 0)

    def count_ge(cand):
        def body(c, cnt):
            m = jnp.where(key_scr[c] >= cand, 1.0, 0.0)
            parts = [m[8 * i:8 * (i + 1)] for i in range(kb // 8)]
            while len(parts) > 1:
                parts = [parts[i] + parts[i + 1] for i in range(0, len(parts), 2)]
            return cnt + parts[0]
        cnt = lax.fori_loop(0, nkc, body, jnp.zeros((8, tq), F32))
        return jnp.sum(cnt, axis=0, keepdims=True)

    def count16(ref, pred):
        def body(c, cnt):
            m = jnp.where(pred(ref[c]), jnp.int16(1), jnp.int16(0))
            parts = [m[16 * i:16 * (i + 1)] for i in range(kb // 16)]
            while len(parts) > 1:
                parts = [parts[i] + parts[i + 1] for i in range(0, len(parts), 2)]
            return cnt + parts[0]
        cnt = lax.fori_loop(0, nkc, body, jnp.zeros((16, tq), I16))
        return jnp.sum(cnt.astype(I32), axis=0, keepdims=True)

    def radix16(ref, need):
        t0 = jnp.where(count16(ref, lambda h: h >= jnp.int16(0)) >= need, 0, -32768).astype(I32)

        def bit_body(i, t):
            cand = t | lax.shift_left(jnp.int32(1), 14 - i)
            c16 = cand.astype(I16)
            return jnp.where(count16(ref, lambda h: h >= c16) >= need, cand, t)

        return lax.fori_loop(0, 15, bit_body, t0)

    t_hi = radix16(hi_scr, topk)
    h16 = t_hi.astype(I16)
    need_lo = topk - count16(hi_scr, lambda h: h > h16)

    def keep_bucket(c, carry):
        lo_scr[c] = jnp.where(hi_scr[c] == h16, lo_scr[c], jnp.int16(-32768))
        return carry

    lax.fori_loop(0, nkc, keep_bucket, 0)
    t_lo = radix16(lo_scr, need_lo)
    thr = (t_hi << 16) | ((t_lo + 32768) & 0xFFFF)
    kf = float(topk)
    few = lim <= topk
    thr = jnp.where(few, KEY_MIN_FINITE, thr)

    tie = jnp.logical_and(count_ge(thr) > kf, jnp.logical_not(few))
    any_tie = jnp.max(jnp.where(tie, 1.0, 0.0)) > 0.0

    @pl.when(any_tie)
    def _():
        need = kf - count_ge(thr + 1)
        r_i = lax.broadcasted_iota(I32, (kb, kb), 0)
        c_i = lax.broadcasted_iota(I32, (kb, kb), 1)
        tri = jnp.where(c_i < r_i, 1.0, 0.0).astype(BF16)

        def body(c, run):
            kc = key_scr[c]
            eqf = jnp.where(kc == thr, 1.0, 0.0)
            rank = run + jnp.dot(tri, eqf.astype(BF16), preferred_element_type=F32)
            drop = jnp.logical_and(jnp.logical_and(tie, kc == thr), rank >= need)
            key_scr[c] = jnp.where(drop, KEY_NEG_INF, kc)
            return run + jnp.sum(eqf, axis=0, keepdims=True)

        lax.fori_loop(0, nkc, body, jnp.zeros((1, tq), F32))

    nl = kb // LANES
    m_scr[...] = jnp.full(m_scr.shape, MASK_BIAS, F32)
    acc_scr[...] = jnp.zeros(acc_scr.shape, F32)

    def sweep1(c, carry):
        bias = jnp.where(key_scr[c] >= thr, 0.0, MASK_BIAS).T
        for g in range(N_KV_HEADS):
            s2 = lax.dot_general(q_ref[0, :, g * LANES:(g + 1) * LANES], kbd_scr[g, c], nt,
                                 preferred_element_type=F32)
            for r in range(GROUP):
                h = GROUP * g + r
                s = s2[:, r * kb:(r + 1) * kb] + bias
                s_scr[h, c] = s
                mx = m_scr[h]
                for i in range(nl):
                    mx = jnp.maximum(mx, s[:, i * LANES:(i + 1) * LANES])
                m_scr[h] = mx
        return carry

    lax.fori_loop(0, nkc, sweep1, 0)
    for h in range(N_HEADS):
        m_scr[h] = jnp.broadcast_to(jnp.max(m_scr[h], axis=1, keepdims=True), (tq, LANES))

    def sweep2(c, carry):
        for g in range(N_KV_HEADS):
            p2 = []
            for r in range(GROUP):
                h = GROUP * g + r
                s = s_scr[h, c]
                mb = m_scr[h]
                p2 += [jnp.exp2(s[:, i * LANES:(i + 1) * LANES] - mb) for i in range(nl)]
            pb = jnp.concatenate(p2, axis=1).astype(BF16)
            acc_scr[g] += jnp.dot(pb, vbd_scr[g, c], preferred_element_type=F32)
        return carry

    lax.fori_loop(0, nkc, sweep2, 0)
    for g in range(N_KV_HEADS):
        o_ref[0, :, g * LANES:(g + 1) * LANES] = (acc_scr[g, :, 0:LANES]
                                                  / acc_scr[g, :, LANES:2 * LANES]).astype(BF16)


def _attention(q, qi, kiw, k_all, v_all, ki_all, *, layer, tq, pos0, l_valid, topk):
    b, s, _ = q.shape
    l_pad = k_all.shape[2]
    nkc_max = l_pad // KEY_BLOCK
    qspec = lambda w: pl.BlockSpec((1, tq, w), lambda bi, ji: (bi, ji, 0))
    kspec = lambda w: pl.BlockSpec((1, l_pad, w), lambda bi, ji: (bi, 0, 0))
    lspec = lambda w: pl.BlockSpec((1, 1, l_pad, w), lambda bi, ji: (layer, bi, 0, 0))
    return pl.pallas_call(
        functools.partial(_attn_kernel, tq=tq, pos0=pos0, l_valid=l_valid, topk=topk),
        grid=(b, s // tq),
        in_specs=[qspec(D_Q), qspec(D_QI), qspec(LANES), lspec(D_KV), lspec(D_KV), kspec(LANES)],
        out_specs=qspec(D_Q),
        out_shape=jax.ShapeDtypeStruct((b, s, D_Q), BF16),
        scratch_shapes=[pltpu.VMEM((N_KV_HEADS, nkc_max, 2 * KEY_BLOCK, LANES), BF16),
                        pltpu.VMEM((N_KV_HEADS, nkc_max, 2 * KEY_BLOCK, 2 * LANES), BF16),
                        pltpu.VMEM((nkc_max, 2 * KEY_BLOCK, LANES), BF16),
                        pltpu.VMEM((nkc_max, KEY_BLOCK, tq), I32),
                        pltpu.VMEM((nkc_max, KEY_BLOCK, tq), I16),
                        pltpu.VMEM((nkc_max, KEY_BLOCK, tq), I16),
                        pltpu.VMEM((N_HEADS, nkc_max, tq, KEY_BLOCK), F32),
                        pltpu.VMEM((N_HEADS, tq, LANES), F32),
                        pltpu.VMEM((N_KV_HEADS, tq, 2 * LANES), F32)],
        compiler_params=_cparams(2), name="attention",
    )(q, qi, kiw, k_all, v_all, ki_all)


def _split3(a):
    hi = a.astype(BF16)
    r = a - hi.astype(F32)
    mid = r.astype(BF16)
    lo = (r - mid.astype(F32)).astype(BF16)
    return hi, mid, lo


def _merge_kernel(*refs, with_router):
    if with_router:
        (x_ref, ya_ref, ao_ref, sg_ref, mod_ref, wc_ref, wa_ref, wo_ref, g2_ref, rw_ref, rb_ref,
         x1_ref, h2_ref, lg_ref) = refs
    else:
        (x_ref, ya_ref, ao_ref, sg_ref, mod_ref, wc_ref, wa_ref, wo_ref, g2_ref,
         x1_ref, h2_ref) = refs
    a = jnp.dot(ya_ref[0], wc_ref[...], preferred_element_type=F32)
    b = jnp.dot(ao_ref[0], wa_ref[...], preferred_element_type=F32)
    merged = sg_ref[0, :, 0:D_MODEL].astype(F32) * a + sg_ref[0, :, D_MODEL:2 * D_MODEL].astype(F32) * b
    mix = jnp.dot(merged.astype(BF16), wo_ref[...], preferred_element_type=F32)
    x1 = x_ref[0] + mod_ref[0, 2] * mix
    x1_ref[0] = x1
    h2 = _rmsnorm(x1, g2_ref[...]) * (1.0 + mod_ref[0, 4]) + mod_ref[0, 3]
    h2_ref[0] = h2.astype(h2_ref.dtype)
    if with_router:
        hs = _split3(h2)
        acc = jnp.zeros((h2.shape[0], LANES), F32)
        for i, jj in ((0, 0), (0, 1), (1, 0)):
            acc = acc + jnp.dot(hs[i], rw_ref[jj], preferred_element_type=F32)
        lane = lax.broadcasted_iota(I32, acc.shape, 1)
        lg = jnp.where(lane < N_EXPERTS, acc + rb_ref[...], -jnp.inf)
        m1 = jnp.max(lg, axis=1, keepdims=True)
        i1 = jnp.min(jnp.where(lg == m1, lane, LANES), axis=1, keepdims=True)
        rest = jnp.where(lane == i1, -jnp.inf, lg)
        m2 = jnp.max(rest, axis=1, keepdims=True)
        i2 = jnp.min(jnp.where(rest == m2, lane, LANES), axis=1, keepdims=True)
        e2 = jnp.exp(m2 - m1)
        den = 1.0 + e2
        lg_ref[0] = jnp.where(lane == 0, i1.astype(F32),
                              jnp.where(lane == 1, i2.astype(F32),
                                        jnp.where(lane == 2, 1.0 / den,
                                                  jnp.where(lane == 3, e2 / den, 0.0))))


def _merge(x, ya, ao, sg, mod, wc, wa, wo, g2, router=None, *, tm):
    b, s, d = x.shape
    r = mod.shape[2]
    rspec = lambda w: pl.BlockSpec((1, tm, w), lambda bi, ti: (bi, ti, 0))
    mspec = (pl.BlockSpec((1, 6, 1, d), lambda bi, ti: (bi, 0, 0, 0)) if r == 1 else
             pl.BlockSpec((1, 6, tm, d), lambda bi, ti: (bi, 0, ti, 0)))
    in_specs = [rspec(d), rspec(D_CONV), rspec(D_Q), rspec(2 * d), mspec,
                _resident((D_CONV, d)), _resident((D_Q, d)), _resident((d, d)), _resident((1, d))]
    args = [x, ya, ao, sg, mod, wc, wa, wo, g2]
    out_specs = [rspec(d), rspec(d)]
    out_shape = [jax.ShapeDtypeStruct((b, s, d), F32),
                 jax.ShapeDtypeStruct((b, s, d), BF16 if router is None else F32)]
    if router is not None:
        in_specs += [_resident((3, d, LANES)), _resident((1, LANES))]
        args += list(router)
        out_specs.append(rspec(LANES))
        out_shape.append(jax.ShapeDtypeStruct((b, s, LANES), F32))
    return pl.pallas_call(
        functools.partial(_merge_kernel, with_router=router is not None),
        grid=(b, s // tm),
        in_specs=in_specs, out_specs=out_specs, out_shape=out_shape,
        compiler_params=_cparams(2), name="merge",
    )(*args)


def _ffn_kernel(h2_ref, x1_ref, mod_ref, w1_ref, w3_ref, w2_ref, o_ref, acc_scr, *, nc):
    h = h2_ref[0]
    acc_scr[...] = _swiglu_chunk(h, w1_ref[0], w3_ref[0], w2_ref[0])

    def body(c, carry):
        acc_scr[...] += _swiglu_chunk(h, w1_ref[c], w3_ref[c], w2_ref[c])
        return carry

    lax.fori_loop(1, nc, body, 0)
    o_ref[0] = x1_ref[0] + mod_ref[0, 5] * acc_scr[...]


def _ffn(h2, x1, mod, w1, w3, w2, *, tm):
    b, s, d = x1.shape
    nc = w1.shape[0]
    r = mod.shape[2]
    rspec = lambda w: pl.BlockSpec((1, tm, w), lambda bi, ti: (bi, ti, 0))
    mspec = (pl.BlockSpec((1, 6, 1, d), lambda bi, ti: (bi, 0, 0, 0)) if r == 1 else
             pl.BlockSpec((1, 6, tm, d), lambda bi, ti: (bi, 0, ti, 0)))
    return pl.pallas_call(
        functools.partial(_ffn_kernel, nc=nc),
        grid=(b, s // tm),
        in_specs=[rspec(d), rspec(d), mspec, _resident(w1.shape), _resident(w3.shape), _resident(w2.shape)],
        out_specs=rspec(d),
        out_shape=jax.ShapeDtypeStruct((b, s, d), F32),
        scratch_shapes=[pltpu.VMEM((tm, d), F32)],
        compiler_params=_cparams(2), name="ffn",
    )(h2, x1, mod, w1, w3, w2)


def _swiglu_chunk(xb, w1, w3, w2):
    a = jnp.dot(xb, w1, preferred_element_type=F32)
    hm = (a * _sigmoid(a) * jnp.dot(xb, w3, preferred_element_type=F32)).astype(BF16)
    return jnp.dot(hm, w2, preferred_element_type=F32)


def _moe_dense_kernel(h2_ref, x1_ref, rt_ref, mod_ref, w1_ref, w3_ref, w2_ref, fg_ref, o_ref,
                      eacc_scr, acc_scr, *, final_norm):
    e = pl.program_id(2)
    c = pl.program_id(3)
    ne = pl.num_programs(2)
    nc = pl.num_programs(3)

    @pl.when(jnp.logical_and(e == 0, c == 0))
    def _():
        acc_scr[...] = jnp.zeros(acc_scr.shape, F32)

    @pl.when(c == 0)
    def _():
        eacc_scr[...] = jnp.zeros(eacc_scr.shape, F32)

    eacc_scr[...] += _swiglu_chunk(h2_ref[0].astype(BF16), w1_ref[0], w3_ref[0], w2_ref[0])

    @pl.when(c == nc - 1)
    def _():
        rt = rt_ref[0]
        ef = e.astype(F32)
        gate = (jnp.where(rt[:, 0:1] == ef, rt[:, 2:3], 0.0)
                + jnp.where(rt[:, 1:2] == ef, rt[:, 3:4], 0.0))
        acc_scr[...] += gate * eacc_scr[...]

    @pl.when(jnp.logical_and(e == ne - 1, c == nc - 1))
    def _():
        x2 = x1_ref[0] + mod_ref[0, 5] * acc_scr[...]
        o_ref[0] = _rmsnorm(x2, fg_ref[...]) if final_norm else x2


def _moe_dense(h2, x1, route, mod, w1, w3, w2, fg, *, tm, final_norm):
    b, s, d = x1.shape
    ne, _, f = w1.shape
    fc = MOE_CHUNK
    r = mod.shape[2]
    rspec = lambda w: pl.BlockSpec((1, tm, w), lambda bi, ti, ei, ci: (bi, ti, 0))
    mspec = (pl.BlockSpec((1, 6, 1, d), lambda bi, ti, ei, ci: (bi, 0, 0, 0)) if r == 1 else
             pl.BlockSpec((1, 6, tm, d), lambda bi, ti, ei, ci: (bi, 0, ti, 0)))
    wcol = pl.BlockSpec((1, d, fc), lambda bi, ti, ei, ci: (ei, 0, ci))
    return pl.pallas_call(
        functools.partial(_moe_dense_kernel, final_norm=final_norm),
        grid=(b, s // tm, ne, f // fc),
        in_specs=[rspec(d), rspec(d), rspec(LANES), mspec, wcol, wcol,
                  pl.BlockSpec((1, fc, d), lambda bi, ti, ei, ci: (ei, ci, 0)),
                  _resident((1, d))],
        out_specs=rspec(d),
        out_shape=jax.ShapeDtypeStruct((b, s, d), F32),
        scratch_shapes=[pltpu.VMEM((tm, d), F32), pltpu.VMEM((tm, d), F32)],
        compiler_params=_cparams(4), name="moe_dense",
    )(h2, x1, route, mod, w1, w3, w2, fg)


def _route_tables(route, tmr):
    n = route.shape[0]
    e1 = route[:, 0].astype(I32)
    e2 = route[:, 1].astype(I32)
    ar = jnp.arange(N_EXPERTS, dtype=I32)
    oh1 = e1[:, None] == ar
    oh2 = e2[:, None] == ar
    oh = oh1.astype(I32) + oh2.astype(I32)
    cum = jnp.cumsum(oh, axis=0)
    excl = cum - oh
    cnt = cum[-1]
    tiles_e = (cnt + tmr - 1) // tmr
    tile_end = jnp.cumsum(tiles_e)
    tile_start = tile_end - tiles_e
    row_off = tile_start * tmr
    d1 = row_off[e1] + jnp.sum(jnp.where(oh1, excl, 0), axis=1)
    d2 = row_off[e2] + jnp.sum(jnp.where(oh2, excl, 0), axis=1)
    nt = -(-2 * n // tmr) + N_EXPERTS
    tok = jnp.arange(n, dtype=I32)
    pair = jnp.zeros((nt * tmr,), I32).at[jnp.concatenate([d1, d2])].set(
        jnp.concatenate([2 * tok, 2 * tok + 1]), unique_indices=True)
    src = pair >> 1
    dst = (pair & 1) * n + src
    tile = jnp.arange(nt, dtype=I32)
    te = jnp.minimum(jnp.searchsorted(tile_end, tile, side="right"), N_EXPERTS - 1).astype(I32)
    nv = jnp.where(tile < tile_end[-1], jnp.clip(cnt[te] - (tile - tile_start[te]) * tmr, 0, tmr), 0)
    return (src.reshape(nt, 1, tmr), dst.reshape(nt, 1, tmr), te, nv.astype(I32),
            tile_end[-1:].astype(I32))


def _moe_routed_kernel(te_ref, nv_ref, nu_ref, src0_ref, srcn_ref, dstp_ref, h2_hbm, w1_ref, w3_ref, w2_ref,
                       out_hbm, xbuf, xb_scr, ybuf, gsem, ssem, *, tmr, nc, gs):
    i = pl.program_id(0)
    c = pl.program_id(1)
    nu = nu_ref[0]
    slot = i % 2
    other = 1 - slot
    rpc = tmr // nc
    rpg = tmr // gs

    def gather(src_ref, cc, k, sl):
        return pltpu.make_async_copy(h2_hbm.at[pl.ds(src_ref[0, 0, cc * rpg + k], 1)],
                                     xbuf.at[sl, cc, pl.ds(k, 1)], gsem.at[sl])

    def scatter(cc, k, sl):
        return pltpu.make_async_copy(ybuf.at[sl, cc, pl.ds(k, 1)],
                                     out_hbm.at[pl.ds(dstp_ref[0, 0, cc * rpc + k], 1)], ssem.at[sl])

    @pl.when(jnp.logical_and(i == 0, c == 0))
    def _():
        def body(cc, carry):
            for k in range(rpg):
                gather(src0_ref, cc, k, 0).start()
            return carry
        lax.fori_loop(0, gs, body, 0)

    @pl.when(jnp.logical_and(c == 0, i < nu))
    def _():
        pltpu.make_async_copy(xbuf.at[slot], xbuf.at[slot], gsem.at[slot]).wait()
        xb_scr[...] = xbuf[slot].reshape(tmr, xb_scr.shape[1]).astype(BF16)

    last = nv_ref.shape[0] - 1
    nv_drain = nv_ref[jnp.clip(i - 2, 0, last)]
    push = jnp.logical_and(i >= 1, i - 1 < nu)
    nv_push = nv_ref[jnp.clip(i - 1, 0, last)]
    fast = jnp.logical_and(i >= 1, jnp.logical_and(i + 1 < nu, nv_push == tmr))
    slow = jnp.logical_not(fast)
    drain = jnp.logical_and(c == 0, jnp.logical_and(i >= 2, i - 2 < nu))

    @pl.when(jnp.logical_and(drain, nv_drain == tmr))
    def _():
        pltpu.make_async_copy(ybuf.at[slot], ybuf.at[slot], ssem.at[slot]).wait()

    @pl.when(jnp.logical_and(drain, nv_drain < tmr))
    def _():
        def body(r, carry):
            scatter(0, 0, slot).wait()
            return carry
        lax.fori_loop(0, nv_drain, body, 0)

    @pl.when(fast)
    def _():
        xb = xb_scr[...]
        a = jnp.dot(xb, w1_ref[0], preferred_element_type=F32)
        b = jnp.dot(xb, w3_ref[0], preferred_element_type=F32)
        for k in range(rpg):
            gather(srcn_ref, c, k, other).start()
        for k in range(rpc):
            scatter(c, k, other).start()
        hm = (a * _sigmoid(a) * b).astype(BF16)
        y = jnp.dot(hm, w2_ref[0], preferred_element_type=F32).reshape(nc, rpc, xb_scr.shape[1])
        ybuf[slot] = jnp.where(c == 0, y, ybuf[slot] + y)

    @pl.when(jnp.logical_and(slow, jnp.logical_and(i + 1 < nu, c < gs)))
    def _():
        for k in range(rpg):
            gather(srcn_ref, c, k, other).start()

    @pl.when(jnp.logical_and(slow, jnp.logical_and(push, nv_push == tmr)))
    def _():
        for k in range(rpc):
            scatter(c, k, other).start()

    @pl.when(jnp.logical_and(slow, jnp.logical_and(push, nv_push < tmr)))
    def _():
        def body(k, carry):
            scatter(c, k, other).start()
            return carry
        lax.fori_loop(0, jnp.clip(nv_push - c * rpc, 0, rpc), body, 0)

    @pl.when(jnp.logical_and(slow, i < nu))
    def _():
        y = _swiglu_chunk(xb_scr[...], w1_ref[0], w3_ref[0], w2_ref[0]).reshape(nc, rpc, xb_scr.shape[1])

        @pl.when(c == 0)
        def _():
            ybuf[slot] = y

        @pl.when(c > 0)
        def _():
            ybuf[slot] += y


def _moe_routed(h2, route, w1, w3, w2, *, tmr):
    n, d = h2.shape
    f = w1.shape[2]
    fc = MOE_CHUNK
    nc = f // fc
    gs = MOE_GATHER_STEPS
    src, dst, te, nv, nu = _route_tables(route, tmr)
    nt = src.shape[0]
    last = nt - 1
    smem = lambda imap: pl.BlockSpec((1, 1, tmr), imap, memory_space=pltpu.SMEM)
    chunk = lambda i, c, nu_r: jnp.where(i < nu_r[0], c, nc - 1)
    wcol = pl.BlockSpec((1, d, fc), lambda i, c, te_r, nv_r, nu_r: (te_r[jnp.minimum(i, last)], 0, chunk(i, c, nu_r)))
    wrow = pl.BlockSpec((1, fc, d), lambda i, c, te_r, nv_r, nu_r: (te_r[jnp.minimum(i, last)], chunk(i, c, nu_r), 0))
    return pl.pallas_call(
        functools.partial(_moe_routed_kernel, tmr=tmr, nc=nc, gs=gs),
        grid_spec=pltpu.PrefetchScalarGridSpec(
            num_scalar_prefetch=3,
            grid=(nt + 2, nc),
            in_specs=[smem(lambda i, c, te_r, nv_r, nu_r: (0, 0, 0)),
                      smem(lambda i, c, te_r, nv_r, nu_r: (jnp.minimum(i + 1, last), 0, 0)),
                      smem(lambda i, c, te_r, nv_r, nu_r: (jnp.clip(i - 1, 0, last), 0, 0)),
                      pl.BlockSpec(memory_space=pl.ANY), wcol, wcol, wrow],
            out_specs=pl.BlockSpec(memory_space=pl.ANY),
            scratch_shapes=[pltpu.VMEM((2, gs, tmr // gs, d), F32), pltpu.VMEM((tmr, d), BF16),
                            pltpu.VMEM((2, nc, tmr // nc, d), F32),
                            pltpu.SemaphoreType.DMA((2,)), pltpu.SemaphoreType.DMA((2,))]),
        out_shape=jax.ShapeDtypeStruct((2 * n, d), F32),
        compiler_params=_cparams(2), name="moe_routed",
    )(te, nv, nu, src, src, dst, h2, w1, w3, w2)


def _moe_combine_kernel(x1_ref, ya_ref, yb_ref, rt_ref, mod_ref, fg_ref, o_ref, *, final_norm):
    rt = rt_ref[0]
    f = rt[:, 2:3] * ya_ref[...] + rt[:, 3:4] * yb_ref[...]
    x2 = x1_ref[0] + mod_ref[0, 5] * f
    o_ref[0] = _rmsnorm(x2, fg_ref[...]) if final_norm else x2


def _moe_combine(x1, y2, route, mod, fg, *, tm, final_norm):
    b, s, d = x1.shape
    nt = s // tm
    rspec = lambda w: pl.BlockSpec((1, tm, w), lambda bi, ti: (bi, ti, 0))
    return pl.pallas_call(
        functools.partial(_moe_combine_kernel, final_norm=final_norm),
        grid=(b, nt),
        in_specs=[rspec(d),
                  pl.BlockSpec((tm, d), lambda bi, ti: (bi * nt + ti, 0)),
                  pl.BlockSpec((tm, d), lambda bi, ti: (b * nt + bi * nt + ti, 0)),
                  rspec(LANES),
                  pl.BlockSpec((1, 6, 1, d), lambda bi, ti: (bi, 0, 0, 0)), _resident((1, d))],
        out_specs=rspec(d),
        out_shape=jax.ShapeDtypeStruct((b, s, d), F32),
        compiler_params=_cparams(2), name="moe_combine",
    )(x1, y2, y2, route, mod, fg)


def _pack_w_in(w):
    cut = C_KIW + IDX_DIM + N_IDX_HEADS
    pad = jnp.zeros((w.shape[0], C_GATE - cut), w.dtype)
    return jnp.concatenate([w[:, :cut], pad, w[:, cut:]], axis=1).astype(BF16)


def _chunk_cols(w, fc):
    *lead, d, f = w.shape
    return jnp.moveaxis(w.astype(BF16).reshape(*lead, d, f // fc, fc), -2, -3)


def _chunk_w2(w2, fc):
    *lead, f, d = w2.shape
    return w2.reshape(*lead, f // fc, fc, d).astype(BF16)


def _router_terms(rw, rb):
    d = rw.shape[0]
    rwp = jnp.zeros((d, LANES), F32).at[:, :N_EXPERTS].set(rw)
    hi, mid, lo = _split3(rwp)
    rbp = jnp.zeros((1, LANES), F32).at[0, :N_EXPERTS].set(rb)
    return jnp.stack([hi, mid, lo]), rbp


def _trunk(x, mod_all, caches, p, *, tm, tq):
    b, t, d = x.shape
    depth = mod_all.shape[0]
    flat = caches is not None
    stacks, tails = None, []
    for l in range(depth):
        mod = mod_all[l].reshape(b, 6, 1, d)
        if caches is None:
            conv_state = jnp.zeros((b, CONV_W - 1, D_CONV), F32)
        else:
            conv_state = caches[3][l]
        ya, q, k_st, v_st, qi, kiw, ki_st, sg, tail = _inproj(
            x, mod, p["norm1_g"][l], p["w_in"][l], p["conv_w"][l], conv_state, tm,
            layer=l, depth=depth, stacks=stacks)
        stacks = (k_st, v_st, ki_st)
        if caches is None:
            ao = _attention(q, qi, kiw, k_st, v_st, kiw, layer=l, tq=tq, pos0=0, l_valid=t,
                            topk=min(TOPK_MAX, t // 4))
        else:
            past = caches[0][l].shape[1]
            l_valid = past + t
            l_pad = -(-l_valid // KEY_BLOCK) * KEY_BLOCK
            cat = lambda old, new: jnp.concatenate(
                [old.reshape(b, past, -1), new, jnp.zeros((b, l_pad - l_valid, new.shape[-1]), F32)], axis=1)
            k_all, v_all = cat(caches[0][l], k_st[l])[None], cat(caches[1][l], v_st[l])[None]
            ki_all = jnp.pad(cat(caches[2][l], ki_st[l]), ((0, 0), (0, 0), (0, LANES - IDX_DIM)))
            padq = lambda a: jnp.pad(a, ((0, 0), (0, tq - t), (0, 0)))
            ao = _attention(padq(q), padq(qi), padq(kiw), k_all, v_all, ki_all, layer=0, tq=tq, pos0=past,
                            l_valid=l_valid, topk=min(TOPK_MAX, l_valid // 4))[:, :t]
        is_moe = l % 2 == 1
        router = (p["router_w"][l // 2], p["router_b"][l // 2]) if is_moe else None
        if flat:
            n = b * t
            fl = lambda a: a.reshape(1, n, a.shape[-1])
            modf = jnp.repeat(mod_all[l].reshape(b, 6, d), t, axis=0).transpose(1, 0, 2)[None]
            xs, yas, aos, sgs, tmf = fl(x), fl(ya), fl(ao), fl(sg), n
        else:
            modf, xs, yas, aos, sgs, tmf = mod, x, ya, ao, sg, tm
        outs = _merge(xs, yas, aos, sgs, modf, p["w_conv_out"][l], p["w_attn_out"][l], p["w_o"][l],
                      p["norm2_g"][l], router, tm=tmf)
        if is_moe and flat:
            x1, h2, route = outs
            x = _moe_dense(h2, x1, route, modf, p["moe_w1"][l // 2], p["moe_w3"][l // 2],
                           p["moe_w2"][l // 2], p["final_g"], tm=xs.shape[1], final_norm=l == depth - 1)
        elif is_moe:
            x1, h2, route = outs
            y2 = _moe_routed(h2.reshape(b * t, d), route.reshape(b * t, LANES), p["moe_w1"][l // 2],
                             p["moe_w3"][l // 2], p["moe_w2"][l // 2], tmr=MOE_ROWS)
            x = _moe_combine(x1, y2, route, mod, p["final_g"], tm=tm,
                             final_norm=l == depth - 1)
        else:
            x1, h2 = outs
            x = _ffn(h2, x1, modf, p["ffn_w1"][l // 2], p["ffn_w3"][l // 2], p["ffn_w2"][l // 2],
                     tm=tmf if flat else min(2 * tm, t))
        x = x.reshape(b, t, d)
        tails.append(tail)
    k_st, v_st, ki_st = stacks
    heads = (depth, b, t, N_KV_HEADS, HEAD_DIM)
    return x, k_st.reshape(heads), v_st.reshape(heads), ki_st, jnp.stack(tails)


def kernel(x_prompt, x_sample, c_prompt, c_sample, cache_k, cache_v, cache_idx_k, state_conv, w_ada, b_ada, norm1_g, w_in, conv_w, w_conv_out, w_attn_out, w_o, norm2_g, ffn_w1, ffn_w3, ffn_w2, router_w, router_b, moe_w1, moe_w3, moe_w2, final_g):
    depth = w_in.shape[0]
    bp = x_prompt.shape[0]
    rterms = [_router_terms(router_w[i], router_b[i]) for i in range(router_w.shape[0])]
    p = {
        "norm1_g": norm1_g.reshape(depth, 1, -1),
        "norm2_g": norm2_g.reshape(depth, 1, -1),
        "final_g": final_g.reshape(1, -1),
        "w_in": jnp.stack([_pack_w_in(w_in[l]) for l in range(depth)]),
        "conv_w": conv_w,
        "w_conv_out": w_conv_out.astype(BF16),
        "w_attn_out": w_attn_out.astype(BF16),
        "w_o": w_o.astype(BF16),
        "ffn_w1": _chunk_cols(ffn_w1, FFN_CHUNK),
        "ffn_w3": _chunk_cols(ffn_w3, FFN_CHUNK),
        "ffn_w2": _chunk_w2(ffn_w2, FFN_CHUNK),
        "moe_w1": moe_w1.astype(BF16),
        "moe_w3": moe_w3.astype(BF16),
        "moe_w2": moe_w2.astype(BF16),
        "router_w": [r[0] for r in rterms],
        "router_b": [r[1] for r in rterms],
    }
    mod_all = _adaln(jnp.concatenate([c_prompt, c_sample], axis=0), w_ada.astype(BF16), b_ada)
    seq = x_prompt.shape[1]
    tm = min(512, seq)
    tq = min(256, seq)
    yp, kp, vp, kip, cp = _trunk(x_prompt, mod_all[:, :bp], None, p, tm=tm, tq=tq)
    ts = x_sample.shape[1]
    ys, ksm, vsm, kism, csm = _trunk(x_sample, mod_all[:, bp:], (cache_k, cache_v, cache_idx_k, state_conv), p,
                                     tm=ts, tq=-(-ts // LANES) * LANES)
    return (yp, ys, kp, vp, kip, cp, ksm, vsm, kism, csm)
```

```python
import functools

import jax
import jax.numpy as jnp
from jax import lax
from jax.experimental import pallas as pl
from jax.experimental.pallas import tpu as pltpu

F32 = jnp.float32
BF16 = jnp.bfloat16
I32 = jnp.int32
I16 = jnp.int16

D_MODEL = 1024
D_CONV = 512
CONV_W = 3
N_HEADS = 8
N_KV_HEADS = 4
HEAD_DIM = 64
GROUP = N_HEADS // N_KV_HEADS
N_IDX_HEADS = 16
IDX_DIM = 64
TOPK_MAX = 256
CHUNK = 64
D_FF = 2816
N_EXPERTS = 8
D_FF_EXPERT = 3584
EPS = 1e-6

D_Q = N_HEADS * HEAD_DIM
D_KV = N_KV_HEADS * HEAD_DIM
D_QI = N_IDX_HEADS * IDX_DIM
LANES = 128
C_CONV = 0
C_Q = 3 * D_CONV
C_KV = C_Q + D_Q
C_QI = C_KV + 2 * D_KV
C_KIW = C_QI + D_QI
C_GATE = C_KIW + LANES
D_PACK = C_GATE + 2 * D_MODEL
WI_SCALE = (IDX_DIM ** -0.5) * (N_IDX_HEADS ** -0.5)
Q_SCALE = HEAD_DIM ** -0.5 * 1.4426950408889634

KEY_BLOCK = 256
FFN_CHUNK = 256
MOE_CHUNK = 512
MOE_ROWS = (D_FF_EXPERT // MOE_CHUNK) * 128
MOE_GATHER_STEPS = D_FF_EXPERT // MOE_CHUNK
V7X_VMEM_LIMIT = 56 * 1024 * 1024

KEY_NEG_INF = -2139095041
KEY_MIN_FINITE = KEY_NEG_INF + 1
MASK_BIAS = -1e30


def _sigmoid(x):
    return 1.0 / (1.0 + jnp.exp(-x))


def _rmsnorm(x, g):
    ms = jnp.mean(x * x, axis=-1, keepdims=True)
    return x * lax.rsqrt(ms + EPS) * g


def _cparams(n_axes):
    return pltpu.CompilerParams(dimension_semantics=("arbitrary",) * n_axes,
                                vmem_limit_bytes=V7X_VMEM_LIMIT)


def _resident(shape):
    nd = len(shape)
    return pl.BlockSpec(shape, lambda *_: (0,) * nd, pipeline_mode=pl.Buffered(1))


def _adaln_kernel(c_ref, w_ref, b_ref, o_ref):
    c = c_ref[...]
    cond = (c * _sigmoid(c)).astype(BF16)
    o_ref[0] = jnp.dot(cond, w_ref[0], preferred_element_type=F32) + b_ref[0]


def _adaln(c_all, w_ada, b_ada):
    depth, d, n6 = w_ada.shape
    bc = c_all.shape[0]
    nblk = n6 // D_MODEL
    return pl.pallas_call(
        _adaln_kernel,
        grid=(depth, nblk),
        in_specs=[pl.BlockSpec((bc, d), lambda l, n: (0, 0)),
                  pl.BlockSpec((1, d, D_MODEL), lambda l, n: (l, 0, n)),
                  pl.BlockSpec((1, 1, D_MODEL), lambda l, n: (l, 0, n))],
        out_specs=pl.BlockSpec((1, bc, D_MODEL), lambda l, n: (l, 0, n)),
        out_shape=jax.ShapeDtypeStruct((depth, bc, n6), F32),
        compiler_params=_cparams(2), name="adaln",
    )(c_all, w_ada, b_ada.reshape(depth, 1, n6))


def _inproj_kernel(*refs, tm, layer):
    if layer == 0:
        (x_ref, mod_ref, g_ref, w_ref, cw_ref, st_ref,
         ya_ref, q_ref, k_ref, v_ref, qi_ref, kiw_ref, ki_ref, sg_ref, tail_ref, state_scr) = refs
    else:
        (x_ref, mod_ref, g_ref, w_ref, cw_ref, st_ref, _, _, _,
         ya_ref, q_ref, k_ref, v_ref, qi_ref, kiw_ref, ki_ref, sg_ref, tail_ref, state_scr) = refs

    def put(ref, val):
        if layer == 0:
            ref[0, 0] = val
            for other_layer in range(1, ref.shape[0]):
                ref[other_layer, 0] = jnp.zeros_like(val)
        else:
            ref[0, 0] = val

    t = pl.program_id(1)

    @pl.when(t == 0)
    def _():
        state_scr[...] = st_ref[0]

    x = x_ref[0]
    h = _rmsnorm(x, g_ref[...]) * (1.0 + mod_ref[0, 1]) + mod_ref[0, 0]
    hb = h.astype(BF16)

    def seg(a, b):
        return jnp.dot(hb, w_ref[:, a:b], preferred_element_type=F32)

    bg = seg(C_CONV, C_CONV + D_CONV)
    u = seg(C_CONV + D_CONV, C_CONV + 2 * D_CONV) * seg(C_CONV + 2 * D_CONV, C_CONV + 3 * D_CONV)
    s0 = state_scr[0:1, :]
    s1 = state_scr[1:2, :]
    ri = lax.broadcasted_iota(I32, u.shape, 0)
    um1 = jnp.where(ri == 0, s1, pltpu.roll(u, 1, 0))
    um2 = jnp.where(ri == 0, s0, jnp.where(ri == 1, s1, pltpu.roll(u, 2, 0)))
    cw = cw_ref[...]
    y = cw[0:1] * um2
    y = y + cw[1:2] * um1
    y = y + cw[2:3] * u
    ya_ref[0] = (bg * y).astype(BF16)
    tail = u[tm - 2:tm, :]
    state_scr[...] = tail
    tail_ref[0] = tail

    q_ref[0] = (seg(C_Q, C_Q + D_Q) * Q_SCALE).astype(BF16)
    put(k_ref, seg(C_KV, C_KV + D_KV))
    put(v_ref, seg(C_KV + D_KV, C_KV + 2 * D_KV))
    half = D_QI // 2
    qi_ref[0, :, 0:half] = seg(C_QI, C_QI + half).astype(BF16)
    qi_ref[0, :, half:D_QI] = seg(C_QI + half, C_QI + D_QI).astype(BF16)
    kw = seg(C_KIW, C_KIW + LANES)
    put(ki_ref, kw[:, 0:IDX_DIM])
    lane = lax.broadcasted_iota(I32, (1, LANES), 1)
    scale = jnp.where(lane < IDX_DIM, 1.0, jnp.where(lane < IDX_DIM + N_IDX_HEADS, WI_SCALE, 0.0))
    kiw_ref[0] = kw * scale

    for i in range(4):
        a = C_GATE + i * 512
        sg_ref[0, :, i * 512:(i + 1) * 512] = _sigmoid(seg(a, a + 512)).astype(BF16)


def _inproj(x, mod, g1, w_pack, conv_w, conv_state, tm, *, layer, depth, stacks):
    b, s, d = x.shape
    nt = s // tm
    row = lambda w, dt: jax.ShapeDtypeStruct((b, s, w), dt)
    stk = lambda w: jax.ShapeDtypeStruct((depth, b, s, w), F32)
    rspec = lambda w: pl.BlockSpec((1, tm, w), lambda bi, ti: (bi, ti, 0))
    if layer == 0:
        sspec = lambda w: pl.BlockSpec((depth, 1, tm, w), lambda bi, ti: (0, bi, ti, 0))
    else:
        sspec = lambda w: pl.BlockSpec((1, 1, tm, w), lambda bi, ti: (layer, bi, ti, 0))
    in_specs = [rspec(d),
                pl.BlockSpec((1, 6, 1, d), lambda bi, ti: (bi, 0, 0, 0)),
                _resident((1, d)),
                _resident((d, D_PACK)),
                _resident((CONV_W, D_CONV)),
                pl.BlockSpec((1, CONV_W - 1, D_CONV), lambda bi, ti: (bi, 0, 0))]
    args = [x, mod, g1, w_pack, conv_w, conv_state]
    aliases = {}
    if layer > 0:
        in_specs += [pl.BlockSpec(memory_space=pl.ANY)] * 3
        aliases = {len(args): 2, len(args) + 1: 3, len(args) + 2: 6}
        args += list(stacks)
    return pl.pallas_call(
        functools.partial(_inproj_kernel, tm=tm, layer=layer),
        grid=(b, nt),
        in_specs=in_specs,
        out_specs=[rspec(D_CONV), rspec(D_Q), sspec(D_KV), sspec(D_KV), rspec(D_QI),
                   rspec(LANES), sspec(IDX_DIM), rspec(2 * D_MODEL),
                   pl.BlockSpec((1, CONV_W - 1, D_CONV), lambda bi, ti: (bi, 0, 0))],
        out_shape=[row(D_CONV, BF16), row(D_Q, BF16), stk(D_KV), stk(D_KV),
                   row(D_QI, BF16), row(LANES, F32), stk(IDX_DIM), row(2 * D_MODEL, BF16),
                   jax.ShapeDtypeStruct((b, CONV_W - 1, D_CONV), F32)],
        scratch_shapes=[pltpu.VMEM((CONV_W - 1, D_CONV), F32)],
        input_output_aliases=aliases,
        compiler_params=_cparams(2), name="inproj",
    )(*args)


def _pair_blockdiag(tile, low_half):
    lane = lax.broadcasted_iota(I32, tile.shape, 1)
    lo = lane < HEAD_DIM
    swapped = pltpu.roll(tile, HEAD_DIM, 1)
    top = jnp.where(lo, tile if low_half else swapped, 0.0)
    bot = jnp.where(lo, 0.0, swapped if low_half else tile)
    return jnp.concatenate([top, bot], axis=0).astype(BF16)


def _attn_kernel(q_ref, qi_ref, kiw_ref, k_ref, v_ref, ki_ref, o_ref,
                 kbd_scr, vbd_scr, kibd_scr, key_scr, hi_scr, lo_scr, s_scr, m_scr, acc_scr,
                 *, tq, pos0, l_valid, topk):
    kb = KEY_BLOCK
    j = pl.program_id(1)
    nkc_all = key_scr.shape[0]
    npair = N_IDX_HEADS // 2
    nt = (((1,), (1,)), ((), ()))

    @pl.when(j == 0)
    def _():
        r_i = lax.broadcasted_iota(I32, (2 * kb, LANES), 0)
        c_i = lax.broadcasted_iota(I32, (2 * kb, LANES), 1)
        ones_bd = jnp.where((r_i < kb) == (c_i < HEAD_DIM), 1.0, 0.0).astype(BF16)

        def build(c, carry):
            rows = pl.ds(pl.multiple_of(c * kb, kb), kb)
            kibd_scr[c] = _pair_blockdiag(ki_ref[0, rows, :], True)
            for g in range(N_KV_HEADS):
                t = g // 2
                kbd_scr[g, c] = _pair_blockdiag(k_ref[0, 0, rows, t * LANES:(t + 1) * LANES], g % 2 == 0)
                vbd_scr[g, c, :, 0:LANES] = _pair_blockdiag(v_ref[0, 0, rows, t * LANES:(t + 1) * LANES],
                                                            g % 2 == 0)
                vbd_scr[g, c, :, LANES:2 * LANES] = ones_bd
            return carry

        lax.fori_loop(0, nkc_all, build, 0)

    w_t = kiw_ref[0].T

    row0 = pos0 + j * tq
    rows = row0 + lax.broadcasted_iota(I32, (1, tq), 1)
    lim = jnp.minimum(((rows >> 6) + 1) << 6, l_valid)
    last = row0 + tq - 1
    lim_max = jnp.minimum(((last >> 6) + 1) << 6, l_valid)
    nkc = (lim_max + kb - 1) >> 8

    def phase_a(c, carry):
        kbd = kibd_scr[c]
        acc = jnp.zeros((kb, tq), F32)
        for t in range(npair):
            d2 = lax.dot_general(kbd, qi_ref[0, :, t * LANES:(t + 1) * LANES], nt,
                                 preferred_element_type=F32)
            r = IDX_DIM + 2 * t
            acc = acc + w_t[r:r + 1, :] * jnp.maximum(d2[0:kb], 0.0)
            acc = acc + w_t[r + 1:r + 2, :] * jnp.maximum(d2[kb:2 * kb], 0.0)
        col = c * kb + lax.broadcasted_iota(I32, (kb, tq), 0)
        sc = jnp.where(col < lim, acc, -jnp.inf)
        bits = lax.bitcast_convert_type(sc, I32)
        key = bits ^ ((bits >> 31) & 0x7FFFFFFF)
        key_scr[c] = key
        hi_scr[c] = (key >> 16).astype(I16)
        lo_scr[c] = ((key & 0xFFFF) - 32768).astype(I16)
        return carry

    lax.fori_loop(0, nkc, phase_a, 0)

    def count_ge(cand):
        def body(c, cnt):
            m = jnp.where(key_scr[c] >= cand, 1.0, 0.0)
            parts = [m[8 * i:8 * (i + 1)] for i in range(kb // 8)]
            while len(parts) > 1:
                parts = [parts[i] + parts[i + 1] for i in range(0, len(parts), 2)]
            return cnt + parts[0]
        cnt = lax.fori_loop(0, nkc, body, jnp.zeros((8, tq), F32))
        return jnp.sum(cnt, axis=0, keepdims=True)

    def count16(ref, pred):
        def body(c, cnt):
            m = jnp.where(pred(ref[c]), jnp.int16(1), jnp.int16(0))
            parts = [m[16 * i:16 * (i + 1)] for i in range(kb // 16)]
            while len(parts) > 1:
                parts = [parts[i] + parts[i + 1] for i in range(0, len(parts), 2)]
            return cnt + parts[0]
        cnt = lax.fori_loop(0, nkc, body, jnp.zeros((16, tq), I16))
        return jnp.sum(cnt.astype(I32), axis=0, keepdims=True)

    def radix16(ref, need):
        t0 = jnp.where(count16(ref, lambda h: h >= jnp.int16(0)) >= need, 0, -32768).astype(I32)

        def bit_body(i, t):
            cand = t | lax.shift_left(jnp.int32(1), 14 - i)
            c16 = cand.astype(I16)
            return jnp.where(count16(ref, lambda h: h >= c16) >= need, cand, t)

        return lax.fori_loop(0, 15, bit_body, t0)

    t_hi = radix16(hi_scr, topk)
    h16 = t_hi.astype(I16)
    need_lo = topk - count16(hi_scr, lambda h: h > h16)

    def keep_bucket(c, carry):
        lo_scr[c] = jnp.where(hi_scr[c] == h16, lo_scr[c], jnp.int16(-32768))
        return carry

    lax.fori_loop(0, nkc, keep_bucket, 0)
    t_lo = radix16(lo_scr, need_lo)
    thr = (t_hi << 16) | ((t_lo + 32768) & 0xFFFF)
    kf = float(topk)
    few = lim <= topk
    thr = jnp.where(few, KEY_MIN_FINITE, thr)

    tie = jnp.logical_and(count_ge(thr) > kf, jnp.logical_not(few))
    any_tie = jnp.max(jnp.where(tie, 1.0, 0.0)) > 0.0

    @pl.when(any_tie)
    def _():
        need = kf - count_ge(thr + 1)
        r_i = lax.broadcasted_iota(I32, (kb, kb), 0)
        c_i = lax.broadcasted_iota(I32, (kb, kb), 1)
        tri = jnp.where(c_i < r_i, 1.0, 0.0).astype(BF16)

        def body(c, run):
            kc = key_scr[c]
            eqf = jnp.where(kc == thr, 1.0, 0.0)
            rank = run + jnp.dot(tri, eqf.astype(BF16), preferred_element_type=F32)
            drop = jnp.logical_and(jnp.logical_and(tie, kc == thr), rank >= need)
            key_scr[c] = jnp.where(drop, KEY_NEG_INF, kc)
            return run + jnp.sum(eqf, axis=0, keepdims=True)

        lax.fori_loop(0, nkc, body, jnp.zeros((1, tq), F32))

    nl = kb // LANES
    m_scr[...] = jnp.full(m_scr.shape, MASK_BIAS, F32)
    acc_scr[...] = jnp.zeros(acc_scr.shape, F32)

    def sweep1(c, carry):
        bias = jnp.where(key_scr[c] >= thr, 0.0, MASK_BIAS).T
        for g in range(N_KV_HEADS):
            s2 = lax.dot_general(q_ref[0, :, g * LANES:(g + 1) * LANES], kbd_scr[g, c], nt,
                                 preferred_element_type=F32)
            for r in range(GROUP):
                h = GROUP * g + r
                s = s2[:, r * kb:(r + 1) * kb] + bias
                s_scr[h, c] = s
                mx = m_scr[h]
                for i in range(nl):
                    mx = jnp.maximum(mx, s[:, i * LANES:(i + 1) * LANES])
                m_scr[h] = mx
        return carry

    lax.fori_loop(0, nkc, sweep1, 0)
    for h in range(N_HEADS):
        m_scr[h] = jnp.broadcast_to(jnp.max(m_scr[h], axis=1, keepdims=True), (tq, LANES))

    def sweep2(c, carry):
        for g in range(N_KV_HEADS):
            p2 = []
            for r in range(GROUP):
                h = GROUP * g + r
                s = s_scr[h, c]
                mb = m_scr[h]
                p2 += [jnp.exp2(s[:, i * LANES:(i + 1) * LANES] - mb) for i in range(nl)]
            pb = jnp.concatenate(p2, axis=1).astype(BF16)
            acc_scr[g] += jnp.dot(pb, vbd_scr[g, c], preferred_element_type=F32)
        return carry

    lax.fori_loop(0, nkc, sweep2, 0)
    for g in range(N_KV_HEADS):
        o_ref[0, :, g * LANES:(g + 1) * LANES] = (acc_scr[g, :, 0:LANES]
                                                  / acc_scr[g, :, LANES:2 * LANES]).astype(BF16)


def _attention(q, qi, kiw, k_all, v_all, ki_all, *, layer, tq, pos0, l_valid, topk):
    b, s, _ = q.shape
    l_pad = k_all.shape[2]
    nkc_max = l_pad // KEY_BLOCK
    qspec = lambda w: pl.BlockSpec((1, tq, w), lambda bi, ji: (bi, ji, 0))
    kspec = lambda w: pl.BlockSpec((1, l_pad, w), lambda bi, ji: (bi, 0, 0))
    lspec = lambda w: pl.BlockSpec((1, 1, l_pad, w), lambda bi, ji: (layer, bi, 0, 0))
    return pl.pallas_call(
        functools.partial(_attn_kernel, tq=tq, pos0=pos0, l_valid=l_valid, topk=topk),
        grid=(b, s // tq),
        in_specs=[qspec(D_Q), qspec(D_QI), qspec(LANES), lspec(D_KV), lspec(D_KV), kspec(LANES)],
        out_specs=qspec(D_Q),
        out_shape=jax.ShapeDtypeStruct((b, s, D_Q), BF16),
        scratch_shapes=[pltpu.VMEM((N_KV_HEADS, nkc_max, 2 * KEY_BLOCK, LANES), BF16),
                        pltpu.VMEM((N_KV_HEADS, nkc_max, 2 * KEY_BLOCK, 2 * LANES), BF16),
                        pltpu.VMEM((nkc_max, 2 * KEY_BLOCK, LANES), BF16),
                        pltpu.VMEM((nkc_max, KEY_BLOCK, tq), I32),
                        pltpu.VMEM((nkc_max, KEY_BLOCK, tq), I16),
                        pltpu.VMEM((nkc_max, KEY_BLOCK, tq), I16),
                        pltpu.VMEM((N_HEADS, nkc_max, tq, KEY_BLOCK), F32),
                        pltpu.VMEM((N_HEADS, tq, LANES), F32),
                        pltpu.VMEM((N_KV_HEADS, tq, 2 * LANES), F32)],
        compiler_params=_cparams(2), name="attention",
    )(q, qi, kiw, k_all, v_all, ki_all)


def _split3(a):
    hi = a.astype(BF16)
    r = a - hi.astype(F32)
    mid = r.astype(BF16)
    lo = (r - mid.astype(F32)).astype(BF16)
    return hi, mid, lo


def _merge_kernel(*refs, with_router):
    if with_router:
        (x_ref, ya_ref, ao_ref, sg_ref, mod_ref, wc_ref, wa_ref, wo_ref, g2_ref, rw_ref, rb_ref,
         x1_ref, h2_ref, lg_ref, cnt_ref, cnt_scr) = refs
    else:
        (x_ref, ya_ref, ao_ref, sg_ref, mod_ref, wc_ref, wa_ref, wo_ref, g2_ref,
         x1_ref, h2_ref) = refs
    a = jnp.dot(ya_ref[0], wc_ref[...], preferred_element_type=F32)
    b = jnp.dot(ao_ref[0], wa_ref[...], preferred_element_type=F32)
    merged = sg_ref[0, :, 0:D_MODEL].astype(F32) * a + sg_ref[0, :, D_MODEL:2 * D_MODEL].astype(F32) * b
    mix = jnp.dot(merged.astype(BF16), wo_ref[...], preferred_element_type=F32)
    x1 = x_ref[0] + mod_ref[0, 2] * mix
    x1_ref[0] = x1
    h2 = _rmsnorm(x1, g2_ref[...]) * (1.0 + mod_ref[0, 4]) + mod_ref[0, 3]
    h2_ref[0] = h2.astype(h2_ref.dtype)
    if with_router:
        hs = _split3(h2)
        acc = jnp.zeros((h2.shape[0], LANES), F32)
        for i, jj in ((0, 0), (0, 1), (1, 0)):
            acc = acc + jnp.dot(hs[i], rw_ref[jj], preferred_element_type=F32)
        lane = lax.broadcasted_iota(I32, acc.shape, 1)
        lg = jnp.where(lane < N_EXPERTS, acc + rb_ref[...], -jnp.inf)
        m1 = jnp.max(lg, axis=1, keepdims=True)
        i1 = jnp.min(jnp.where(lg == m1, lane, LANES), axis=1, keepdims=True)
        rest = jnp.where(lane == i1, -jnp.inf, lg)
        m2 = jnp.max(rest, axis=1, keepdims=True)
        i2 = jnp.min(jnp.where(rest == m2, lane, LANES), axis=1, keepdims=True)
        e2 = jnp.exp(m2 - m1)
        den = 1.0 + e2
        first = jnp.logical_and(pl.program_id(0) == 0, pl.program_id(1) == 0)

        @pl.when(first)
        def _():
            cnt_scr[...] = jnp.zeros(cnt_scr.shape, F32)

        tm = acc.shape[0]
        hit1 = lane == i1
        hit2 = lane == i2
        onehot = jnp.where(jnp.logical_or(hit1, hit2), 1.0, 0.0)
        r_i = lax.broadcasted_iota(I32, (tm, tm), 0)
        c_i = lax.broadcasted_iota(I32, (tm, tm), 1)
        tri = jnp.where(c_i < r_i, 1.0, 0.0).astype(BF16)
        before = cnt_scr[...] + jnp.dot(tri, onehot.astype(BF16), preferred_element_type=F32)
        rank1 = jnp.sum(jnp.where(hit1, before, 0.0), axis=1, keepdims=True)
        rank2 = jnp.sum(jnp.where(hit2, before, 0.0), axis=1, keepdims=True)
        total = cnt_scr[...] + jnp.sum(onehot, axis=0, keepdims=True)
        cnt_scr[...] = total
        cnt_ref[...] = total
        vals = (i1.astype(F32), i2.astype(F32), 1.0 / den, e2 / den, rank1, rank2)
        route = jnp.zeros(acc.shape, F32)
        for col, val in enumerate(vals):
            route = jnp.where(lane == col, val, route)
        lg_ref[0] = route


def _merge(x, ya, ao, sg, mod, wc, wa, wo, g2, router=None, *, tm):
    b, s, d = x.shape
    r = mod.shape[2]
    rspec = lambda w: pl.BlockSpec((1, tm, w), lambda bi, ti: (bi, ti, 0))
    mspec = (pl.BlockSpec((1, 6, 1, d), lambda bi, ti: (bi, 0, 0, 0)) if r == 1 else
             pl.BlockSpec((1, 6, tm, d), lambda bi, ti: (bi, 0, ti, 0)))
    in_specs = [rspec(d), rspec(D_CONV), rspec(D_Q), rspec(2 * d), mspec,
                _resident((D_CONV, d)), _resident((D_Q, d)), _resident((d, d)), _resident((1, d))]
    args = [x, ya, ao, sg, mod, wc, wa, wo, g2]
    out_specs = [rspec(d), rspec(d)]
    out_shape = [jax.ShapeDtypeStruct((b, s, d), F32),
                 jax.ShapeDtypeStruct((b, s, d), BF16 if router is None else F32)]
    scratch = []
    if router is not None:
        in_specs += [_resident((3, d, LANES)), _resident((1, LANES))]
        args += list(router)
        out_specs += [rspec(LANES), pl.BlockSpec((1, LANES), lambda bi, ti: (0, 0))]
        out_shape += [jax.ShapeDtypeStruct((b, s, LANES), F32), jax.ShapeDtypeStruct((1, LANES), F32)]
        scratch = [pltpu.VMEM((1, LANES), F32)]
    return pl.pallas_call(
        functools.partial(_merge_kernel, with_router=router is not None),
        grid=(b, s // tm),
        in_specs=in_specs, out_specs=out_specs, out_shape=out_shape, scratch_shapes=scratch,
        compiler_params=_cparams(2), name="merge",
    )(*args)


def _ffn_kernel(h2_ref, x1_ref, mod_ref, w1_ref, w3_ref, w2_ref, o_ref, acc_scr, *, nc):
    h = h2_ref[0]
    acc_scr[...] = _swiglu_chunk(h, w1_ref[0], w3_ref[0], w2_ref[0])

    def body(c, carry):
        acc_scr[...] += _swiglu_chunk(h, w1_ref[c], w3_ref[c], w2_ref[c])
        return carry

    lax.fori_loop(1, nc, body, 0)
    o_ref[0] = x1_ref[0] + mod_ref[0, 5] * acc_scr[...]


def _ffn(h2, x1, mod, w1, w3, w2, *, tm):
    b, s, d = x1.shape
    nc = w1.shape[0]
    r = mod.shape[2]
    rspec = lambda w: pl.BlockSpec((1, tm, w), lambda bi, ti: (bi, ti, 0))
    mspec = (pl.BlockSpec((1, 6, 1, d), lambda bi, ti: (bi, 0, 0, 0)) if r == 1 else
             pl.BlockSpec((1, 6, tm, d), lambda bi, ti: (bi, 0, ti, 0)))
    return pl.pallas_call(
        functools.partial(_ffn_kernel, nc=nc),
        grid=(b, s // tm),
        in_specs=[rspec(d), rspec(d), mspec, _resident(w1.shape), _resident(w3.shape), _resident(w2.shape)],
        out_specs=rspec(d),
        out_shape=jax.ShapeDtypeStruct((b, s, d), F32),
        scratch_shapes=[pltpu.VMEM((tm, d), F32)],
        compiler_params=_cparams(2), name="ffn",
    )(h2, x1, mod, w1, w3, w2)


def _swiglu_chunk(xb, w1, w3, w2):
    a = jnp.dot(xb, w1, preferred_element_type=F32)
    hm = (a * _sigmoid(a) * jnp.dot(xb, w3, preferred_element_type=F32)).astype(BF16)
    return jnp.dot(hm, w2, preferred_element_type=F32)


def _moe_dense_kernel(h2_ref, x1_ref, rt_ref, mod_ref, w1_ref, w3_ref, w2_ref, fg_ref, o_ref,
                      eacc_scr, acc_scr, *, final_norm):
    e = pl.program_id(2)
    c = pl.program_id(3)
    ne = pl.num_programs(2)
    nc = pl.num_programs(3)

    @pl.when(jnp.logical_and(e == 0, c == 0))
    def _():
        acc_scr[...] = jnp.zeros(acc_scr.shape, F32)

    @pl.when(c == 0)
    def _():
        eacc_scr[...] = jnp.zeros(eacc_scr.shape, F32)

    eacc_scr[...] += _swiglu_chunk(h2_ref[0].astype(BF16), w1_ref[0], w3_ref[0], w2_ref[0])

    @pl.when(c == nc - 1)
    def _():
        rt = rt_ref[0]
        ef = e.astype(F32)
        gate = (jnp.where(rt[:, 0:1] == ef, rt[:, 2:3], 0.0)
                + jnp.where(rt[:, 1:2] == ef, rt[:, 3:4], 0.0))
        acc_scr[...] += gate * eacc_scr[...]

    @pl.when(jnp.logical_and(e == ne - 1, c == nc - 1))
    def _():
        x2 = x1_ref[0] + mod_ref[0, 5] * acc_scr[...]
        o_ref[0] = _rmsnorm(x2, fg_ref[...]) if final_norm else x2


def _moe_dense(h2, x1, route, mod, w1, w3, w2, fg, *, tm, final_norm):
    b, s, d = x1.shape
    ne, _, f = w1.shape
    fc = MOE_CHUNK
    r = mod.shape[2]
    rspec = lambda w: pl.BlockSpec((1, tm, w), lambda bi, ti, ei, ci: (bi, ti, 0))
    mspec = (pl.BlockSpec((1, 6, 1, d), lambda bi, ti, ei, ci: (bi, 0, 0, 0)) if r == 1 else
             pl.BlockSpec((1, 6, tm, d), lambda bi, ti, ei, ci: (bi, 0, ti, 0)))
    wcol = pl.BlockSpec((1, d, fc), lambda bi, ti, ei, ci: (ei, 0, ci))
    return pl.pallas_call(
        functools.partial(_moe_dense_kernel, final_norm=final_norm),
        grid=(b, s // tm, ne, f // fc),
        in_specs=[rspec(d), rspec(d), rspec(LANES), mspec, wcol, wcol,
                  pl.BlockSpec((1, fc, d), lambda bi, ti, ei, ci: (ei, ci, 0)),
                  _resident((1, d))],
        out_specs=rspec(d),
        out_shape=jax.ShapeDtypeStruct((b, s, d), F32),
        scratch_shapes=[pltpu.VMEM((tm, d), F32), pltpu.VMEM((tm, d), F32)],
        compiler_params=_cparams(4), name="moe_dense",
    )(h2, x1, route, mod, w1, w3, w2, fg)


def _route_tables(route, counts, tmr):
    n = route.shape[0]
    e1 = route[:, 0].astype(I32)
    e2 = route[:, 1].astype(I32)
    cnt = counts[0, :N_EXPERTS].astype(I32)
    tiles_e = (cnt + tmr - 1) // tmr
    tile_end = jnp.cumsum(tiles_e)
    tile_start = tile_end - tiles_e
    row_off = tile_start * tmr
    d1 = row_off[e1] + route[:, 4].astype(I32)
    d2 = row_off[e2] + route[:, 5].astype(I32)
    nt = -(-2 * n // tmr) + N_EXPERTS
    tok = jnp.arange(n, dtype=I32)
    pair = jnp.zeros((nt * tmr,), I32).at[jnp.concatenate([d1, d2])].set(
        jnp.concatenate([2 * tok, 2 * tok + 1]), unique_indices=True)
    src = pair >> 1
    dst = (pair & 1) * n + src
    tile = jnp.arange(nt, dtype=I32)
    te = jnp.minimum(jnp.searchsorted(tile_end, tile, side="right"), N_EXPERTS - 1).astype(I32)
    nv = jnp.where(tile < tile_end[-1], jnp.clip(cnt[te] - (tile - tile_start[te]) * tmr, 0, tmr), 0)
    return (src.reshape(nt, 1, tmr), dst.reshape(nt, 1, tmr), te, nv.astype(I32),
            tile_end[-1:].astype(I32))


def _moe_routed_kernel(te_ref, nv_ref, nu_ref, src0_ref, srcn_ref, dstp_ref, h2_hbm, w1_ref, w3_ref, w2_ref,
                       out_hbm, xbuf, xb_scr, ybuf, gsem, ssem, *, tmr, nc, gs):
    i = pl.program_id(0)
    c = pl.program_id(1)
    nu = nu_ref[0]
    slot = i % 2
    other = 1 - slot
    rpc = tmr // nc
    rpg = tmr // gs

    def gather(src_ref, cc, k, sl):
        return pltpu.make_async_copy(h2_hbm.at[pl.ds(src_ref[0, 0, cc * rpg + k], 1)],
                                     xbuf.at[sl, cc, pl.ds(k, 1)], gsem.at[sl])

    def scatter(cc, k, sl):
        return pltpu.make_async_copy(ybuf.at[sl, cc, pl.ds(k, 1)],
                                     out_hbm.at[pl.ds(dstp_ref[0, 0, cc * rpc + k], 1)], ssem.at[sl])

    @pl.when(jnp.logical_and(i == 0, c == 0))
    def _():
        def body(cc, carry):
            for k in range(rpg):
                gather(src0_ref, cc, k, 0).start()
            return carry
        lax.fori_loop(0, gs, body, 0)

    @pl.when(jnp.logical_and(c == 0, i < nu))
    def _():
        pltpu.make_async_copy(xbuf.at[slot], xbuf.at[slot], gsem.at[slot]).wait()
        xb_scr[...] = xbuf[slot].reshape(tmr, xb_scr.shape[1]).astype(BF16)

    last = nv_ref.shape[0] - 1
    nv_drain = nv_ref[jnp.clip(i - 2, 0, last)]
    push = jnp.logical_and(i >= 1, i - 1 < nu)
    nv_push = nv_ref[jnp.clip(i - 1, 0, last)]
    fast = jnp.logical_and(i >= 1, jnp.logical_and(i + 1 < nu, nv_push == tmr))
    slow = jnp.logical_not(fast)
    drain = jnp.logical_and(c == 0, jnp.logical_and(i >= 2, i - 2 < nu))

    @pl.when(jnp.logical_and(drain, nv_drain == tmr))
    def _():
        pltpu.make_async_copy(ybuf.at[slot], ybuf.at[slot], ssem.at[slot]).wait()

    @pl.when(jnp.logical_and(drain, nv_drain < tmr))
    def _():
        def body(r, carry):
            scatter(0, 0, slot).wait()
            return carry
        lax.fori_loop(0, nv_drain, body, 0)

    @pl.when(fast)
    def _():
        xb = xb_scr[...]
        a = jnp.dot(xb, w1_ref[0], preferred_element_type=F32)
        b = jnp.dot(xb, w3_ref[0], preferred_element_type=F32)
        for k in range(rpg):
            gather(srcn_ref, c, k, other).start()
        for k in range(rpc):
            scatter(c, k, other).start()
        hm = (a * _sigmoid(a) * b).astype(BF16)
        y = jnp.dot(hm, w2_ref[0], preferred_element_type=F32).reshape(nc, rpc, xb_scr.shape[1])
        ybuf[slot] = jnp.where(c == 0, y, ybuf[slot] + y)

    @pl.when(jnp.logical_and(slow, jnp.logical_and(i + 1 < nu, c < gs)))
    def _():
        for k in range(rpg):
            gather(srcn_ref, c, k, other).start()

    @pl.when(jnp.logical_and(slow, jnp.logical_and(push, nv_push == tmr)))
    def _():
        for k in range(rpc):
            scatter(c, k, other).start()

    @pl.when(jnp.logical_and(slow, jnp.logical_and(push, nv_push < tmr)))
    def _():
        def body(k, carry):
            scatter(c, k, other).start()
            return carry
        lax.fori_loop(0, jnp.clip(nv_push - c * rpc, 0, rpc), body, 0)

    @pl.when(jnp.logical_and(slow, i < nu))
    def _():
        y = _swiglu_chunk(xb_scr[...], w1_ref[0], w3_ref[0], w2_ref[0]).reshape(nc, rpc, xb_scr.shape[1])

        @pl.when(c == 0)
        def _():
            ybuf[slot] = y

        @pl.when(c > 0)
        def _():
            ybuf[slot] += y


def _moe_routed(h2, route, counts, w1, w3, w2, *, tmr):
    n, d = h2.shape
    f = w1.shape[2]
    fc = MOE_CHUNK
    nc = f // fc
    gs = MOE_GATHER_STEPS
    src, dst, te, nv, nu = _route_tables(route, counts, tmr)
    nt = src.shape[0]
    last = nt - 1
    smem = lambda imap: pl.BlockSpec((1, 1, tmr), imap, memory_space=pltpu.SMEM)
    chunk = lambda i, c, nu_r: jnp.where(i < nu_r[0], c, nc - 1)
    wcol = pl.BlockSpec((1, d, fc), lambda i, c, te_r, nv_r, nu_r: (te_r[jnp.minimum(i, last)], 0, chunk(i, c, nu_r)))
    wrow = pl.BlockSpec((1, fc, d), lambda i, c, te_r, nv_r, nu_r: (te_r[jnp.minimum(i, last)], chunk(i, c, nu_r), 0))
    return pl.pallas_call(
        functools.partial(_moe_routed_kernel, tmr=tmr, nc=nc, gs=gs),
        grid_spec=pltpu.PrefetchScalarGridSpec(
            num_scalar_prefetch=3,
            grid=(nt + 2, nc),
            in_specs=[smem(lambda i, c, te_r, nv_r, nu_r: (0, 0, 0)),
                      smem(lambda i, c, te_r, nv_r, nu_r: (jnp.minimum(i + 1, last), 0, 0)),
                      smem(lambda i, c, te_r, nv_r, nu_r: (jnp.clip(i - 1, 0, last), 0, 0)),
                      pl.BlockSpec(memory_space=pl.ANY), wcol, wcol, wrow],
            out_specs=pl.BlockSpec(memory_space=pl.ANY),
            scratch_shapes=[pltpu.VMEM((2, gs, tmr // gs, d), F32), pltpu.VMEM((tmr, d), BF16),
                            pltpu.VMEM((2, nc, tmr // nc, d), F32),
                            pltpu.SemaphoreType.DMA((2,)), pltpu.SemaphoreType.DMA((2,))]),
        out_shape=jax.ShapeDtypeStruct((2 * n, d), F32),
        compiler_params=_cparams(2), name="moe_routed",
    )(te, nv, nu, src, src, dst, h2, w1, w3, w2)


def _moe_combine_kernel(x1_ref, ya_ref, yb_ref, rt_ref, mod_ref, fg_ref, o_ref, *, final_norm):
    rt = rt_ref[0]
    f = rt[:, 2:3] * ya_ref[...] + rt[:, 3:4] * yb_ref[...]
    x2 = x1_ref[0] + mod_ref[0, 5] * f
    o_ref[0] = _rmsnorm(x2, fg_ref[...]) if final_norm else x2


def _moe_combine(x1, y2, route, mod, fg, *, tm, final_norm):
    b, s, d = x1.shape
    nt = s // tm
    rspec = lambda w: pl.BlockSpec((1, tm, w), lambda bi, ti: (bi, ti, 0))
    return pl.pallas_call(
        functools.partial(_moe_combine_kernel, final_norm=final_norm),
        grid=(b, nt),
        in_specs=[rspec(d),
                  pl.BlockSpec((tm, d), lambda bi, ti: (bi * nt + ti, 0)),
                  pl.BlockSpec((tm, d), lambda bi, ti: (b * nt + bi * nt + ti, 0)),
                  rspec(LANES),
                  pl.BlockSpec((1, 6, 1, d), lambda bi, ti: (bi, 0, 0, 0)), _resident((1, d))],
        out_specs=rspec(d),
        out_shape=jax.ShapeDtypeStruct((b, s, d), F32),
        compiler_params=_cparams(2), name="moe_combine",
    )(x1, y2, y2, route, mod, fg)


def _pack_w_in(w):
    cut = C_KIW + IDX_DIM + N_IDX_HEADS
    pad = jnp.zeros((w.shape[0], C_GATE - cut), w.dtype)
    return jnp.concatenate([w[:, :cut], pad, w[:, cut:]], axis=1).astype(BF16)


def _chunk_cols(w, fc):
    *lead, d, f = w.shape
    return jnp.moveaxis(w.astype(BF16).reshape(*lead, d, f // fc, fc), -2, -3)


def _chunk_w2(w2, fc):
    *lead, f, d = w2.shape
    return w2.reshape(*lead, f // fc, fc, d).astype(BF16)


def _router_terms(rw, rb):
    d = rw.shape[0]
    rwp = jnp.zeros((d, LANES), F32).at[:, :N_EXPERTS].set(rw)
    hi, mid, lo = _split3(rwp)
    rbp = jnp.zeros((1, LANES), F32).at[0, :N_EXPERTS].set(rb)
    return jnp.stack([hi, mid, lo]), rbp


def _trunk(x, mod_all, caches, p, *, tm, tq):
    b, t, d = x.shape
    depth = mod_all.shape[0]
    flat = caches is not None
    stacks, tails = None, []
    for l in range(depth):
        mod = mod_all[l].reshape(b, 6, 1, d)
        if caches is None:
            conv_state = jnp.zeros((b, CONV_W - 1, D_CONV), F32)
        else:
            conv_state = caches[3][l]
        ya, q, k_st, v_st, qi, kiw, ki_st, sg, tail = _inproj(
            x, mod, p["norm1_g"][l], p["w_in"][l], p["conv_w"][l], conv_state, tm,
            layer=l, depth=depth, stacks=stacks)
        stacks = (k_st, v_st, ki_st)
        if caches is None:
            ao = _attention(q, qi, kiw, k_st, v_st, kiw, layer=l, tq=tq, pos0=0, l_valid=t,
                            topk=min(TOPK_MAX, t // 4))
        else:
            past = caches[0][l].shape[1]
            l_valid = past + t
            l_pad = -(-l_valid // KEY_BLOCK) * KEY_BLOCK
            cat = lambda old, new: jnp.concatenate(
                [old.reshape(b, past, -1), new, jnp.zeros((b, l_pad - l_valid, new.shape[-1]), F32)], axis=1)
            k_all, v_all = cat(caches[0][l], k_st[l])[None], cat(caches[1][l], v_st[l])[None]
            ki_all = jnp.pad(cat(caches[2][l], ki_st[l]), ((0, 0), (0, 0), (0, LANES - IDX_DIM)))
            padq = lambda a: jnp.pad(a, ((0, 0), (0, tq - t), (0, 0)))
            ao = _attention(padq(q), padq(qi), padq(kiw), k_all, v_all, ki_all, layer=0, tq=tq, pos0=past,
                            l_valid=l_valid, topk=min(TOPK_MAX, l_valid // 4))[:, :t]
        is_moe = l % 2 == 1
        router = (p["router_w"][l // 2], p["router_b"][l // 2]) if is_moe else None
        if flat:
            n = b * t
            fl = lambda a: a.reshape(1, n, a.shape[-1])
            modf = jnp.repeat(mod_all[l].reshape(b, 6, d), t, axis=0).transpose(1, 0, 2)[None]
            xs, yas, aos, sgs, tmf = fl(x), fl(ya), fl(ao), fl(sg), n
        else:
            modf, xs, yas, aos, sgs, tmf = mod, x, ya, ao, sg, tm
        outs = _merge(xs, yas, aos, sgs, modf, p["w_conv_out"][l], p["w_attn_out"][l], p["w_o"][l],
                      p["norm2_g"][l], router, tm=tmf)
        if is_moe and flat:
            x1, h2, route, _ = outs
            x = _moe_dense(h2, x1, route, modf, p["moe_w1"][l // 2], p["moe_w3"][l // 2],
                           p["moe_w2"][l // 2], p["final_g"], tm=xs.shape[1], final_norm=l == depth - 1)
        elif is_moe:
            x1, h2, route, counts = outs
            y2 = _moe_routed(h2.reshape(b * t, d), route.reshape(b * t, LANES), counts, p["moe_w1"][l // 2],
                             p["moe_w3"][l // 2], p["moe_w2"][l // 2], tmr=MOE_ROWS)
            x = _moe_combine(x1, y2, route, mod, p["final_g"], tm=tm,
                             final_norm=l == depth - 1)
        else:
            x1, h2 = outs
            x = _ffn(h2, x1, modf, p["ffn_w1"][l // 2], p["ffn_w3"][l // 2], p["ffn_w2"][l // 2],
                     tm=tmf if flat else min(2 * tm, t))
        x = x.reshape(b, t, d)
        tails.append(tail)
    k_st, v_st, ki_st = stacks
    heads = (depth, b, t, N_KV_HEADS, HEAD_DIM)
    return x, k_st.reshape(heads), v_st.reshape(heads), ki_st, jnp.stack(tails)


def kernel(x_prompt, x_sample, c_prompt, c_sample, cache_k, cache_v, cache_idx_k, state_conv, w_ada, b_ada, norm1_g, w_in, conv_w, w_conv_out, w_attn_out, w_o, norm2_g, ffn_w1, ffn_w3, ffn_w2, router_w, router_b, moe_w1, moe_w3, moe_w2, final_g):
    depth = w_in.shape[0]
    bp = x_prompt.shape[0]
    rterms = [_router_terms(router_w[i], router_b[i]) for i in range(router_w.shape[0])]
    p = {
        "norm1_g": norm1_g.reshape(depth, 1, -1),
        "norm2_g": norm2_g.reshape(depth, 1, -1),
        "final_g": final_g.reshape(1, -1),
        "w_in": jnp.stack([_pack_w_in(w_in[l]) for l in range(depth)]),
        "conv_w": conv_w,
        "w_conv_out": w_conv_out.astype(BF16),
        "w_attn_out": w_attn_out.astype(BF16),
        "w_o": w_o.astype(BF16),
        "ffn_w1": _chunk_cols(ffn_w1, FFN_CHUNK),
        "ffn_w3": _chunk_cols(ffn_w3, FFN_CHUNK),
        "ffn_w2": _chunk_w2(ffn_w2, FFN_CHUNK),
        "moe_w1": moe_w1.astype(BF16),
        "moe_w3": moe_w3.astype(BF16),
        "moe_w2": moe_w2.astype(BF16),
        "router_w": [r[0] for r in rterms],
        "router_b": [r[1] for r in rterms],
    }
    mod_all = _adaln(jnp.concatenate([c_prompt, c_sample], axis=0), w_ada.astype(BF16), b_ada)
    seq = x_prompt.shape[1]
    tm = min(512, seq)
    tq = min(256, seq)
    yp, kp, vp, kip, cp = _trunk(x_prompt, mod_all[:, :bp], None, p, tm=tm, tq=tq)
    ts = x_sample.shape[1]
    ys, ksm, vsm, kism, csm = _trunk(x_sample, mod_all[:, bp:], (cache_k, cache_v, cache_idx_k, state_conv), p,
                                     tm=ts, tq=-(-ts // LANES) * LANES)
    return (yp, ys, kp, vp, kip, cp, ksm, vsm, kism, csm)
```

```python
import functools

import jax
import jax.numpy as jnp
from jax import lax
from jax.experimental import pallas as pl
from jax.experimental.pallas import tpu as pltpu

F32 = jnp.float32
BF16 = jnp.bfloat16
I32 = jnp.int32
I16 = jnp.int16

D_MODEL = 1024
D_CONV = 512
CONV_W = 3
N_HEADS = 8
N_KV_HEADS = 4
HEAD_DIM = 64
GROUP = N_HEADS // N_KV_HEADS
N_IDX_HEADS = 16
IDX_DIM = 64
TOPK_MAX = 256
CHUNK = 64
D_FF = 2816
N_EXPERTS = 8
D_FF_EXPERT = 3584
EPS = 1e-6

D_Q = N_HEADS * HEAD_DIM
D_KV = N_KV_HEADS * HEAD_DIM
D_QI = N_IDX_HEADS * IDX_DIM
LANES = 128
C_CONV = 0
C_Q = 3 * D_CONV
C_KV = C_Q + D_Q
C_QI = C_KV + 2 * D_KV
C_KIW = C_QI + D_QI
C_GATE = C_KIW + LANES
D_PACK = C_GATE + 2 * D_MODEL
WI_SCALE = (IDX_DIM ** -0.5) * (N_IDX_HEADS ** -0.5)
Q_SCALE = HEAD_DIM ** -0.5 * 1.4426950408889634

KEY_BLOCK = 256
FFN_CHUNK = 256
MOE_CHUNK = 512
MOE_ROWS = (D_FF_EXPERT // MOE_CHUNK) * 128
MOE_GATHER_STEPS = 4
V7X_VMEM_LIMIT = 56 * 1024 * 1024

KEY_NEG_INF = -2139095041
KEY_MIN_FINITE = KEY_NEG_INF + 1
MASK_BIAS = -1e30


def _sigmoid(x):
    return 1.0 / (1.0 + jnp.exp(-x))


def _rmsnorm(x, g):
    ms = jnp.mean(x * x, axis=-1, keepdims=True)
    return x * lax.rsqrt(ms + EPS) * g


def _cparams(n_axes):
    return pltpu.CompilerParams(dimension_semantics=("arbitrary",) * n_axes,
                                vmem_limit_bytes=V7X_VMEM_LIMIT)


def _resident(shape):
    nd = len(shape)
    return pl.BlockSpec(shape, lambda *_: (0,) * nd, pipeline_mode=pl.Buffered(1))


def _adaln_kernel(c_ref, w_ref, b_ref, o_ref):
    c = c_ref[...]
    cond = (c * _sigmoid(c)).astype(BF16)
    o_ref[0] = jnp.dot(cond, w_ref[0], preferred_element_type=F32) + b_ref[0]


def _adaln(c_all, w_ada, b_ada):
    depth, d, n6 = w_ada.shape
    bc = c_all.shape[0]
    nblk = n6 // D_MODEL
    return pl.pallas_call(
        _adaln_kernel,
        grid=(depth, nblk),
        in_specs=[pl.BlockSpec((bc, d), lambda l, n: (0, 0)),
                  pl.BlockSpec((1, d, D_MODEL), lambda l, n: (l, 0, n)),
                  pl.BlockSpec((1, 1, D_MODEL), lambda l, n: (l, 0, n))],
        out_specs=pl.BlockSpec((1, bc, D_MODEL), lambda l, n: (l, 0, n)),
        out_shape=jax.ShapeDtypeStruct((depth, bc, n6), F32),
        compiler_params=_cparams(2), name="adaln",
    )(c_all, w_ada, b_ada.reshape(depth, 1, n6))


def _inproj_kernel(*refs, tm, layer):
    if layer == 0:
        (x_ref, mod_ref, g_ref, w_ref, cw_ref, st_ref,
         ya_ref, q_ref, k_ref, v_ref, qi_ref, kiw_ref, ki_ref, sg_ref, tail_ref, state_scr) = refs
    else:
        (x_ref, mod_ref, g_ref, w_ref, cw_ref, st_ref, _, _, _,
         ya_ref, q_ref, k_ref, v_ref, qi_ref, kiw_ref, ki_ref, sg_ref, tail_ref, state_scr) = refs

    def put(ref, val):
        if layer == 0:
            ref[0, 0] = val
            for other_layer in range(1, ref.shape[0]):
                ref[other_layer, 0] = jnp.zeros_like(val)
        else:
            ref[0, 0] = val

    t = pl.program_id(1)

    @pl.when(t == 0)
    def _():
        state_scr[...] = st_ref[0]

    x = x_ref[0]
    h = _rmsnorm(x, g_ref[...]) * (1.0 + mod_ref[0, 1]) + mod_ref[0, 0]
    hb = h.astype(BF16)

    def seg(a, b):
        return jnp.dot(hb, w_ref[:, a:b], preferred_element_type=F32)

    bg = seg(C_CONV, C_CONV + D_CONV)
    u = seg(C_CONV + D_CONV, C_CONV + 2 * D_CONV) * seg(C_CONV + 2 * D_CONV, C_CONV + 3 * D_CONV)
    s0 = state_scr[0:1, :]
    s1 = state_scr[1:2, :]
    ri = lax.broadcasted_iota(I32, u.shape, 0)
    um1 = jnp.where(ri == 0, s1, pltpu.roll(u, 1, 0))
    um2 = jnp.where(ri == 0, s0, jnp.where(ri == 1, s1, pltpu.roll(u, 2, 0)))
    cw = cw_ref[...]
    y = cw[0:1] * um2
    y = y + cw[1:2] * um1
    y = y + cw[2:3] * u
    ya_ref[0] = (bg * y).astype(BF16)
    tail = u[tm - 2:tm, :]
    state_scr[...] = tail
    tail_ref[0] = tail

    q_ref[0] = (seg(C_Q, C_Q + D_Q) * Q_SCALE).astype(BF16)
    put(k_ref, seg(C_KV, C_KV + D_KV))
    put(v_ref, seg(C_KV + D_KV, C_KV + 2 * D_KV))
    half = D_QI // 2
    qi_ref[0, :, 0:half] = seg(C_QI, C_QI + half).astype(BF16)
    qi_ref[0, :, half:D_QI] = seg(C_QI + half, C_QI + D_QI).astype(BF16)
    kw = seg(C_KIW, C_KIW + LANES)
    put(ki_ref, kw[:, 0:IDX_DIM])
    lane = lax.broadcasted_iota(I32, (1, LANES), 1)
    scale = jnp.where(lane < IDX_DIM, 1.0, jnp.where(lane < IDX_DIM + N_IDX_HEADS, WI_SCALE, 0.0))
    kiw_ref[0] = kw * scale

    for i in range(4):
        a = C_GATE + i * 512
        sg_ref[0, :, i * 512:(i + 1) * 512] = _sigmoid(seg(a, a + 512)).astype(BF16)


def _inproj(x, mod, g1, w_pack, conv_w, conv_state, tm, *, layer, depth, stacks):
    b, s, d = x.shape
    nt = s // tm
    row = lambda w, dt: jax.ShapeDtypeStruct((b, s, w), dt)
    stk = lambda w: jax.ShapeDtypeStruct((depth, b, s, w), F32)
    rspec = lambda w: pl.BlockSpec((1, tm, w), lambda bi, ti: (bi, ti, 0))
    if layer == 0:
        sspec = lambda w: pl.BlockSpec((depth, 1, tm, w), lambda bi, ti: (0, bi, ti, 0))
    else:
        sspec = lambda w: pl.BlockSpec((1, 1, tm, w), lambda bi, ti: (layer, bi, ti, 0))
    in_specs = [rspec(d),
                pl.BlockSpec((1, 6, 1, d), lambda bi, ti: (bi, 0, 0, 0)),
                _resident((1, d)),
                _resident((d, D_PACK)),
                _resident((CONV_W, D_CONV)),
                pl.BlockSpec((1, CONV_W - 1, D_CONV), lambda bi, ti: (bi, 0, 0))]
    args = [x, mod, g1, w_pack, conv_w, conv_state]
    aliases = {}
    if layer > 0:
        in_specs += [pl.BlockSpec(memory_space=pl.ANY)] * 3
        aliases = {len(args): 2, len(args) + 1: 3, len(args) + 2: 6}
        args += list(stacks)
    return pl.pallas_call(
        functools.partial(_inproj_kernel, tm=tm, layer=layer),
        grid=(b, nt),
        in_specs=in_specs,
        out_specs=[rspec(D_CONV), rspec(D_Q), sspec(D_KV), sspec(D_KV), rspec(D_QI),
                   rspec(LANES), sspec(IDX_DIM), rspec(2 * D_MODEL),
                   pl.BlockSpec((1, CONV_W - 1, D_CONV), lambda bi, ti: (bi, 0, 0))],
        out_shape=[row(D_CONV, BF16), row(D_Q, BF16), stk(D_KV), stk(D_KV),
                   row(D_QI, BF16), row(LANES, F32), stk(IDX_DIM), row(2 * D_MODEL, BF16),
                   jax.ShapeDtypeStruct((b, CONV_W - 1, D_CONV), F32)],
        scratch_shapes=[pltpu.VMEM((CONV_W - 1, D_CONV), F32)],
        input_output_aliases=aliases,
        compiler_params=_cparams(2), name="inproj",
    )(*args)


def _pair_blockdiag(tile, low_half):
    lane = lax.broadcasted_iota(I32, tile.shape, 1)
    lo = lane < HEAD_DIM
    swapped = pltpu.roll(tile, HEAD_DIM, 1)
    top = jnp.where(lo, tile if low_half else swapped, 0.0)
    bot = jnp.where(lo, 0.0, swapped if low_half else tile)
    return jnp.concatenate([top, bot], axis=0).astype(BF16)


def _attn_kernel(q_ref, qi_ref, kiw_ref, k_ref, v_ref, ki_ref, o_ref,
                 kbd_scr, vbd_scr, kibd_scr, key_scr, hi_scr, lo_scr, s_scr, m_scr, acc_scr,
                 *, tq, pos0, l_valid, topk):
    kb = KEY_BLOCK
    j = pl.program_id(1)
    nkc_all = key_scr.shape[0]
    npair = N_IDX_HEADS // 2
    nt = (((1,), (1,)), ((), ()))

    @pl.when(j == 0)
    def _():
        r_i = lax.broadcasted_iota(I32, (2 * kb, LANES), 0)
        c_i = lax.broadcasted_iota(I32, (2 * kb, LANES), 1)
        ones_bd = jnp.where((r_i < kb) == (c_i < HEAD_DIM), 1.0, 0.0).astype(BF16)

        def build(c, carry):
            rows = pl.ds(pl.multiple_of(c * kb, kb), kb)
            kibd_scr[c] = _pair_blockdiag(ki_ref[0, rows, :], True)
            for g in range(N_KV_HEADS):
                t = g // 2
                kbd_scr[g, c] = _pair_blockdiag(k_ref[0, 0, rows, t * LANES:(t + 1) * LANES], g % 2 == 0)
                vbd_scr[g, c, :, 0:LANES] = _pair_blockdiag(v_ref[0, 0, rows, t * LANES:(t + 1) * LANES],
                                                            g % 2 == 0)
                vbd_scr[g, c, :, LANES:2 * LANES] = ones_bd
            return carry

        lax.fori_loop(0, nkc_all, build, 0)

    w_t = kiw_ref[0].T

    row0 = pos0 + j * tq
    rows = row0 + lax.broadcasted_iota(I32, (1, tq), 1)
    lim = jnp.minimum(((rows >> 6) + 1) << 6, l_valid)
    last = row0 + tq - 1
    lim_max = jnp.minimum(((last >> 6) + 1) << 6, l_valid)
    nkc = (lim_max + kb - 1) >> 8

    def phase_a(c, carry):
        kbd = kibd_scr[c]
        acc = jnp.zeros((kb, tq), F32)
        for t in range(npair):
            d2 = lax.dot_general(kbd, qi_ref[0, :, t * LANES:(t + 1) * LANES], nt,
                                 preferred_element_type=F32)
            r = IDX_DIM + 2 * t
            acc = acc + w_t[r:r + 1, :] * jnp.maximum(d2[0:kb], 0.0)
            acc = acc + w_t[r + 1:r + 2, :] * jnp.maximum(d2[kb:2 * kb], 0.0)
        col = c * kb + lax.broadcasted_iota(I32, (kb, tq), 0)
        sc = jnp.where(col < lim, acc, -jnp.inf)
        bits = lax.bitcast_convert_type(sc, I32)
        key = bits ^ ((bits >> 31) & 0x7FFFFFFF)
        key_scr[c] = key
        hi_scr[c] = (key >> 16).astype(I16)
        lo_scr[c] = ((key & 0xFFFF) - 32768).astype(I16)
        return carry

    lax.fori_loop(0, nkc, phase_a, 0)

    def count_ge(cand):
        def body(c, cnt):
            m = jnp.where(key_scr[c] >= cand, 1.0, 0.0)
            parts = [m[8 * i:8 * (i + 1)] for i in range(kb // 8)]
            while len(parts) > 1:
                parts = [parts[i] + parts[i + 1] for i in range(0, len(parts), 2)]
            return cnt + parts[0]
        cnt = lax.fori_loop(0, nkc, body, jnp.zeros((8, tq), F32))
        return jnp.sum(cnt, axis=0, keepdims=True)

    def count16(ref, pred):
        def body(c, cnt):
            m = jnp.where(pred(ref[c]), jnp.int16(1), jnp.int16(0))
            parts = [m[16 * i:16 * (i + 1)] for i in range(kb // 16)]
            while len(parts) > 1:
                parts = [parts[i] + parts[i + 1] for i in range(0, len(parts), 2)]
            return cnt + parts[0]
        cnt = lax.fori_loop(0, nkc, body, jnp.zeros((16, tq), I16))
        return jnp.sum(cnt.astype(I32), axis=0, keepdims=True)

    def radix16(ref, need):
        t0 = jnp.where(count16(ref, lambda h: h >= jnp.int16(0)) >= need, 0, -32768).astype(I32)

        def bit_body(i, t):
            cand = t | lax.shift_left(jnp.int32(1), 14 - i)
            c16 = cand.astype(I16)
            return jnp.where(count16(ref, lambda h: h >= c16) >= need, cand, t)

        return lax.fori_loop(0, 15, bit_body, t0)

    t_hi = radix16(hi_scr, topk)
    h16 = t_hi.astype(I16)
    need_lo = topk - count16(hi_scr, lambda h: h > h16)

    def keep_bucket(c, carry):
        lo_scr[c] = jnp.where(hi_scr[c] == h16, lo_scr[c], jnp.int16(-32768))
        return carry

    lax.fori_loop(0, nkc, keep_bucket, 0)
    t_lo = radix16(lo_scr, need_lo)
    thr = (t_hi << 16) | ((t_lo + 32768) & 0xFFFF)
    kf = float(topk)
    few = lim <= topk
    thr = jnp.where(few, KEY_MIN_FINITE, thr)

    tie = jnp.logical_and(count_ge(thr) > kf, jnp.logical_not(few))
    any_tie = jnp.max(jnp.where(tie, 1.0, 0.0)) > 0.0

    @pl.when(any_tie)
    def _():
        need = kf - count_ge(thr + 1)
        r_i = lax.broadcasted_iota(I32, (kb, kb), 0)
        c_i = lax.broadcasted_iota(I32, (kb, kb), 1)
        tri = jnp.where(c_i < r_i, 1.0, 0.0).astype(BF16)

        def body(c, run):
            kc = key_scr[c]
            eqf = jnp.where(kc == thr, 1.0, 0.0)
            rank = run + jnp.dot(tri, eqf.astype(BF16), preferred_element_type=F32)
            drop = jnp.logical_and(jnp.logical_and(tie, kc == thr), rank >= need)
            key_scr[c] = jnp.where(drop, KEY_NEG_INF, kc)
            return run + jnp.sum(eqf, axis=0, keepdims=True)

        lax.fori_loop(0, nkc, body, jnp.zeros((1, tq), F32))

    nl = kb // LANES
    m_scr[...] = jnp.full(m_scr.shape, MASK_BIAS, F32)
    acc_scr[...] = jnp.zeros(acc_scr.shape, F32)

    def sweep1(c, carry):
        bias = jnp.where(key_scr[c] >= thr, 0.0, MASK_BIAS).T
        for g in range(N_KV_HEADS):
            s2 = lax.dot_general(q_ref[0, :, g * LANES:(g + 1) * LANES], kbd_scr[g, c], nt,
                                 preferred_element_type=F32)
            for r in range(GROUP):
                h = GROUP * g + r
                s = s2[:, r * kb:(r + 1) * kb] + bias
                s_scr[h, c] = s
                mx = m_scr[h]
                for i in range(nl):
                    mx = jnp.maximum(mx, s[:, i * LANES:(i + 1) * LANES])
                m_scr[h] = mx
        return carry

    lax.fori_loop(0, nkc, sweep1, 0)
    for h in range(N_HEADS):
        m_scr[h] = jnp.broadcast_to(jnp.max(m_scr[h], axis=1, keepdims=True), (tq, LANES))

    def sweep2(c, carry):
        for g in range(N_KV_HEADS):
            p2 = []
            for r in range(GROUP):
                h = GROUP * g + r
                s = s_scr[h, c]
                mb = m_scr[h]
                p2 += [jnp.exp2(s[:, i * LANES:(i + 1) * LANES] - mb) for i in range(nl)]
            pb = jnp.concatenate(p2, axis=1).astype(BF16)
            acc_scr[g] += jnp.dot(pb, vbd_scr[g, c], preferred_element_type=F32)
        return carry

    lax.fori_loop(0, nkc, sweep2, 0)
    for g in range(N_KV_HEADS):
        o_ref[0, :, g * LANES:(g + 1) * LANES] = (acc_scr[g, :, 0:LANES]
                                                  / acc_scr[g, :, LANES:2 * LANES]).astype(BF16)


def _attention(q, qi, kiw, k_all, v_all, ki_all, *, layer, tq, pos0, l_valid, topk):
    b, s, _ = q.shape
    l_pad = k_all.shape[2]
    nkc_max = l_pad // KEY_BLOCK
    qspec = lambda w: pl.BlockSpec((1, tq, w), lambda bi, ji: (bi, ji, 0))
    kspec = lambda w: pl.BlockSpec((1, l_pad, w), lambda bi, ji: (bi, 0, 0))
    lspec = lambda w: pl.BlockSpec((1, 1, l_pad, w), lambda bi, ji: (layer, bi, 0, 0))
    return pl.pallas_call(
        functools.partial(_attn_kernel, tq=tq, pos0=pos0, l_valid=l_valid, topk=topk),
        grid=(b, s // tq),
        in_specs=[qspec(D_Q), qspec(D_QI), qspec(LANES), lspec(D_KV), lspec(D_KV), kspec(LANES)],
        out_specs=qspec(D_Q),
        out_shape=jax.ShapeDtypeStruct((b, s, D_Q), BF16),
        scratch_shapes=[pltpu.VMEM((N_KV_HEADS, nkc_max, 2 * KEY_BLOCK, LANES), BF16),
                        pltpu.VMEM((N_KV_HEADS, nkc_max, 2 * KEY_BLOCK, 2 * LANES), BF16),
                        pltpu.VMEM((nkc_max, 2 * KEY_BLOCK, LANES), BF16),
                        pltpu.VMEM((nkc_max, KEY_BLOCK, tq), I32),
                        pltpu.VMEM((nkc_max, KEY_BLOCK, tq), I16),
                        pltpu.VMEM((nkc_max, KEY_BLOCK, tq), I16),
                        pltpu.VMEM((N_HEADS, nkc_max, tq, KEY_BLOCK), F32),
                        pltpu.VMEM((N_HEADS, tq, LANES), F32),
                        pltpu.VMEM((N_KV_HEADS, tq, 2 * LANES), F32)],
        compiler_params=_cparams(2), name="attention",
    )(q, qi, kiw, k_all, v_all, ki_all)


def _split3(a):
    hi = a.astype(BF16)
    r = a - hi.astype(F32)
    mid = r.astype(BF16)
    lo = (r - mid.astype(F32)).astype(BF16)
    return hi, mid, lo


def _merge_kernel(*refs, with_router):
    if with_router:
        (x_ref, ya_ref, ao_ref, sg_ref, mod_ref, wc_ref, wa_ref, wo_ref, g2_ref, rw_ref, rb_ref,
         x1_ref, h2_ref, lg_ref) = refs
    else:
        (x_ref, ya_ref, ao_ref, sg_ref, mod_ref, wc_ref, wa_ref, wo_ref, g2_ref,
         x1_ref, h2_ref) = refs
    a = jnp.dot(ya_ref[0], wc_ref[...], preferred_element_type=F32)
    b = jnp.dot(ao_ref[0], wa_ref[...], preferred_element_type=F32)
    merged = sg_ref[0, :, 0:D_MODEL].astype(F32) * a + sg_ref[0, :, D_MODEL:2 * D_MODEL].astype(F32) * b
    mix = jnp.dot(merged.astype(BF16), wo_ref[...], preferred_element_type=F32)
    x1 = x_ref[0] + mod_ref[0, 2] * mix
    x1_ref[0] = x1
    h2 = _rmsnorm(x1, g2_ref[...]) * (1.0 + mod_ref[0, 4]) + mod_ref[0, 3]
    h2_ref[0] = h2.astype(h2_ref.dtype)
    if with_router:
        hs = _split3(h2)
        acc = jnp.zeros((h2.shape[0], LANES), F32)
        for i, jj in ((0, 0), (0, 1), (1, 0)):
            acc = acc + jnp.dot(hs[i], rw_ref[jj], preferred_element_type=F32)
        lane = lax.broadcasted_iota(I32, acc.shape, 1)
        lg = jnp.where(lane < N_EXPERTS, acc + rb_ref[...], -jnp.inf)
        m1 = jnp.max(lg, axis=1, keepdims=True)
        i1 = jnp.min(jnp.where(lg == m1, lane, LANES), axis=1, keepdims=True)
        rest = jnp.where(lane == i1, -jnp.inf, lg)
        m2 = jnp.max(rest, axis=1, keepdims=True)
        i2 = jnp.min(jnp.where(rest == m2, lane, LANES), axis=1, keepdims=True)
        e2 = jnp.exp(m2 - m1)
        den = 1.0 + e2
        lg_ref[0] = jnp.where(lane == 0, i1.astype(F32),
                              jnp.where(lane == 1, i2.astype(F32),
                                        jnp.where(lane == 2, 1.0 / den,
                                                  jnp.where(lane == 3, e2 / den, 0.0))))


def _merge(x, ya, ao, sg, mod, wc, wa, wo, g2, router=None, *, tm):
    b, s, d = x.shape
    r = mod.shape[2]
    rspec = lambda w: pl.BlockSpec((1, tm, w), lambda bi, ti: (bi, ti, 0))
    mspec = (pl.BlockSpec((1, 6, 1, d), lambda bi, ti: (bi, 0, 0, 0)) if r == 1 else
             pl.BlockSpec((1, 6, tm, d), lambda bi, ti: (bi, 0, ti, 0)))
    in_specs = [rspec(d), rspec(D_CONV), rspec(D_Q), rspec(2 * d), mspec,
                _resident((D_CONV, d)), _resident((D_Q, d)), _resident((d, d)), _resident((1, d))]
    args = [x, ya, ao, sg, mod, wc, wa, wo, g2]
    out_specs = [rspec(d), rspec(d)]
    out_shape = [jax.ShapeDtypeStruct((b, s, d), F32),
                 jax.ShapeDtypeStruct((b, s, d), BF16 if router is None else F32)]
    if router is not None:
        in_specs += [_resident((3, d, LANES)), _resident((1, LANES))]
        args += list(router)
        out_specs.append(rspec(LANES))
        out_shape.append(jax.ShapeDtypeStruct((b, s, LANES), F32))
    return pl.pallas_call(
        functools.partial(_merge_kernel, with_router=router is not None),
        grid=(b, s // tm),
        in_specs=in_specs, out_specs=out_specs, out_shape=out_shape,
        compiler_params=_cparams(2), name="merge",
    )(*args)


def _ffn_kernel(h2_ref, x1_ref, mod_ref, w1_ref, w3_ref, w2_ref, o_ref, acc_scr, *, nc):
    h = h2_ref[0]
    acc_scr[...] = _swiglu_chunk(h, w1_ref[0], w3_ref[0], w2_ref[0])

    def body(c, carry):
        acc_scr[...] += _swiglu_chunk(h, w1_ref[c], w3_ref[c], w2_ref[c])
        return carry

    lax.fori_loop(1, nc, body, 0)
    o_ref[0] = x1_ref[0] + mod_ref[0, 5] * acc_scr[...]


def _ffn(h2, x1, mod, w1, w3, w2, *, tm):
    b, s, d = x1.shape
    nc = w1.shape[0]
    r = mod.shape[2]
    rspec = lambda w: pl.BlockSpec((1, tm, w), lambda bi, ti: (bi, ti, 0))
    mspec = (pl.BlockSpec((1, 6, 1, d), lambda bi, ti: (bi, 0, 0, 0)) if r == 1 else
             pl.BlockSpec((1, 6, tm, d), lambda bi, ti: (bi, 0, ti, 0)))
    return pl.pallas_call(
        functools.partial(_ffn_kernel, nc=nc),
        grid=(b, s // tm),
        in_specs=[rspec(d), rspec(d), mspec, _resident(w1.shape), _resident(w3.shape), _resident(w2.shape)],
        out_specs=rspec(d),
        out_shape=jax.ShapeDtypeStruct((b, s, d), F32),
        scratch_shapes=[pltpu.VMEM((tm, d), F32)],
        compiler_params=_cparams(2), name="ffn",
    )(h2, x1, mod, w1, w3, w2)


def _swiglu_chunk(xb, w1, w3, w2):
    a = jnp.dot(xb, w1, preferred_element_type=F32)
    hm = (a * _sigmoid(a) * jnp.dot(xb, w3, preferred_element_type=F32)).astype(BF16)
    return jnp.dot(hm, w2, preferred_element_type=F32)


def _moe_dense_kernel(h2_ref, x1_ref, rt_ref, mod_ref, w1_ref, w3_ref, w2_ref, fg_ref, o_ref,
                      eacc_scr, acc_scr, *, final_norm):
    e = pl.program_id(2)
    c = pl.program_id(3)
    ne = pl.num_programs(2)
    nc = pl.num_programs(3)

    @pl.when(jnp.logical_and(e == 0, c == 0))
    def _():
        acc_scr[...] = jnp.zeros(acc_scr.shape, F32)

    @pl.when(c == 0)
    def _():
        eacc_scr[...] = jnp.zeros(eacc_scr.shape, F32)

    eacc_scr[...] += _swiglu_chunk(h2_ref[0].astype(BF16), w1_ref[0], w3_ref[0], w2_ref[0])

    @pl.when(c == nc - 1)
    def _():
        rt = rt_ref[0]
        ef = e.astype(F32)
        gate = (jnp.where(rt[:, 0:1] == ef, rt[:, 2:3], 0.0)
                + jnp.where(rt[:, 1:2] == ef, rt[:, 3:4], 0.0))
        acc_scr[...] += gate * eacc_scr[...]

    @pl.when(jnp.logical_and(e == ne - 1, c == nc - 1))
    def _():
        x2 = x1_ref[0] + mod_ref[0, 5] * acc_scr[...]
        o_ref[0] = _rmsnorm(x2, fg_ref[...]) if final_norm else x2


def _moe_dense(h2, x1, route, mod, w1, w3, w2, fg, *, tm, final_norm):
    b, s, d = x1.shape
    ne, _, f = w1.shape
    fc = MOE_CHUNK
    r = mod.shape[2]
    rspec = lambda w: pl.BlockSpec((1, tm, w), lambda bi, ti, ei, ci: (bi, ti, 0))
    mspec = (pl.BlockSpec((1, 6, 1, d), lambda bi, ti, ei, ci: (bi, 0, 0, 0)) if r == 1 else
             pl.BlockSpec((1, 6, tm, d), lambda bi, ti, ei, ci: (bi, 0, ti, 0)))
    wcol = pl.BlockSpec((1, d, fc), lambda bi, ti, ei, ci: (ei, 0, ci))
    return pl.pallas_call(
        functools.partial(_moe_dense_kernel, final_norm=final_norm),
        grid=(b, s // tm, ne, f // fc),
        in_specs=[rspec(d), rspec(d), rspec(LANES), mspec, wcol, wcol,
                  pl.BlockSpec((1, fc, d), lambda bi, ti, ei, ci: (ei, ci, 0)),
                  _resident((1, d))],
        out_specs=rspec(d),
        out_shape=jax.ShapeDtypeStruct((b, s, d), F32),
        scratch_shapes=[pltpu.VMEM((tm, d), F32), pltpu.VMEM((tm, d), F32)],
        compiler_params=_cparams(4), name="moe_dense",
    )(h2, x1, route, mod, w1, w3, w2, fg)


def _route_tables(route, tmr):
    n = route.shape[0]
    e1 = route[:, 0].astype(I32)
    e2 = route[:, 1].astype(I32)
    ar = jnp.arange(N_EXPERTS, dtype=I32)
    oh1 = e1[:, None] == ar
    oh2 = e2[:, None] == ar
    oh = oh1.astype(I32) + oh2.astype(I32)
    cum = jnp.cumsum(oh, axis=0)
    excl = cum - oh
    cnt = cum[-1]
    tiles_e = (cnt + tmr - 1) // tmr
    tile_end = jnp.cumsum(tiles_e)
    tile_start = tile_end - tiles_e
    row_off = tile_start * tmr
    d1 = row_off[e1] + jnp.sum(jnp.where(oh1, excl, 0), axis=1)
    d2 = row_off[e2] + jnp.sum(jnp.where(oh2, excl, 0), axis=1)
    nt = -(-2 * n // tmr) + N_EXPERTS
    tok = jnp.arange(n, dtype=I32)
    pair = jnp.zeros((nt * tmr,), I32).at[jnp.concatenate([d1, d2])].set(
        jnp.concatenate([2 * tok, 2 * tok + 1]), unique_indices=True)
    src = pair >> 1
    dst = (pair & 1) * n + src
    tile = jnp.arange(nt, dtype=I32)
    te = jnp.minimum(jnp.searchsorted(tile_end, tile, side="right"), N_EXPERTS - 1).astype(I32)
    nv = jnp.where(tile < tile_end[-1], jnp.clip(cnt[te] - (tile - tile_start[te]) * tmr, 0, tmr), 0)
    return (src.reshape(nt, 1, tmr), dst.reshape(nt, 1, tmr), te, nv.astype(I32),
            tile_end[-1:].astype(I32))


def _moe_routed_kernel(te_ref, nv_ref, nu_ref, src0_ref, srcn_ref, dstp_ref, h2_hbm, w1_ref, w3_ref, w2_ref,
                       out_hbm, xbuf, xb_scr, ybuf, gsem, ssem, *, tmr, nc, gs):
    i = pl.program_id(0)
    c = pl.program_id(1)
    nu = nu_ref[0]
    slot = i % 2
    other = 1 - slot
    rpc = tmr // nc
    rpg = tmr // gs

    def gather(src_ref, cc, k, sl):
        return pltpu.make_async_copy(h2_hbm.at[pl.ds(src_ref[0, 0, cc * rpg + k], 1)],
                                     xbuf.at[sl, cc, pl.ds(k, 1)], gsem.at[sl])

    def scatter(cc, k, sl):
        return pltpu.make_async_copy(ybuf.at[sl, cc, pl.ds(k, 1)],
                                     out_hbm.at[pl.ds(dstp_ref[0, 0, cc * rpc + k], 1)], ssem.at[sl])

    @pl.when(jnp.logical_and(i == 0, c == 0))
    def _():
        def body(cc, carry):
            for k in range(rpg):
                gather(src0_ref, cc, k, 0).start()
            return carry
        lax.fori_loop(0, gs, body, 0)

    @pl.when(jnp.logical_and(c == 0, i < nu))
    def _():
        pltpu.make_async_copy(xbuf.at[slot], xbuf.at[slot], gsem.at[slot]).wait()
        xb_scr[...] = xbuf[slot].reshape(tmr, xb_scr.shape[1]).astype(BF16)

    last = nv_ref.shape[0] - 1
    nv_drain = nv_ref[jnp.clip(i - 2, 0, last)]
    push = jnp.logical_and(i >= 1, i - 1 < nu)
    nv_push = nv_ref[jnp.clip(i - 1, 0, last)]
    fast = jnp.logical_and(i >= 1, jnp.logical_and(i + 1 < nu, nv_push == tmr))
    slow = jnp.logical_not(fast)
    drain = jnp.logical_and(c == 0, jnp.logical_and(i >= 2, i - 2 < nu))

    @pl.when(jnp.logical_and(drain, nv_drain == tmr))
    def _():
        pltpu.make_async_copy(ybuf.at[slot], ybuf.at[slot], ssem.at[slot]).wait()

    @pl.when(jnp.logical_and(drain, nv_drain < tmr))
    def _():
        def body(r, carry):
            scatter(0, 0, slot).wait()
            return carry
        lax.fori_loop(0, nv_drain, body, 0)

    def fast_step(with_gather):
        xb = xb_scr[...]
        a = jnp.dot(xb, w1_ref[0], preferred_element_type=F32)
        b = jnp.dot(xb, w3_ref[0], preferred_element_type=F32)
        if with_gather:
            for k in range(rpg):
                gather(srcn_ref, c, k, other).start()
        for k in range(rpc):
            scatter(c, k, other).start()
        hm = (a * _sigmoid(a) * b).astype(BF16)
        y = jnp.dot(hm, w2_ref[0], preferred_element_type=F32).reshape(nc, rpc, xb_scr.shape[1])
        ybuf[slot] = jnp.where(c == 0, y, ybuf[slot] + y)

    @pl.when(jnp.logical_and(fast, c < gs))
    def _():
        fast_step(True)

    @pl.when(jnp.logical_and(fast, c >= gs))
    def _():
        fast_step(False)

    @pl.when(jnp.logical_and(slow, jnp.logical_and(i + 1 < nu, c < gs)))
    def _():
        for k in range(rpg):
            gather(srcn_ref, c, k, other).start()

    @pl.when(jnp.logical_and(slow, jnp.logical_and(push, nv_push == tmr)))
    def _():
        for k in range(rpc):
            scatter(c, k, other).start()

    @pl.when(jnp.logical_and(slow, jnp.logical_and(push, nv_push < tmr)))
    def _():
        def body(k, carry):
            scatter(c, k, other).start()
            return carry
        lax.fori_loop(0, jnp.clip(nv_push - c * rpc, 0, rpc), body, 0)

    @pl.when(jnp.logical_and(slow, i < nu))
    def _():
        y = _swiglu_chunk(xb_scr[...], w1_ref[0], w3_ref[0], w2_ref[0]).reshape(nc, rpc, xb_scr.shape[1])

        @pl.when(c == 0)
        def _():
            ybuf[slot] = y

        @pl.when(c > 0)
        def _():
            ybuf[slot] += y


def _moe_routed(h2, route, w1, w3, w2, *, tmr):
    n, d = h2.shape
    f = w1.shape[2]
    fc = MOE_CHUNK
    nc = f // fc
    gs = MOE_GATHER_STEPS
    src, dst, te, nv, nu = _route_tables(route, tmr)
    nt = src.shape[0]
    last = nt - 1
    smem = lambda imap: pl.BlockSpec((1, 1, tmr), imap, memory_space=pltpu.SMEM)
    chunk = lambda i, c, nu_r: jnp.where(i < nu_r[0], c, nc - 1)
    wcol = pl.BlockSpec((1, d, fc), lambda i, c, te_r, nv_r, nu_r: (te_r[jnp.minimum(i, last)], 0, chunk(i, c, nu_r)))
    wrow = pl.BlockSpec((1, fc, d), lambda i, c, te_r, nv_r, nu_r: (te_r[jnp.minimum(i, last)], chunk(i, c, nu_r), 0))
    return pl.pallas_call(
        functools.partial(_moe_routed_kernel, tmr=tmr, nc=nc, gs=gs),
        grid_spec=pltpu.PrefetchScalarGridSpec(
            num_scalar_prefetch=3,
            grid=(nt + 2, nc),
            in_specs=[smem(lambda i, c, te_r, nv_r, nu_r: (0, 0, 0)),
                      smem(lambda i, c, te_r, nv_r, nu_r: (jnp.minimum(i + 1, last), 0, 0)),
                      smem(lambda i, c, te_r, nv_r, nu_r: (jnp.clip(i - 1, 0, last), 0, 0)),
                      pl.BlockSpec(memory_space=pl.ANY), wcol, wcol, wrow],
            out_specs=pl.BlockSpec(memory_space=pl.ANY),
            scratch_shapes=[pltpu.VMEM((2, gs, tmr // gs, d), F32), pltpu.VMEM((tmr, d), BF16),
                            pltpu.VMEM((2, nc, tmr // nc, d), F32),
                            pltpu.SemaphoreType.DMA((2,)), pltpu.SemaphoreType.DMA((2,))]),
        out_shape=jax.ShapeDtypeStruct((2 * n, d), F32),
        compiler_params=_cparams(2), name="moe_routed",
    )(te, nv, nu, src, src, dst, h2, w1, w3, w2)


def _moe_combine_kernel(x1_ref, ya_ref, yb_ref, rt_ref, mod_ref, fg_ref, o_ref, *, final_norm):
    rt = rt_ref[0]
    f = rt[:, 2:3] * ya_ref[...] + rt[:, 3:4] * yb_ref[...]
    x2 = x1_ref[0] + mod_ref[0, 5] * f
    o_ref[0] = _rmsnorm(x2, fg_ref[...]) if final_norm else x2


def _moe_combine(x1, y2, route, mod, fg, *, tm, final_norm):
    b, s, d = x1.shape
    nt = s // tm
    rspec = lambda w: pl.BlockSpec((1, tm, w), lambda bi, ti: (bi, ti, 0))
    return pl.pallas_call(
        functools.partial(_moe_combine_kernel, final_norm=final_norm),
        grid=(b, nt),
        in_specs=[rspec(d),
                  pl.BlockSpec((tm, d), lambda bi, ti: (bi * nt + ti, 0)),
                  pl.BlockSpec((tm, d), lambda bi, ti: (b * nt + bi * nt + ti, 0)),
                  rspec(LANES),
                  pl.BlockSpec((1, 6, 1, d), lambda bi, ti: (bi, 0, 0, 0)), _resident((1, d))],
        out_specs=rspec(d),
        out_shape=jax.ShapeDtypeStruct((b, s, d), F32),
        compiler_params=_cparams(2), name="moe_combine",
    )(x1, y2, y2, route, mod, fg)


def _pack_w_in(w):
    cut = C_KIW + IDX_DIM + N_IDX_HEADS
    pad = jnp.zeros((w.shape[0], C_GATE - cut), w.dtype)
    return jnp.concatenate([w[:, :cut], pad, w[:, cut:]], axis=1).astype(BF16)


def _chunk_cols(w, fc):
    *lead, d, f = w.shape
    return jnp.moveaxis(w.astype(BF16).reshape(*lead, d, f // fc, fc), -2, -3)


def _chunk_w2(w2, fc):
    *lead, f, d = w2.shape
    return w2.reshape(*lead, f // fc, fc, d).astype(BF16)


def _router_terms(rw, rb):
    d = rw.shape[0]
    rwp = jnp.zeros((d, LANES), F32).at[:, :N_EXPERTS].set(rw)
    hi, mid, lo = _split3(rwp)
    rbp = jnp.zeros((1, LANES), F32).at[0, :N_EXPERTS].set(rb)
    return jnp.stack([hi, mid, lo]), rbp


def _trunk(x, mod_all, caches, p, *, tm, tq):
    b, t, d = x.shape
    depth = mod_all.shape[0]
    flat = caches is not None
    stacks, tails = None, []
    for l in range(depth):
        mod = mod_all[l].reshape(b, 6, 1, d)
        if caches is None:
            conv_state = jnp.zeros((b, CONV_W - 1, D_CONV), F32)
        else:
            conv_state = caches[3][l]
        ya, q, k_st, v_st, qi, kiw, ki_st, sg, tail = _inproj(
            x, mod, p["norm1_g"][l], p["w_in"][l], p["conv_w"][l], conv_state, tm,
            layer=l, depth=depth, stacks=stacks)
        stacks = (k_st, v_st, ki_st)
        if caches is None:
            ao = _attention(q, qi, kiw, k_st, v_st, kiw, layer=l, tq=tq, pos0=0, l_valid=t,
                            topk=min(TOPK_MAX, t // 4))
        else:
            past = caches[0][l].shape[1]
            l_valid = past + t
            l_pad = -(-l_valid // KEY_BLOCK) * KEY_BLOCK
            cat = lambda old, new: jnp.concatenate(
                [old.reshape(b, past, -1), new, jnp.zeros((b, l_pad - l_valid, new.shape[-1]), F32)], axis=1)
            k_all, v_all = cat(caches[0][l], k_st[l])[None], cat(caches[1][l], v_st[l])[None]
            ki_all = jnp.pad(cat(caches[2][l], ki_st[l]), ((0, 0), (0, 0), (0, LANES - IDX_DIM)))
            padq = lambda a: jnp.pad(a, ((0, 0), (0, tq - t), (0, 0)))
            ao = _attention(padq(q), padq(qi), padq(kiw), k_all, v_all, ki_all, layer=0, tq=tq, pos0=past,
                            l_valid=l_valid, topk=min(TOPK_MAX, l_valid // 4))[:, :t]
        is_moe = l % 2 == 1
        router = (p["router_w"][l // 2], p["router_b"][l // 2]) if is_moe else None
        if flat:
            n = b * t
            fl = lambda a: a.reshape(1, n, a.shape[-1])
            modf = jnp.repeat(mod_all[l].reshape(b, 6, d), t, axis=0).transpose(1, 0, 2)[None]
            xs, yas, aos, sgs, tmf = fl(x), fl(ya), fl(ao), fl(sg), n
        else:
            modf, xs, yas, aos, sgs, tmf = mod, x, ya, ao, sg, tm
        outs = _merge(xs, yas, aos, sgs, modf, p["w_conv_out"][l], p["w_attn_out"][l], p["w_o"][l],
                      p["norm2_g"][l], router, tm=tmf)
        if is_moe and flat:
            x1, h2, route = outs
            x = _moe_dense(h2, x1, route, modf, p["moe_w1"][l // 2], p["moe_w3"][l // 2],
                           p["moe_w2"][l // 2], p["final_g"], tm=xs.shape[1], final_norm=l == depth - 1)
        elif is_moe:
            x1, h2, route = outs
            y2 = _moe_routed(h2.reshape(b * t, d), route.reshape(b * t, LANES), p["moe_w1"][l // 2],
                             p["moe_w3"][l // 2], p["moe_w2"][l // 2], tmr=MOE_ROWS)
            x = _moe_combine(x1, y2, route, mod, p["final_g"], tm=tm,
                             final_norm=l == depth - 1)
        else:
            x1, h2 = outs
            x = _ffn(h2, x1, modf, p["ffn_w1"][l // 2], p["ffn_w3"][l // 2], p["ffn_w2"][l // 2],
                     tm=tmf if flat else min(2 * tm, t))
        x = x.reshape(b, t, d)
        tails.append(tail)
    k_st, v_st, ki_st = stacks
    heads = (depth, b, t, N_KV_HEADS, HEAD_DIM)
    return x, k_st.reshape(heads), v_st.reshape(heads), ki_st, jnp.stack(tails)


def kernel(x_prompt, x_sample, c_prompt, c_sample, cache_k, cache_v, cache_idx_k, state_conv, w_ada, b_ada, norm1_g, w_in, conv_w, w_conv_out, w_attn_out, w_o, norm2_g, ffn_w1, ffn_w3, ffn_w2, router_w, router_b, moe_w1, moe_w3, moe_w2, final_g):
    depth = w_in.shape[0]
    bp = x_prompt.shape[0]
    rterms = [_router_terms(router_w[i], router_b[i]) for i in range(router_w.shape[0])]
    p = {
        "norm1_g": norm1_g.reshape(depth, 1, -1),
        "norm2_g": norm2_g.reshape(depth, 1, -1),
        "final_g": final_g.reshape(1, -1),
        "w_in": jnp.stack([_pack_w_in(w_in[l]) for l in range(depth)]),
        "conv_w": conv_w,
        "w_conv_out": w_conv_out.astype(BF16),
        "w_attn_out": w_attn_out.astype(BF16),
        "w_o": w_o.astype(BF16),
        "ffn_w1": _chunk_cols(ffn_w1, FFN_CHUNK),
        "ffn_w3": _chunk_cols(ffn_w3, FFN_CHUNK),
        "ffn_w2": _chunk_w2(ffn_w2, FFN_CHUNK),
        "moe_w1": moe_w1.astype(BF16),
        "moe_w3": moe_w3.astype(BF16),
        "moe_w2": moe_w2.astype(BF16),
        "router_w": [r[0] for r in rterms],
        "router_b": [r[1] for r in rterms],
    }
    mod_all = _adaln(jnp.concatenate([c_prompt, c_sample], axis=0), w_ada.astype(BF16), b_ada)
    seq = x_prompt.shape[1]
    tm = min(512, seq)
    tq = min(256, seq)
    yp, kp, vp, kip, cp = _trunk(x_prompt, mod_all[:, :bp], None, p, tm=tm, tq=tq)
    ts = x_sample.shape[1]
    ys, ksm, vsm, kism, csm = _trunk(x_sample, mod_all[:, bp:], (cache_k, cache_v, cache_idx_k, state_conv), p,
                                     tm=ts, tq=-(-ts // LANES) * LANES)
    return (yp, ys, kp, vp, kip, cp, ksm, vsm, kism, csm)
```

```python
import functools

import jax
import jax.numpy as jnp
from jax import lax
from jax.experimental import pallas as pl
from jax.experimental.pallas import tpu as pltpu

F32 = jnp.float32
BF16 = jnp.bfloat16
I32 = jnp.int32
I16 = jnp.int16

D_MODEL = 1024
D_CONV = 512
CONV_W = 3
N_HEADS = 8
N_KV_HEADS = 4
HEAD_DIM = 64
GROUP = N_HEADS // N_KV_HEADS
N_IDX_HEADS = 16
IDX_DIM = 64
TOPK_MAX = 256
CHUNK = 64
D_FF = 2816
N_EXPERTS = 8
D_FF_EXPERT = 3584
EPS = 1e-6

D_Q = N_HEADS * HEAD_DIM
D_KV = N_KV_HEADS * HEAD_DIM
D_QI = N_IDX_HEADS * IDX_DIM
LANES = 128
C_CONV = 0
C_Q = 3 * D_CONV
C_KV = C_Q + D_Q
C_QI = C_KV + 2 * D_KV
C_KIW = C_QI + D_QI
C_GATE = C_KIW + LANES
D_PACK = C_GATE + 2 * D_MODEL
WI_SCALE = (IDX_DIM ** -0.5) * (N_IDX_HEADS ** -0.5)
Q_SCALE = HEAD_DIM ** -0.5 * 1.4426950408889634

KEY_BLOCK = 256
FFN_CHUNK = 256
MOE_CHUNK = 512
MOE_ROWS = (D_FF_EXPERT // MOE_CHUNK) * 128
MOE_GATHER_STEPS = D_FF_EXPERT // MOE_CHUNK
V7X_VMEM_LIMIT = 56 * 1024 * 1024

KEY_NEG_INF = -2139095041
KEY_MIN_FINITE = KEY_NEG_INF + 1
MASK_BIAS = -1e30


def _sigmoid(x):
    return 1.0 / (1.0 + jnp.exp(-x))


def _rmsnorm(x, g):
    ms = jnp.mean(x * x, axis=-1, keepdims=True)
    return x * lax.rsqrt(ms + EPS) * g


def _cparams(n_axes):
    return pltpu.CompilerParams(dimension_semantics=("arbitrary",) * n_axes,
                                vmem_limit_bytes=V7X_VMEM_LIMIT)


def _resident(shape):
    nd = len(shape)
    return pl.BlockSpec(shape, lambda *_: (0,) * nd, pipeline_mode=pl.Buffered(1))


def _adaln_kernel(c_ref, w_ref, b_ref, o_ref):
    c = c_ref[...]
    cond = (c * _sigmoid(c)).astype(BF16)
    o_ref[0] = jnp.dot(cond, w_ref[0], preferred_element_type=F32) + b_ref[0]


def _adaln(c_all, w_ada, b_ada):
    depth, d, n6 = w_ada.shape
    bc = c_all.shape[0]
    nblk = n6 // D_MODEL
    return pl.pallas_call(
        _adaln_kernel,
        grid=(depth, nblk),
        in_specs=[pl.BlockSpec((bc, d), lambda l, n: (0, 0)),
                  pl.BlockSpec((1, d, D_MODEL), lambda l, n: (l, 0, n)),
                  pl.BlockSpec((1, 1, D_MODEL), lambda l, n: (l, 0, n))],
        out_specs=pl.BlockSpec((1, bc, D_MODEL), lambda l, n: (l, 0, n)),
        out_shape=jax.ShapeDtypeStruct((depth, bc, n6), F32),
        compiler_params=_cparams(2), name="adaln",
    )(c_all, w_ada, b_ada.reshape(depth, 1, n6))


def _inproj_kernel(*refs, tm, layer):
    if layer == 0:
        (x_ref, mod_ref, g_ref, w_ref, cw_ref, st_ref,
         ya_ref, q_ref, k_ref, v_ref, qi_ref, kiw_ref, ki_ref, sg_ref, tail_ref, state_scr) = refs
    else:
        (x_ref, mod_ref, g_ref, w_ref, cw_ref, st_ref, _, _, _,
         ya_ref, q_ref, k_ref, v_ref, qi_ref, kiw_ref, ki_ref, sg_ref, tail_ref, state_scr) = refs

    def put(ref, val):
        if layer == 0:
            ref[0, 0] = val
            for other_layer in range(1, ref.shape[0]):
                ref[other_layer, 0] = jnp.zeros_like(val)
        else:
            ref[0, 0] = val

    t = pl.program_id(1)

    @pl.when(t == 0)
    def _():
        state_scr[...] = st_ref[0]

    x = x_ref[0]
    h = _rmsnorm(x, g_ref[...]) * (1.0 + mod_ref[0, 1]) + mod_ref[0, 0]
    hb = h.astype(BF16)

    def seg(a, b):
        return jnp.dot(hb, w_ref[:, a:b], preferred_element_type=F32)

    bg = seg(C_CONV, C_CONV + D_CONV)
    u = seg(C_CONV + D_CONV, C_CONV + 2 * D_CONV) * seg(C_CONV + 2 * D_CONV, C_CONV + 3 * D_CONV)
    s0 = state_scr[0:1, :]
    s1 = state_scr[1:2, :]
    ri = lax.broadcasted_iota(I32, u.shape, 0)
    um1 = jnp.where(ri == 0, s1, pltpu.roll(u, 1, 0))
    um2 = jnp.where(ri == 0, s0, jnp.where(ri == 1, s1, pltpu.roll(u, 2, 0)))
    cw = cw_ref[...]
    y = cw[0:1] * um2
    y = y + cw[1:2] * um1
    y = y + cw[2:3] * u
    ya_ref[0] = (bg * y).astype(BF16)
    tail = u[tm - 2:tm, :]
    state_scr[...] = tail
    tail_ref[0] = tail

    q_ref[0] = (seg(C_Q, C_Q + D_Q) * Q_SCALE).astype(BF16)
    put(k_ref, seg(C_KV, C_KV + D_KV))
    put(v_ref, seg(C_KV + D_KV, C_KV + 2 * D_KV))
    half = D_QI // 2
    qi_ref[0, :, 0:half] = seg(C_QI, C_QI + half).astype(BF16)
    qi_ref[0, :, half:D_QI] = seg(C_QI + half, C_QI + D_QI).astype(BF16)
    kw = seg(C_KIW, C_KIW + LANES)
    put(ki_ref, kw[:, 0:IDX_DIM])
    lane = lax.broadcasted_iota(I32, (1, LANES), 1)
    scale = jnp.where(lane < IDX_DIM, 1.0, jnp.where(lane < IDX_DIM + N_IDX_HEADS, WI_SCALE, 0.0))
    kiw_ref[0] = kw * scale

    for i in range(4):
        a = C_GATE + i * 512
        sg_ref[0, :, i * 512:(i + 1) * 512] = _sigmoid(seg(a, a + 512)).astype(BF16)


def _inproj(x, mod, g1, w_pack, conv_w, conv_state, tm, *, layer, depth, stacks):
    b, s, d = x.shape
    nt = s // tm
    row = lambda w, dt: jax.ShapeDtypeStruct((b, s, w), dt)
    stk = lambda w: jax.ShapeDtypeStruct((depth, b, s, w), F32)
    rspec = lambda w: pl.BlockSpec((1, tm, w), lambda bi, ti: (bi, ti, 0))
    if layer == 0:
        sspec = lambda w: pl.BlockSpec((depth, 1, tm, w), lambda bi, ti: (0, bi, ti, 0))
    else:
        sspec = lambda w: pl.BlockSpec((1, 1, tm, w), lambda bi, ti: (layer, bi, ti, 0))
    in_specs = [rspec(d),
                pl.BlockSpec((1, 6, 1, d), lambda bi, ti: (bi, 0, 0, 0)),
                _resident((1, d)),
                _resident((d, D_PACK)),
                _resident((CONV_W, D_CONV)),
                pl.BlockSpec((1, CONV_W - 1, D_CONV), lambda bi, ti: (bi, 0, 0))]
    args = [x, mod, g1, w_pack, conv_w, conv_state]
    aliases = {}
    if layer > 0:
        in_specs += [pl.BlockSpec(memory_space=pl.ANY)] * 3
        aliases = {len(args): 2, len(args) + 1: 3, len(args) + 2: 6}
        args += list(stacks)
    return pl.pallas_call(
        functools.partial(_inproj_kernel, tm=tm, layer=layer),
        grid=(b, nt),
        in_specs=in_specs,
        out_specs=[rspec(D_CONV), rspec(D_Q), sspec(D_KV), sspec(D_KV), rspec(D_QI),
                   rspec(LANES), sspec(IDX_DIM), rspec(2 * D_MODEL),
                   pl.BlockSpec((1, CONV_W - 1, D_CONV), lambda bi, ti: (bi, 0, 0))],
        out_shape=[row(D_CONV, BF16), row(D_Q, BF16), stk(D_KV), stk(D_KV),
                   row(D_QI, BF16), row(LANES, F32), stk(IDX_DIM), row(2 * D_MODEL, BF16),
                   jax.ShapeDtypeStruct((b, CONV_W - 1, D_CONV), F32)],
        scratch_shapes=[pltpu.VMEM((CONV_W - 1, D_CONV), F32)],
        input_output_aliases=aliases,
        compiler_params=_cparams(2), name="inproj",
    )(*args)


def _pair_blockdiag(tile, low_half):
    lane = lax.broadcasted_iota(I32, tile.shape, 1)
    lo = lane < HEAD_DIM
    swapped = pltpu.roll(tile, HEAD_DIM, 1)
    top = jnp.where(lo, tile if low_half else swapped, 0.0)
    bot = jnp.where(lo, 0.0, swapped if low_half else tile)
    return jnp.concatenate([top, bot], axis=0).astype(BF16)


def _attn_kernel(q_ref, qi_ref, kiw_ref, k_ref, v_ref, ki_ref, o_ref,
                 kbd_scr, vbd_scr, kibd_scr, key_scr, hi_scr, lo_scr, s_scr, m_scr, acc_scr,
                 *, tq, pos0, l_valid, topk):
    kb = KEY_BLOCK
    j = pl.program_id(1)
    nkc_all = key_scr.shape[0]
    npair = N_IDX_HEADS // 2
    nt = (((1,), (1,)), ((), ()))

    @pl.when(j == 0)
    def _():
        r_i = lax.broadcasted_iota(I32, (2 * kb, LANES), 0)
        c_i = lax.broadcasted_iota(I32, (2 * kb, LANES), 1)
        ones_bd = jnp.where((r_i < kb) == (c_i < HEAD_DIM), 1.0, 0.0).astype(BF16)

        def build(c, carry):
            rows = pl.ds(pl.multiple_of(c * kb, kb), kb)
            kibd_scr[c] = _pair_blockdiag(ki_ref[0, rows, :], True)
            for g in range(N_KV_HEADS):
                t = g // 2
                kbd_scr[g, c] = _pair_blockdiag(k_ref[0, 0, rows, t * LANES:(t + 1) * LANES], g % 2 == 0)
                vbd_scr[g, c, :, 0:LANES] = _pair_blockdiag(v_ref[0, 0, rows, t * LANES:(t + 1) * LANES],
                                                            g % 2 == 0)
                vbd_scr[g, c, :, LANES:2 * LANES] = ones_bd
            return carry

        lax.fori_loop(0, nkc_all, build, 0)

    w_t = kiw_ref[0].T

    row0 = pos0 + j * tq
    rows = row0 + lax.broadcasted_iota(I32, (1, tq), 1)
    lim = jnp.minimum(((rows >> 6) + 1) << 6, l_valid)
    last = row0 + tq - 1
    lim_max = jnp.minimum(((last >> 6) + 1) << 6, l_valid)
    nkc = (lim_max + kb - 1) >> 8

    def phase_a(c, carry):
        kbd = kibd_scr[c]
        acc = jnp.zeros((kb, tq), F32)
        for t in range(npair):
            d2 = lax.dot_general(kbd, qi_ref[0, :, t * LANES:(t + 1) * LANES], nt,
                                 preferred_element_type=F32)
            r = IDX_DIM + 2 * t
            acc = acc + w_t[r:r + 1, :] * jnp.maximum(d2[0:kb], 0.0)
            acc = acc + w_t[r + 1:r + 2, :] * jnp.maximum(d2[kb:2 * kb], 0.0)
        col = c * kb + lax.broadcasted_iota(I32, (kb, tq), 0)
        sc = jnp.where(col < lim, acc, -jnp.inf)
        bits = lax.bitcast_convert_type(sc, I32)
        key = bits ^ ((bits >> 31) & 0x7FFFFFFF)
        key_scr[c] = key
        hi_scr[c] = (key >> 16).astype(I16)
        lo_scr[c] = ((key & 0xFFFF) - 32768).astype(I16)
        return carry

    lax.fori_loop(0, nkc, phase_a, 0)

    def count_ge(cand):
        def body(c, cnt):
            m = jnp.where(key_scr[c] >= cand, 1.0, 0.0)
            parts = [m[8 * i:8 * (i + 1)] for i in range(kb // 8)]
            while len(parts) > 1:
                parts = [parts[i] + parts[i + 1] for i in range(0, len(parts), 2)]
            return cnt + parts[0]
        cnt = lax.fori_loop(0, nkc, body, jnp.zeros((8, tq), F32))
        return jnp.sum(cnt, axis=0, keepdims=True)

    def count16(ref, pred):
        def body(c, cnt):
            m = jnp.where(pred(ref[c]), jnp.int16(1), jnp.int16(0))
            parts = [m[16 * i:16 * (i + 1)] for i in range(kb // 16)]
            while len(parts) > 1:
                parts = [parts[i] + parts[i + 1] for i in range(0, len(parts), 2)]
            return cnt + parts[0]
        cnt = lax.fori_loop(0, nkc, body, jnp.zeros((16, tq), I16))
        return jnp.sum(cnt.astype(I32), axis=0, keepdims=True)

    def radix16(ref, need):
        t0 = jnp.where(count16(ref, lambda h: h >= jnp.int16(0)) >= need, 0, -32768).astype(I32)

        def bit_body(i, t):
            cand = t | lax.shift_left(jnp.int32(1), 14 - i)
            c16 = cand.astype(I16)
            return jnp.where(count16(ref, lambda h: h >= c16) >= need, cand, t)

        return lax.fori_loop(0, 15, bit_body, t0)

    t_hi = radix16(hi_scr, topk)
    h16 = t_hi.astype(I16)
    need_lo = topk - count16(hi_scr, lambda h: h > h16)

    def keep_bucket(c, carry):
        lo_scr[c] = jnp.where(hi_scr[c] == h16, lo_scr[c], jnp.int16(-32768))
        return carry

    lax.fori_loop(0, nkc, keep_bucket, 0)
    t_lo = radix16(lo_scr, need_lo)
    thr = (t_hi << 16) | ((t_lo + 32768) & 0xFFFF)
    kf = float(topk)
    few = lim <= topk
    thr = jnp.where(few, KEY_MIN_FINITE, thr)

    tie = jnp.logical_and(count_ge(thr) > kf, jnp.logical_not(few))
    any_tie = jnp.max(jnp.where(tie, 1.0, 0.0)) > 0.0

    @pl.when(any_tie)
    def _():
        need = kf - count_ge(thr + 1)
        r_i = lax.broadcasted_iota(I32, (kb, kb), 0)
        c_i = lax.broadcasted_iota(I32, (kb, kb), 1)
        tri = jnp.where(c_i < r_i, 1.0, 0.0).astype(BF16)

        def body(c, run):
            kc = key_scr[c]
            eqf = jnp.where(kc == thr, 1.0, 0.0)
            rank = run + jnp.dot(tri, eqf.astype(BF16), preferred_element_type=F32)
            drop = jnp.logical_and(jnp.logical_and(tie, kc == thr), rank >= need)
            key_scr[c] = jnp.where(drop, KEY_NEG_INF, kc)
            return run + jnp.sum(eqf, axis=0, keepdims=True)

        lax.fori_loop(0, nkc, body, jnp.zeros((1, tq), F32))

    nl = kb // LANES
    m_scr[...] = jnp.full(m_scr.shape, MASK_BIAS, F32)
    acc_scr[...] = jnp.zeros(acc_scr.shape, F32)

    def sweep1(c, carry):
        bias = jnp.where(key_scr[c] >= thr, 0.0, MASK_BIAS).T
        for g in range(N_KV_HEADS):
            s2 = lax.dot_general(q_ref[0, :, g * LANES:(g + 1) * LANES], kbd_scr[g, c], nt,
                                 preferred_element_type=F32)
            for r in range(GROUP):
                h = GROUP * g + r
                s = s2[:, r * kb:(r + 1) * kb] + bias
                s_scr[h, c] = s
                mx = m_scr[h]
                for i in range(nl):
                    mx = jnp.maximum(mx, s[:, i * LANES:(i + 1) * LANES])
                m_scr[h] = mx
        return carry

    lax.fori_loop(0, nkc, sweep1, 0)
    for h in range(N_HEADS):
        m_scr[h] = jnp.broadcast_to(jnp.max(m_scr[h], axis=1, keepdims=True), (tq, LANES))

    def sweep2(c, carry):
        for g in range(N_KV_HEADS):
            p2 = []
            for r in range(GROUP):
                h = GROUP * g + r
                s = s_scr[h, c]
                mb = m_scr[h]
                p2 += [jnp.exp2(s[:, i * LANES:(i + 1) * LANES] - mb) for i in range(nl)]
            pb = jnp.concatenate(p2, axis=1).astype(BF16)
            acc_scr[g] += jnp.dot(pb, vbd_scr[g, c], preferred_element_type=F32)
        return carry

    lax.fori_loop(0, nkc, sweep2, 0)
    for g in range(N_KV_HEADS):
        o_ref[0, :, g * LANES:(g + 1) * LANES] = (acc_scr[g, :, 0:LANES]
                                                  / acc_scr[g, :, LANES:2 * LANES]).astype(BF16)


def _attention(q, qi, kiw, k_all, v_all, ki_all, *, layer, tq, pos0, l_valid, topk):
    b, s, _ = q.shape
    l_pad = k_all.shape[2]
    nkc_max = l_pad // KEY_BLOCK
    qspec = lambda w: pl.BlockSpec((1, tq, w), lambda bi, ji: (bi, ji, 0))
    kspec = lambda w: pl.BlockSpec((1, l_pad, w), lambda bi, ji: (bi, 0, 0))
    lspec = lambda w: pl.BlockSpec((1, 1, l_pad, w), lambda bi, ji: (layer, bi, 0, 0))
    return pl.pallas_call(
        functools.partial(_attn_kernel, tq=tq, pos0=pos0, l_valid=l_valid, topk=topk),
        grid=(b, s // tq),
        in_specs=[qspec(D_Q), qspec(D_QI), qspec(LANES), lspec(D_KV), lspec(D_KV), kspec(LANES)],
        out_specs=qspec(D_Q),
        out_shape=jax.ShapeDtypeStruct((b, s, D_Q), BF16),
        scratch_shapes=[pltpu.VMEM((N_KV_HEADS, nkc_max, 2 * KEY_BLOCK, LANES), BF16),
                        pltpu.VMEM((N_KV_HEADS, nkc_max, 2 * KEY_BLOCK, 2 * LANES), BF16),
                        pltpu.VMEM((nkc_max, 2 * KEY_BLOCK, LANES), BF16),
                        pltpu.VMEM((nkc_max, KEY_BLOCK, tq), I32),
                        pltpu.VMEM((nkc_max, KEY_BLOCK, tq), I16),
                        pltpu.VMEM((nkc_max, KEY_BLOCK, tq), I16),
                        pltpu.VMEM((N_HEADS, nkc_max, tq, KEY_BLOCK), F32),
                        pltpu.VMEM((N_HEADS, tq, LANES), F32),
                        pltpu.VMEM((N_KV_HEADS, tq, 2 * LANES), F32)],
        compiler_params=_cparams(2), name="attention",
    )(q, qi, kiw, k_all, v_all, ki_all)


def _split3(a):
    hi = a.astype(BF16)
    r = a - hi.astype(F32)
    mid = r.astype(BF16)
    lo = (r - mid.astype(F32)).astype(BF16)
    return hi, mid, lo


def _merge_kernel(*refs, with_router):
    if with_router:
        (x_ref, ya_ref, ao_ref, sg_ref, mod_ref, wc_ref, wa_ref, wo_ref, g2_ref, rw_ref, rb_ref,
         x1_ref, h2_ref, lg_ref) = refs
    else:
        (x_ref, ya_ref, ao_ref, sg_ref, mod_ref, wc_ref, wa_ref, wo_ref, g2_ref,
         x1_ref, h2_ref) = refs
    a = jnp.dot(ya_ref[0], wc_ref[...], preferred_element_type=F32)
    b = jnp.dot(ao_ref[0], wa_ref[...], preferred_element_type=F32)
    merged = sg_ref[0, :, 0:D_MODEL].astype(F32) * a + sg_ref[0, :, D_MODEL:2 * D_MODEL].astype(F32) * b
    mix = jnp.dot(merged.astype(BF16), wo_ref[...], preferred_element_type=F32)
    x1 = x_ref[0] + mod_ref[0, 2] * mix
    x1_ref[0] = x1
    h2 = _rmsnorm(x1, g2_ref[...]) * (1.0 + mod_ref[0, 4]) + mod_ref[0, 3]
    h2_ref[0] = h2.astype(h2_ref.dtype)
    if with_router:
        hs = _split3(h2)
        acc = jnp.zeros((h2.shape[0], LANES), F32)
        for i, jj in ((0, 0), (0, 1), (1, 0)):
            acc = acc + jnp.dot(hs[i], rw_ref[jj], preferred_element_type=F32)
        lane = lax.broadcasted_iota(I32, acc.shape, 1)
        lg = jnp.where(lane < N_EXPERTS, acc + rb_ref[...], -jnp.inf)
        m1 = jnp.max(lg, axis=1, keepdims=True)
        i1 = jnp.min(jnp.where(lg == m1, lane, LANES), axis=1, keepdims=True)
        rest = jnp.where(lane == i1, -jnp.inf, lg)
        m2 = jnp.max(rest, axis=1, keepdims=True)
        i2 = jnp.min(jnp.where(rest == m2, lane, LANES), axis=1, keepdims=True)
        e2 = jnp.exp(m2 - m1)
        den = 1.0 + e2
        lg_ref[0] = jnp.where(lane == 0, i1.astype(F32),
                              jnp.where(lane == 1, i2.astype(F32),
                                        jnp.where(lane == 2, 1.0 / den,
                                                  jnp.where(lane == 3, e2 / den, 0.0))))


def _merge(x, ya, ao, sg, mod, wc, wa, wo, g2, router=None, *, tm):
    b, s, d = x.shape
    r = mod.shape[2]
    rspec = lambda w: pl.BlockSpec((1, tm, w), lambda bi, ti: (bi, ti, 0))
    mspec = (pl.BlockSpec((1, 6, 1, d), lambda bi, ti: (bi, 0, 0, 0)) if r == 1 else
             pl.BlockSpec((1, 6, tm, d), lambda bi, ti: (bi, 0, ti, 0)))
    in_specs = [rspec(d), rspec(D_CONV), rspec(D_Q), rspec(2 * d), mspec,
                _resident((D_CONV, d)), _resident((D_Q, d)), _resident((d, d)), _resident((1, d))]
    args = [x, ya, ao, sg, mod, wc, wa, wo, g2]
    out_specs = [rspec(d), rspec(d)]
    out_shape = [jax.ShapeDtypeStruct((b, s, d), F32),
                 jax.ShapeDtypeStruct((b, s, d), BF16 if router is None else F32)]
    if router is not None:
        in_specs += [_resident((3, d, LANES)), _resident((1, LANES))]
        args += list(router)
        out_specs.append(rspec(LANES))
        out_shape.append(jax.ShapeDtypeStruct((b, s, LANES), F32))
    return pl.pallas_call(
        functools.partial(_merge_kernel, with_router=router is not None),
        grid=(b, s // tm),
        in_specs=in_specs, out_specs=out_specs, out_shape=out_shape,
        compiler_params=_cparams(2), name="merge",
    )(*args)


def _ffn_kernel(h2_ref, x1_ref, mod_ref, w1_ref, w3_ref, w2_ref, o_ref, acc_scr, *, nc):
    h = h2_ref[0]
    acc_scr[...] = _swiglu_chunk(h, w1_ref[0], w3_ref[0], w2_ref[0])

    def body(c, carry):
        acc_scr[...] += _swiglu_chunk(h, w1_ref[c], w3_ref[c], w2_ref[c])
        return carry

    lax.fori_loop(1, nc, body, 0)
    o_ref[0] = x1_ref[0] + mod_ref[0, 5] * acc_scr[...]


def _ffn(h2, x1, mod, w1, w3, w2, *, tm):
    b, s, d = x1.shape
    nc = w1.shape[0]
    r = mod.shape[2]
    rspec = lambda w: pl.BlockSpec((1, tm, w), lambda bi, ti: (bi, ti, 0))
    mspec = (pl.BlockSpec((1, 6, 1, d), lambda bi, ti: (bi, 0, 0, 0)) if r == 1 else
             pl.BlockSpec((1, 6, tm, d), lambda bi, ti: (bi, 0, ti, 0)))
    return pl.pallas_call(
        functools.partial(_ffn_kernel, nc=nc),
        grid=(b, s // tm),
        in_specs=[rspec(d), rspec(d), mspec, _resident(w1.shape), _resident(w3.shape), _resident(w2.shape)],
        out_specs=rspec(d),
        out_shape=jax.ShapeDtypeStruct((b, s, d), F32),
        scratch_shapes=[pltpu.VMEM((tm, d), F32)],
        compiler_params=_cparams(2), name="ffn",
    )(h2, x1, mod, w1, w3, w2)


def _swiglu_chunk(xb, w1, w3, w2):
    a = jnp.dot(xb, w1, preferred_element_type=F32)
    hm = (a * _sigmoid(a) * jnp.dot(xb, w3, preferred_element_type=F32)).astype(BF16)
    return jnp.dot(hm, w2, preferred_element_type=F32)


def _moe_dense_kernel(h2_ref, x1_ref, rt_ref, mod_ref, w1_ref, w3_ref, w2_ref, fg_ref, o_ref,
                      eacc_scr, acc_scr, *, final_norm):
    e = pl.program_id(2)
    c = pl.program_id(3)
    ne = pl.num_programs(2)
    nc = pl.num_programs(3)

    @pl.when(jnp.logical_and(e == 0, c == 0))
    def _():
        acc_scr[...] = jnp.zeros(acc_scr.shape, F32)

    @pl.when(c == 0)
    def _():
        eacc_scr[...] = jnp.zeros(eacc_scr.shape, F32)

    eacc_scr[...] += _swiglu_chunk(h2_ref[0].astype(BF16), w1_ref[0], w3_ref[0], w2_ref[0])

    @pl.when(c == nc - 1)
    def _():
        rt = rt_ref[0]
        ef = e.astype(F32)
        gate = (jnp.where(rt[:, 0:1] == ef, rt[:, 2:3], 0.0)
                + jnp.where(rt[:, 1:2] == ef, rt[:, 3:4], 0.0))
        acc_scr[...] += gate * eacc_scr[...]

    @pl.when(jnp.logical_and(e == ne - 1, c == nc - 1))
    def _():
        x2 = x1_ref[0] + mod_ref[0, 5] * acc_scr[...]
        o_ref[0] = _rmsnorm(x2, fg_ref[...]) if final_norm else x2


def _moe_dense(h2, x1, route, mod, w1, w3, w2, fg, *, tm, final_norm):
    b, s, d = x1.shape
    ne, _, f = w1.shape
    fc = MOE_CHUNK
    r = mod.shape[2]
    rspec = lambda w: pl.BlockSpec((1, tm, w), lambda bi, ti, ei, ci: (bi, ti, 0))
    mspec = (pl.BlockSpec((1, 6, 1, d), lambda bi, ti, ei, ci: (bi, 0, 0, 0)) if r == 1 else
             pl.BlockSpec((1, 6, tm, d), lambda bi, ti, ei, ci: (bi, 0, ti, 0)))
    wcol = pl.BlockSpec((1, d, fc), lambda bi, ti, ei, ci: (ei, 0, ci))
    return pl.pallas_call(
        functools.partial(_moe_dense_kernel, final_norm=final_norm),
        grid=(b, s // tm, ne, f // fc),
        in_specs=[rspec(d), rspec(d), rspec(LANES), mspec, wcol, wcol,
                  pl.BlockSpec((1, fc, d), lambda bi, ti, ei, ci: (ei, ci, 0)),
                  _resident((1, d))],
        out_specs=rspec(d),
        out_shape=jax.ShapeDtypeStruct((b, s, d), F32),
        scratch_shapes=[pltpu.VMEM((tm, d), F32), pltpu.VMEM((tm, d), F32)],
        compiler_params=_cparams(4), name="moe_dense",
    )(h2, x1, route, mod, w1, w3, w2, fg)


def _route_tables(route, tmr):
    n = route.shape[0]
    e1 = route[:, 0].astype(I32)
    e2 = route[:, 1].astype(I32)
    ar = jnp.arange(N_EXPERTS, dtype=I32)
    oh1 = e1[:, None] == ar
    oh2 = e2[:, None] == ar
    oh = oh1.astype(I32) + oh2.astype(I32)
    cum = jnp.cumsum(oh, axis=0)
    excl = cum - oh
    cnt = cum[-1]
    tiles_e = (cnt + tmr - 1) // tmr
    tile_end = jnp.cumsum(tiles_e)
    tile_start = tile_end - tiles_e
    row_off = tile_start * tmr
    d1 = row_off[e1] + jnp.sum(jnp.where(oh1, excl, 0), axis=1)
    d2 = row_off[e2] + jnp.sum(jnp.where(oh2, excl, 0), axis=1)
    nt = -(-2 * n // tmr) + N_EXPERTS
    tok = jnp.arange(n, dtype=I32)
    pair = jnp.zeros((nt * tmr,), I32).at[jnp.concatenate([d1, d2])].set(
        jnp.concatenate([2 * tok, 2 * tok + 1]), unique_indices=True)
    src = pair >> 1
    dst = (pair & 1) * n + src
    tile = jnp.arange(nt, dtype=I32)
    te = jnp.minimum(jnp.searchsorted(tile_end, tile, side="right"), N_EXPERTS - 1).astype(I32)
    nv = jnp.where(tile < tile_end[-1], jnp.clip(cnt[te] - (tile - tile_start[te]) * tmr, 0, tmr), 0)
    return (src.reshape(nt, 1, tmr), dst.reshape(nt, 1, tmr), te, nv.astype(I32),
            tile_end[-1:].astype(I32))


def _moe_routed_kernel(te_ref, nv_ref, nu_ref, src0_ref, srcn_ref, dstp_ref, h2_hbm, w1_ref, w3_ref, w2_ref,
                       out_hbm, xbuf, xb_scr, ybuf, gsem, ssem, *, tmr, nc, gs):
    i = pl.program_id(0)
    c = pl.program_id(1)
    nu = nu_ref[0]
    slot = i % 2
    other = 1 - slot
    rpc = tmr // nc
    rpg = tmr // gs

    def gather(src_ref, cc, k, sl):
        return pltpu.make_async_copy(h2_hbm.at[pl.ds(src_ref[0, 0, cc * rpg + k], 1)],
                                     xbuf.at[sl, cc, pl.ds(k, 1)], gsem.at[sl])

    def scatter(cc, k, sl):
        return pltpu.make_async_copy(ybuf.at[sl, cc, pl.ds(k, 1)],
                                     out_hbm.at[pl.ds(dstp_ref[0, 0, cc * rpc + k], 1)], ssem.at[sl])

    @pl.when(jnp.logical_and(i == 0, c == 0))
    def _():
        def body(cc, carry):
            for k in range(rpg):
                gather(src0_ref, cc, k, 0).start()
            return carry
        lax.fori_loop(0, gs, body, 0)

    @pl.when(jnp.logical_and(c == 0, i < nu))
    def _():
        pltpu.make_async_copy(xbuf.at[slot], xbuf.at[slot], gsem.at[slot]).wait()
        xb_scr[...] = xbuf[slot].reshape(tmr, xb_scr.shape[1]).astype(BF16)

    last = nv_ref.shape[0] - 1
    nv_drain = nv_ref[jnp.clip(i - 2, 0, last)]
    push = jnp.logical_and(i >= 1, i - 1 < nu)
    nv_push = nv_ref[jnp.clip(i - 1, 0, last)]
    fast = jnp.logical_and(i >= 1, jnp.logical_and(i + 1 < nu, nv_push == tmr))
    slow = jnp.logical_not(fast)
    drain = jnp.logical_and(c == 0, jnp.logical_and(i >= 2, i - 2 < nu))

    @pl.when(jnp.logical_and(drain, nv_drain == tmr))
    def _():
        pltpu.make_async_copy(ybuf.at[slot], ybuf.at[slot], ssem.at[slot]).wait()

    @pl.when(jnp.logical_and(drain, nv_drain < tmr))
    def _():
        def body(r, carry):
            scatter(0, 0, slot).wait()
            return carry
        lax.fori_loop(0, nv_drain, body, 0)

    @pl.when(fast)
    def _():
        xb = xb_scr[...]
        a = jnp.dot(xb, w1_ref[0], preferred_element_type=F32)
        b = jnp.dot(xb, w3_ref[0], preferred_element_type=F32)
        for k in range(rpg):
            gather(srcn_ref, c, k, other).start()
        for k in range(rpc):
            scatter(c, k, other).start()
        hm = (a * _sigmoid(a) * b).astype(BF16)
        y = jnp.dot(hm, w2_ref[0], preferred_element_type=F32).reshape(nc, rpc, xb_scr.shape[1])
        ybuf[slot] = jnp.where(c == 0, y, ybuf[slot] + y)

    @pl.when(jnp.logical_and(slow, jnp.logical_and(i + 1 < nu, c < gs)))
    def _():
        for k in range(rpg):
            gather(srcn_ref, c, k, other).start()

    @pl.when(jnp.logical_and(slow, jnp.logical_and(push, nv_push == tmr)))
    def _():
        for k in range(rpc):
            scatter(c, k, other).start()

    @pl.when(jnp.logical_and(slow, jnp.logical_and(push, nv_push < tmr)))
    def _():
        def body(k, carry):
            scatter(c, k, other).start()
            return carry
        lax.fori_loop(0, jnp.clip(nv_push - c * rpc, 0, rpc), body, 0)

    @pl.when(jnp.logical_and(slow, i < nu))
    def _():
        y = _swiglu_chunk(xb_scr[...], w1_ref[0], w3_ref[0], w2_ref[0]).reshape(nc, rpc, xb_scr.shape[1])

        @pl.when(c == 0)
        def _():
            ybuf[slot] = y

        @pl.when(c > 0)
        def _():
            ybuf[slot] += y


def _moe_routed(h2, route, w1, w3, w2, *, tmr):
    n, d = h2.shape
    f = w1.shape[2]
    fc = MOE_CHUNK
    nc = f // fc
    gs = MOE_GATHER_STEPS
    src, dst, te, nv, nu = _route_tables(route, tmr)
    nt = src.shape[0]
    last = nt - 1
    smem = lambda imap: pl.BlockSpec((1, 1, tmr), imap, memory_space=pltpu.SMEM)
    chunk = lambda i, c, nu_r: jnp.where(i < nu_r[0], c, nc - 1)
    wcol = pl.BlockSpec((1, d, fc), lambda i, c, te_r, nv_r, nu_r: (te_r[jnp.minimum(i, last)], 0, chunk(i, c, nu_r)))
    wrow = pl.BlockSpec((1, fc, d), lambda i, c, te_r, nv_r, nu_r: (te_r[jnp.minimum(i, last)], chunk(i, c, nu_r), 0))
    return pl.pallas_call(
        functools.partial(_moe_routed_kernel, tmr=tmr, nc=nc, gs=gs),
        grid_spec=pltpu.PrefetchScalarGridSpec(
            num_scalar_prefetch=3,
            grid=(nt + 2, nc),
            in_specs=[smem(lambda i, c, te_r, nv_r, nu_r: (0, 0, 0)),
                      smem(lambda i, c, te_r, nv_r, nu_r: (jnp.minimum(i + 1, last), 0, 0)),
                      smem(lambda i, c, te_r, nv_r, nu_r: (jnp.clip(i - 1, 0, last), 0, 0)),
                      pl.BlockSpec(memory_space=pl.ANY), wcol, wcol, wrow],
            out_specs=pl.BlockSpec(memory_space=pl.ANY),
            scratch_shapes=[pltpu.VMEM((2, gs, tmr // gs, d), F32), pltpu.VMEM((tmr, d), BF16),
                            pltpu.VMEM((2, nc, tmr // nc, d), F32),
                            pltpu.SemaphoreType.DMA((2,)), pltpu.SemaphoreType.DMA((2,))]),
        out_shape=jax.ShapeDtypeStruct((2 * n, d), F32),
        compiler_params=_cparams(2), name="moe_routed",
    )(te, nv, nu, src, src, dst, h2, w1, w3, w2)


def _moe_combine_kernel(x1_ref, ya_ref, yb_ref, rt_ref, mod_ref, fg_ref, o_ref, *, final_norm):
    rt = rt_ref[0]
    f = rt[:, 2:3] * ya_ref[...] + rt[:, 3:4] * yb_ref[...]
    x2 = x1_ref[0] + mod_ref[0, 5] * f
    o_ref[0] = _rmsnorm(x2, fg_ref[...]) if final_norm else x2


def _moe_combine(x1, y2, route, mod, fg, *, tm, final_norm):
    b, s, d = x1.shape
    nt = s // tm
    rspec = lambda w: pl.BlockSpec((1, tm, w), lambda bi, ti: (bi, ti, 0))
    return pl.pallas_call(
        functools.partial(_moe_combine_kernel, final_norm=final_norm),
        grid=(b, nt),
        in_specs=[rspec(d),
                  pl.BlockSpec((tm, d), lambda bi, ti: (bi * nt + ti, 0)),
                  pl.BlockSpec((tm, d), lambda bi, ti: (b * nt + bi * nt + ti, 0)),
                  rspec(LANES),
                  pl.BlockSpec((1, 6, 1, d), lambda bi, ti: (bi, 0, 0, 0)), _resident((1, d))],
        out_specs=rspec(d),
        out_shape=jax.ShapeDtypeStruct((b, s, d), F32),
        compiler_params=_cparams(2), name="moe_combine",
    )(x1, y2, y2, route, mod, fg)


def _pack_w_in(w):
    cut = C_KIW + IDX_DIM + N_IDX_HEADS
    pad = jnp.zeros((w.shape[0], C_GATE - cut), w.dtype)
    return jnp.concatenate([w[:, :cut], pad, w[:, cut:]], axis=1).astype(BF16)


def _chunk_cols(w, fc):
    *lead, d, f = w.shape
    return jnp.moveaxis(w.astype(BF16).reshape(*lead, d, f // fc, fc), -2, -3)


def _chunk_w2(w2, fc):
    *lead, f, d = w2.shape
    return w2.reshape(*lead, f // fc, fc, d).astype(BF16)


def _router_terms(rw, rb):
    d = rw.shape[0]
    rwp = jnp.zeros((d, LANES), F32).at[:, :N_EXPERTS].set(rw)
    hi, mid, lo = _split3(rwp)
    rbp = jnp.zeros((1, LANES), F32).at[0, :N_EXPERTS].set(rb)
    return jnp.stack([hi, mid, lo]), rbp


def _trunk(x, mod_all, caches, p, *, tm, tq):
    b, t, d = x.shape
    depth = mod_all.shape[0]
    flat = caches is not None
    stacks, tails = None, []
    for l in range(depth):
        mod = mod_all[l].reshape(b, 6, 1, d)
        if caches is None:
            conv_state = jnp.zeros((b, CONV_W - 1, D_CONV), F32)
        else:
            conv_state = caches[3][l]
        ya, q, k_st, v_st, qi, kiw, ki_st, sg, tail = _inproj(
            x, mod, p["norm1_g"][l], p["w_in"][l], p["conv_w"][l], conv_state, tm,
            layer=l, depth=depth, stacks=stacks)
        stacks = (k_st, v_st, ki_st)
        if caches is None:
            ao = _attention(q, qi, kiw, k_st, v_st, kiw, layer=l, tq=tq, pos0=0, l_valid=t,
                            topk=min(TOPK_MAX, t // 4))
        else:
            past = caches[0][l].shape[1]
            l_valid = past + t
            l_pad = -(-l_valid // KEY_BLOCK) * KEY_BLOCK
            cat = lambda old, new: jnp.concatenate(
                [old.reshape(b, past, -1), new, jnp.zeros((b, l_pad - l_valid, new.shape[-1]), F32)], axis=1)
            k_all, v_all = cat(caches[0][l], k_st[l])[None], cat(caches[1][l], v_st[l])[None]
            ki_all = jnp.pad(cat(caches[2][l], ki_st[l]), ((0, 0), (0, 0), (0, LANES - IDX_DIM)))
            padq = lambda a: jnp.pad(a, ((0, 0), (0, tq - t), (0, 0)))
            ao = _attention(padq(q), padq(qi), padq(kiw), k_all, v_all, ki_all, layer=0, tq=tq, pos0=past,
                            l_valid=l_valid, topk=min(TOPK_MAX, l_valid // 4))[:, :t]
        is_moe = l % 2 == 1
        router = (p["router_w"][l // 2], p["router_b"][l // 2]) if is_moe else None
        if flat:
            n = b * t
            fl = lambda a: a.reshape(1, n, a.shape[-1])
            modf = jnp.repeat(mod_all[l].reshape(b, 6, d), t, axis=0).transpose(1, 0, 2)[None]
            xs, yas, aos, sgs, tmf = fl(x), fl(ya), fl(ao), fl(sg), n
        else:
            modf, xs, yas, aos, sgs, tmf = mod, x, ya, ao, sg, tm
        outs = _merge(xs, yas, aos, sgs, modf, p["w_conv_out"][l], p["w_attn_out"][l], p["w_o"][l],
                      p["norm2_g"][l], router, tm=tmf if flat else min(2 * tm, t))
        if is_moe and flat:
            x1, h2, route = outs
            x = _moe_dense(h2, x1, route, modf, p["moe_w1"][l // 2], p["moe_w3"][l // 2],
                           p["moe_w2"][l // 2], p["final_g"], tm=xs.shape[1], final_norm=l == depth - 1)
        elif is_moe:
            x1, h2, route = outs
            y2 = _moe_routed(h2.reshape(b * t, d), route.reshape(b * t, LANES), p["moe_w1"][l // 2],
                             p["moe_w3"][l // 2], p["moe_w2"][l // 2], tmr=MOE_ROWS)
            x = _moe_combine(x1, y2, route, mod, p["final_g"], tm=tm,
                             final_norm=l == depth - 1)
        else:
            x1, h2 = outs
            x = _ffn(h2, x1, modf, p["ffn_w1"][l // 2], p["ffn_w3"][l // 2], p["ffn_w2"][l // 2],
                     tm=tmf if flat else min(2 * tm, t))
        x = x.reshape(b, t, d)
        tails.append(tail)
    k_st, v_st, ki_st = stacks
    heads = (depth, b, t, N_KV_HEADS, HEAD_DIM)
    return x, k_st.reshape(heads), v_st.reshape(heads), ki_st, jnp.stack(tails)


def kernel(x_prompt, x_sample, c_prompt, c_sample, cache_k, cache_v, cache_idx_k, state_conv, w_ada, b_ada, norm1_g, w_in, conv_w, w_conv_out, w_attn_out, w_o, norm2_g, ffn_w1, ffn_w3, ffn_w2, router_w, router_b, moe_w1, moe_w3, moe_w2, final_g):
    depth = w_in.shape[0]
    bp = x_prompt.shape[0]
    rterms = [_router_terms(router_w[i], router_b[i]) for i in range(router_w.shape[0])]
    p = {
        "norm1_g": norm1_g.reshape(depth, 1, -1),
        "norm2_g": norm2_g.reshape(depth, 1, -1),
        "final_g": final_g.reshape(1, -1),
        "w_in": jnp.stack([_pack_w_in(w_in[l]) for l in range(depth)]),
        "conv_w": conv_w,
        "w_conv_out": w_conv_out.astype(BF16),
        "w_attn_out": w_attn_out.astype(BF16),
        "w_o": w_o.astype(BF16),
        "ffn_w1": _chunk_cols(ffn_w1, FFN_CHUNK),
        "ffn_w3": _chunk_cols(ffn_w3, FFN_CHUNK),
        "ffn_w2": _chunk_w2(ffn_w2, FFN_CHUNK),
        "moe_w1": moe_w1.astype(BF16),
        "moe_w3": moe_w3.astype(BF16),
        "moe_w2": moe_w2.astype(BF16),
        "router_w": [r[0] for r in rterms],
        "router_b": [r[1] for r in rterms],
    }
    mod_all = _adaln(jnp.concatenate([c_prompt, c_sample], axis=0), w_ada.astype(BF16), b_ada)
    seq = x_prompt.shape[1]
    tm = min(512, seq)
    tq = min(256, seq)
    yp, kp, vp, kip, cp = _trunk(x_prompt, mod_all[:, :bp], None, p, tm=tm, tq=tq)
    ts = x_sample.shape[1]
    ys, ksm, vsm, kism, csm = _trunk(x_sample, mod_all[:, bp:], (cache_k, cache_v, cache_idx_k, state_conv), p,
                                     tm=ts, tq=-(-ts // LANES) * LANES)
    return (yp, ys, kp, vp, kip, cp, ksm, vsm, kism, csm)
```

```python
import functools

import jax
import jax.numpy as jnp
from jax import lax
from jax.experimental import pallas as pl
from jax.experimental.pallas import tpu as pltpu

F32 = jnp.float32
BF16 = jnp.bfloat16
I32 = jnp.int32
I16 = jnp.int16

D_MODEL = 1024
D_CONV = 512
CONV_W = 3
N_HEADS = 8
N_KV_HEADS = 4
HEAD_DIM = 64
GROUP = N_HEADS // N_KV_HEADS
N_IDX_HEADS = 16
IDX_DIM = 64
TOPK_MAX = 256
CHUNK = 64
D_FF = 2816
N_EXPERTS = 8
D_FF_EXPERT = 3584
EPS = 1e-6

D_Q = N_HEADS * HEAD_DIM
D_KV = N_KV_HEADS * HEAD_DIM
D_QI = N_IDX_HEADS * IDX_DIM
LANES = 128
C_CONV = 0
C_Q = 3 * D_CONV
C_KV = C_Q + D_Q
C_QI = C_KV + 2 * D_KV
C_KIW = C_QI + D_QI
C_GATE = C_KIW + LANES
D_PACK = C_GATE + 2 * D_MODEL
WI_SCALE = (IDX_DIM ** -0.5) * (N_IDX_HEADS ** -0.5)
Q_SCALE = HEAD_DIM ** -0.5 * 1.4426950408889634

KEY_BLOCK = 256
FFN_CHUNK = 256
MOE_CHUNK = 512
MOE_ROWS = (D_FF_EXPERT // MOE_CHUNK) * 128
MOE_GATHER_STEPS = D_FF_EXPERT // MOE_CHUNK
V7X_VMEM_LIMIT = 56 * 1024 * 1024

KEY_NEG_INF = -2139095041
KEY_MIN_FINITE = KEY_NEG_INF + 1
MASK_BIAS = -1e30


def _sigmoid(x):
    return 1.0 / (1.0 + jnp.exp(-x))


def _rmsnorm(x, g):
    ms = jnp.mean(x * x, axis=-1, keepdims=True)
    return x * lax.rsqrt(ms + EPS) * g


def _cparams(n_axes):
    return pltpu.CompilerParams(dimension_semantics=("arbitrary",) * n_axes,
                                vmem_limit_bytes=V7X_VMEM_LIMIT)


def _resident(shape):
    nd = len(shape)
    return pl.BlockSpec(shape, lambda *_: (0,) * nd, pipeline_mode=pl.Buffered(1))


def _adaln_kernel(c_ref, w_ref, b_ref, o_ref):
    c = c_ref[...]
    cond = (c * _sigmoid(c)).astype(BF16)
    o_ref[0] = jnp.dot(cond, w_ref[0], preferred_element_type=F32) + b_ref[0]


def _adaln(c_all, w_ada, b_ada):
    depth, d, n6 = w_ada.shape
    bc = c_all.shape[0]
    nblk = n6 // D_MODEL
    return pl.pallas_call(
        _adaln_kernel,
        grid=(depth, nblk),
        in_specs=[pl.BlockSpec((bc, d), lambda l, n: (0, 0)),
                  pl.BlockSpec((1, d, D_MODEL), lambda l, n: (l, 0, n)),
                  pl.BlockSpec((1, 1, D_MODEL), lambda l, n: (l, 0, n))],
        out_specs=pl.BlockSpec((1, bc, D_MODEL), lambda l, n: (l, 0, n)),
        out_shape=jax.ShapeDtypeStruct((depth, bc, n6), F32),
        compiler_params=_cparams(2), name="adaln",
    )(c_all, w_ada, b_ada.reshape(depth, 1, n6))


def _inproj_kernel(*refs, tm, layer):
    if layer == 0:
        (x_ref, mod_ref, g_ref, w_ref, cw_ref, st_ref,
         ya_ref, q_ref, k_ref, v_ref, qi_ref, kiw_ref, ki_ref, sg_ref, tail_ref, state_scr) = refs
    else:
        (x_ref, mod_ref, g_ref, w_ref, cw_ref, st_ref, _, _, _,
         ya_ref, q_ref, k_ref, v_ref, qi_ref, kiw_ref, ki_ref, sg_ref, tail_ref, state_scr) = refs

    def put(ref, val):
        if layer == 0:
            ref[0, 0] = val
            for other_layer in range(1, ref.shape[0]):
                ref[other_layer, 0] = jnp.zeros_like(val)
        else:
            ref[0, 0] = val

    t = pl.program_id(1)

    @pl.when(t == 0)
    def _():
        state_scr[...] = st_ref[0]

    x = x_ref[0]
    h = _rmsnorm(x, g_ref[...]) * (1.0 + mod_ref[0, 1]) + mod_ref[0, 0]
    hb = h.astype(BF16)

    def seg(a, b):
        return jnp.dot(hb, w_ref[:, a:b], preferred_element_type=F32)

    bg = seg(C_CONV, C_CONV + D_CONV)
    u = seg(C_CONV + D_CONV, C_CONV + 2 * D_CONV) * seg(C_CONV + 2 * D_CONV, C_CONV + 3 * D_CONV)
    s0 = state_scr[0:1, :]
    s1 = state_scr[1:2, :]
    ri = lax.broadcasted_iota(I32, u.shape, 0)
    um1 = jnp.where(ri == 0, s1, pltpu.roll(u, 1, 0))
    um2 = jnp.where(ri == 0, s0, jnp.where(ri == 1, s1, pltpu.roll(u, 2, 0)))
    cw = cw_ref[...]
    y = cw[0:1] * um2
    y = y + cw[1:2] * um1
    y = y + cw[2:3] * u
    ya_ref[0] = (bg * y).astype(BF16)
    tail = u[tm - 2:tm, :]
    state_scr[...] = tail
    tail_ref[0] = tail

    q_ref[0] = (seg(C_Q, C_Q + D_Q) * Q_SCALE).astype(BF16)
    put(k_ref, seg(C_KV, C_KV + D_KV))
    put(v_ref, seg(C_KV + D_KV, C_KV + 2 * D_KV))
    half = D_QI // 2
    qi_ref[0, :, 0:half] = seg(C_QI, C_QI + half).astype(BF16)
    qi_ref[0, :, half:D_QI] = seg(C_QI + half, C_QI + D_QI).astype(BF16)
    kw = seg(C_KIW, C_KIW + LANES)
    put(ki_ref, kw[:, 0:IDX_DIM])
    lane = lax.broadcasted_iota(I32, (1, LANES), 1)
    scale = jnp.where(lane < IDX_DIM, 1.0, jnp.where(lane < IDX_DIM + N_IDX_HEADS, WI_SCALE, 0.0))
    kiw_ref[0] = kw * scale

    for i in range(4):
        a = C_GATE + i * 512
        sg_ref[0, :, i * 512:(i + 1) * 512] = _sigmoid(seg(a, a + 512)).astype(BF16)


def _inproj(x, mod, g1, w_pack, conv_w, conv_state, tm, *, layer, depth, stacks):
    b, s, d = x.shape
    nt = s // tm
    row = lambda w, dt: jax.ShapeDtypeStruct((b, s, w), dt)
    stk = lambda w: jax.ShapeDtypeStruct((depth, b, s, w), F32)
    rspec = lambda w: pl.BlockSpec((1, tm, w), lambda bi, ti: (bi, ti, 0))
    if layer == 0:
        sspec = lambda w: pl.BlockSpec((depth, 1, tm, w), lambda bi, ti: (0, bi, ti, 0))
    else:
        sspec = lambda w: pl.BlockSpec((1, 1, tm, w), lambda bi, ti: (layer, bi, ti, 0))
    in_specs = [rspec(d),
                pl.BlockSpec((1, 6, 1, d), lambda bi, ti: (bi, 0, 0, 0)),
                _resident((1, d)),
                _resident((d, D_PACK)),
                _resident((CONV_W, D_CONV)),
                pl.BlockSpec((1, CONV_W - 1, D_CONV), lambda bi, ti: (bi, 0, 0))]
    args = [x, mod, g1, w_pack, conv_w, conv_state]
    aliases = {}
    if layer > 0:
        in_specs += [pl.BlockSpec(memory_space=pl.ANY)] * 3
        aliases = {len(args): 2, len(args) + 1: 3, len(args) + 2: 6}
        args += list(stacks)
    return pl.pallas_call(
        functools.partial(_inproj_kernel, tm=tm, layer=layer),
        grid=(b, nt),
        in_specs=in_specs,
        out_specs=[rspec(D_CONV), rspec(D_Q), sspec(D_KV), sspec(D_KV), rspec(D_QI),
                   rspec(LANES), sspec(IDX_DIM), rspec(2 * D_MODEL),
                   pl.BlockSpec((1, CONV_W - 1, D_CONV), lambda bi, ti: (bi, 0, 0))],
        out_shape=[row(D_CONV, BF16), row(D_Q, BF16), stk(D_KV), stk(D_KV),
                   row(D_QI, BF16), row(LANES, F32), stk(IDX_DIM), row(2 * D_MODEL, BF16),
                   jax.ShapeDtypeStruct((b, CONV_W - 1, D_CONV), F32)],
        scratch_shapes=[pltpu.VMEM((CONV_W - 1, D_CONV), F32)],
        input_output_aliases=aliases,
        compiler_params=_cparams(2), name="inproj",
    )(*args)


def _pair_blockdiag(tile, low_half):
    lane = lax.broadcasted_iota(I32, tile.shape, 1)
    lo = lane < HEAD_DIM
    swapped = pltpu.roll(tile, HEAD_DIM, 1)
    top = jnp.where(lo, tile if low_half else swapped, 0.0)
    bot = jnp.where(lo, 0.0, swapped if low_half else tile)
    return jnp.concatenate([top, bot], axis=0).astype(BF16)


def _attn_kernel(q_ref, qi_ref, kiw_ref, k_ref, v_ref, ki_ref, o_ref,
                 kbd_scr, vbd_scr, kibd_scr, key_scr, hi_scr, lo_scr, s_scr, m_scr, acc_scr,
                 *, tq, pos0, l_valid, topk):
    kb = KEY_BLOCK
    j = pl.program_id(1)
    nkc_all = key_scr.shape[0]
    npair = N_IDX_HEADS // 2
    nt = (((1,), (1,)), ((), ()))

    @pl.when(j == 0)
    def _():
        r_i = lax.broadcasted_iota(I32, (2 * kb, LANES), 0)
        c_i = lax.broadcasted_iota(I32, (2 * kb, LANES), 1)
        ones_bd = jnp.where((r_i < kb) == (c_i < HEAD_DIM), 1.0, 0.0).astype(BF16)

        def build(c, carry):
            rows = pl.ds(pl.multiple_of(c * kb, kb), kb)
            kibd_scr[c] = _pair_blockdiag(ki_ref[0, rows, :], True)
            for g in range(N_KV_HEADS):
                t = g // 2
                kbd_scr[g, c] = _pair_blockdiag(k_ref[0, 0, rows, t * LANES:(t + 1) * LANES], g % 2 == 0)
                vbd_scr[g, c, :, 0:LANES] = _pair_blockdiag(v_ref[0, 0, rows, t * LANES:(t + 1) * LANES],
                                                            g % 2 == 0)
                vbd_scr[g, c, :, LANES:2 * LANES] = ones_bd
            return carry

        lax.fori_loop(0, nkc_all, build, 0)

    w_t = kiw_ref[0].T

    row0 = pos0 + j * tq
    rows = row0 + lax.broadcasted_iota(I32, (1, tq), 1)
    lim = jnp.minimum(((rows >> 6) + 1) << 6, l_valid)
    last = row0 + tq - 1
    lim_max = jnp.minimum(((last >> 6) + 1) << 6, l_valid)
    nkc = (lim_max + kb - 1) >> 8

    def phase_a(c, carry):
        kbd = kibd_scr[c]
        acc = jnp.zeros((kb, tq), F32)
        for t in range(npair):
            d2 = lax.dot_general(kbd, qi_ref[0, :, t * LANES:(t + 1) * LANES], nt,
                                 preferred_element_type=F32)
            r = IDX_DIM + 2 * t
            acc = acc + w_t[r:r + 1, :] * jnp.maximum(d2[0:kb], 0.0)
            acc = acc + w_t[r + 1:r + 2, :] * jnp.maximum(d2[kb:2 * kb], 0.0)
        col = c * kb + lax.broadcasted_iota(I32, (kb, tq), 0)
        sc = jnp.where(col < lim, acc, -jnp.inf)
        bits = lax.bitcast_convert_type(sc, I32)
        key = bits ^ ((bits >> 31) & 0x7FFFFFFF)
        key_scr[c] = key
        hi_scr[c] = (key >> 16).astype(I16)
        lo_scr[c] = ((key & 0xFFFF) - 32768).astype(I16)
        return carry

    lax.fori_loop(0, nkc, phase_a, 0)

    def count_ge(cand):
        def body(c, cnt):
            m = jnp.where(key_scr[c] >= cand, 1.0, 0.0)
            parts = [m[8 * i:8 * (i + 1)] for i in range(kb // 8)]
            while len(parts) > 1:
                parts = [parts[i] + parts[i + 1] for i in range(0, len(parts), 2)]
            return cnt + parts[0]
        cnt = lax.fori_loop(0, nkc, body, jnp.zeros((8, tq), F32))
        return jnp.sum(cnt, axis=0, keepdims=True)

    def count16(ref, pred):
        def body(c, cnt):
            m = jnp.where(pred(ref[c]), jnp.int16(1), jnp.int16(0))
            parts = [m[16 * i:16 * (i + 1)] for i in range(kb // 16)]
            while len(parts) > 1:
                parts = [parts[i] + parts[i + 1] for i in range(0, len(parts), 2)]
            return cnt + parts[0]
        cnt = lax.fori_loop(0, nkc, body, jnp.zeros((16, tq), I16))
        return jnp.sum(cnt.astype(I32), axis=0, keepdims=True)

    def radix16(ref, need):
        t0 = jnp.where(count16(ref, lambda h: h >= jnp.int16(0)) >= need, 0, -32768).astype(I32)

        def bit_body(i, t):
            cand = t | lax.shift_left(jnp.int32(1), 14 - i)
            c16 = cand.astype(I16)
            return jnp.where(count16(ref, lambda h: h >= c16) >= need, cand, t)

        return lax.fori_loop(0, 15, bit_body, t0)

    t_hi = radix16(hi_scr, topk)
    h16 = t_hi.astype(I16)
    need_lo = topk - count16(hi_scr, lambda h: h > h16)

    def keep_bucket(c, carry):
        lo_scr[c] = jnp.where(hi_scr[c] == h16, lo_scr[c], jnp.int16(-32768))
        return carry

    lax.fori_loop(0, nkc, keep_bucket, 0)
    t_lo = radix16(lo_scr, need_lo)
    thr = (t_hi << 16) | ((t_lo + 32768) & 0xFFFF)
    kf = float(topk)
    few = lim <= topk
    thr = jnp.where(few, KEY_MIN_FINITE, thr)

    tie = jnp.logical_and(count_ge(thr) > kf, jnp.logical_not(few))
    any_tie = jnp.max(jnp.where(tie, 1.0, 0.0)) > 0.0

    @pl.when(any_tie)
    def _():
        need = kf - count_ge(thr + 1)
        r_i = lax.broadcasted_iota(I32, (kb, kb), 0)
        c_i = lax.broadcasted_iota(I32, (kb, kb), 1)
        tri = jnp.where(c_i < r_i, 1.0, 0.0).astype(BF16)

        def body(c, run):
            kc = key_scr[c]
            eqf = jnp.where(kc == thr, 1.0, 0.0)
            rank = run + jnp.dot(tri, eqf.astype(BF16), preferred_element_type=F32)
            drop = jnp.logical_and(jnp.logical_and(tie, kc == thr), rank >= need)
            key_scr[c] = jnp.where(drop, KEY_NEG_INF, kc)
            return run + jnp.sum(eqf, axis=0, keepdims=True)

        lax.fori_loop(0, nkc, body, jnp.zeros((1, tq), F32))

    nl = kb // LANES
    m_scr[...] = jnp.full(m_scr.shape, MASK_BIAS, F32)
    acc_scr[...] = jnp.zeros(acc_scr.shape, F32)

    def sweep1(c, carry):
        bias = jnp.where(key_scr[c] >= thr, 0.0, MASK_BIAS).T
        for g in range(N_KV_HEADS):
            s2 = lax.dot_general(q_ref[0, :, g * LANES:(g + 1) * LANES], kbd_scr[g, c], nt,
                                 preferred_element_type=F32)
            for r in range(GROUP):
                h = GROUP * g + r
                s = s2[:, r * kb:(r + 1) * kb] + bias
                s_scr[h, c] = s
                mx = m_scr[h]
                for i in range(nl):
                    mx = jnp.maximum(mx, s[:, i * LANES:(i + 1) * LANES])
                m_scr[h] = mx
        return carry

    lax.fori_loop(0, nkc, sweep1, 0)
    for h in range(N_HEADS):
        m_scr[h] = jnp.broadcast_to(jnp.max(m_scr[h], axis=1, keepdims=True), (tq, LANES))

    def sweep2(c, carry):
        for g in range(N_KV_HEADS):
            p2 = []
            for r in range(GROUP):
                h = GROUP * g + r
                s = s_scr[h, c]
                mb = m_scr[h]
                p2 += [jnp.exp2(s[:, i * LANES:(i + 1) * LANES] - mb) for i in range(nl)]
            pb = jnp.concatenate(p2, axis=1).astype(BF16)
            acc_scr[g] += jnp.dot(pb, vbd_scr[g, c], preferred_element_type=F32)
        return carry

    lax.fori_loop(0, nkc, sweep2, 0)
    for g in range(N_KV_HEADS):
        o_ref[0, :, g * LANES:(g + 1) * LANES] = (acc_scr[g, :, 0:LANES]
                                                  / acc_scr[g, :, LANES:2 * LANES]).astype(BF16)


def _attention(q, qi, kiw, k_all, v_all, ki_all, *, layer, tq, pos0, l_valid, topk):
    b, s, _ = q.shape
    l_pad = k_all.shape[2]
    nkc_max = l_pad // KEY_BLOCK
    qspec = lambda w: pl.BlockSpec((1, tq, w), lambda bi, ji: (bi, ji, 0))
    kspec = lambda w: pl.BlockSpec((1, l_pad, w), lambda bi, ji: (bi, 0, 0))
    lspec = lambda w: pl.BlockSpec((1, 1, l_pad, w), lambda bi, ji: (layer, bi, 0, 0))
    return pl.pallas_call(
        functools.partial(_attn_kernel, tq=tq, pos0=pos0, l_valid=l_valid, topk=topk),
        grid=(b, s // tq),
        in_specs=[qspec(D_Q), qspec(D_QI), qspec(LANES), lspec(D_KV), lspec(D_KV), kspec(LANES)],
        out_specs=qspec(D_Q),
        out_shape=jax.ShapeDtypeStruct((b, s, D_Q), BF16),
        scratch_shapes=[pltpu.VMEM((N_KV_HEADS, nkc_max, 2 * KEY_BLOCK, LANES), BF16),
                        pltpu.VMEM((N_KV_HEADS, nkc_max, 2 * KEY_BLOCK, 2 * LANES), BF16),
                        pltpu.VMEM((nkc_max, 2 * KEY_BLOCK, LANES), BF16),
                        pltpu.VMEM((nkc_max, KEY_BLOCK, tq), I32),
                        pltpu.VMEM((nkc_max, KEY_BLOCK, tq), I16),
                        pltpu.VMEM((nkc_max, KEY_BLOCK, tq), I16),
                        pltpu.VMEM((N_HEADS, nkc_max, tq, KEY_BLOCK), F32),
                        pltpu.VMEM((N_HEADS, tq, LANES), F32),
                        pltpu.VMEM((N_KV_HEADS, tq, 2 * LANES), F32)],
        compiler_params=_cparams(2), name="attention",
    )(q, qi, kiw, k_all, v_all, ki_all)


def _split3(a):
    hi = a.astype(BF16)
    r = a - hi.astype(F32)
    mid = r.astype(BF16)
    lo = (r - mid.astype(F32)).astype(BF16)
    return hi, mid, lo


def _merge_kernel(*refs, with_router):
    if with_router:
        (x_ref, ya_ref, ao_ref, sg_ref, mod_ref, wc_ref, wa_ref, wo_ref, g2_ref, rw_ref, rb_ref,
         x1_ref, h2_ref, lg_ref) = refs
    else:
        (x_ref, ya_ref, ao_ref, sg_ref, mod_ref, wc_ref, wa_ref, wo_ref, g2_ref,
         x1_ref, h2_ref) = refs
    a = jnp.dot(ya_ref[0], wc_ref[...], preferred_element_type=F32)
    b = jnp.dot(ao_ref[0], wa_ref[...], preferred_element_type=F32)
    merged = sg_ref[0, :, 0:D_MODEL].astype(F32) * a + sg_ref[0, :, D_MODEL:2 * D_MODEL].astype(F32) * b
    mix = jnp.dot(merged.astype(BF16), wo_ref[...], preferred_element_type=F32)
    x1 = x_ref[0] + mod_ref[0, 2] * mix
    x1_ref[0] = x1
    h2 = _rmsnorm(x1, g2_ref[...]) * (1.0 + mod_ref[0, 4]) + mod_ref[0, 3]
    h2_ref[0] = h2.astype(h2_ref.dtype)
    if with_router:
        hs = _split3(h2)
        acc = jnp.zeros((h2.shape[0], LANES), F32)
        for i, jj in ((0, 0), (0, 1), (1, 0)):
            acc = acc + jnp.dot(hs[i], rw_ref[jj], preferred_element_type=F32)
        lane = lax.broadcasted_iota(I32, acc.shape, 1)
        lg = jnp.where(lane < N_EXPERTS, acc + rb_ref[...], -jnp.inf)
        m1 = jnp.max(lg, axis=1, keepdims=True)
        i1 = jnp.min(jnp.where(lg == m1, lane, LANES), axis=1, keepdims=True)
        rest = jnp.where(lane == i1, -jnp.inf, lg)
        m2 = jnp.max(rest, axis=1, keepdims=True)
        i2 = jnp.min(jnp.where(rest == m2, lane, LANES), axis=1, keepdims=True)
        e2 = jnp.exp(m2 - m1)
        den = 1.0 + e2
        lg_ref[0] = jnp.where(lane == 0, i1.astype(F32),
                              jnp.where(lane == 1, i2.astype(F32),
                                        jnp.where(lane == 2, 1.0 / den,
                                                  jnp.where(lane == 3, e2 / den, 0.0))))


def _merge(x, ya, ao, sg, mod, wc, wa, wo, g2, router=None, *, tm):
    b, s, d = x.shape
    r = mod.shape[2]
    rspec = lambda w: pl.BlockSpec((1, tm, w), lambda bi, ti: (bi, ti, 0))
    mspec = (pl.BlockSpec((1, 6, 1, d), lambda bi, ti: (bi, 0, 0, 0)) if r == 1 else
             pl.BlockSpec((1, 6, tm, d), lambda bi, ti: (bi, 0, ti, 0)))
    in_specs = [rspec(d), rspec(D_CONV), rspec(D_Q), rspec(2 * d), mspec,
                _resident((D_CONV, d)), _resident((D_Q, d)), _resident((d, d)), _resident((1, d))]
    args = [x, ya, ao, sg, mod, wc, wa, wo, g2]
    out_specs = [rspec(d), rspec(d)]
    out_shape = [jax.ShapeDtypeStruct((b, s, d), F32),
                 jax.ShapeDtypeStruct((b, s, d), BF16 if router is None else F32)]
    if router is not None:
        in_specs += [_resident((3, d, LANES)), _resident((1, LANES))]
        args += list(router)
        out_specs.append(rspec(LANES))
        out_shape.append(jax.ShapeDtypeStruct((b, s, LANES), F32))
    return pl.pallas_call(
        functools.partial(_merge_kernel, with_router=router is not None),
        grid=(b, s // tm),
        in_specs=in_specs, out_specs=out_specs, out_shape=out_shape,
        compiler_params=_cparams(2), name="merge",
    )(*args)


def _ffn_kernel(h2_ref, x1_ref, mod_ref, w1_ref, w3_ref, w2_ref, o_ref, acc_scr, *, nc):
    h = h2_ref[0]
    acc_scr[...] = _swiglu_chunk(h, w1_ref[0], w3_ref[0], w2_ref[0])

    def body(c, carry):
        acc_scr[...] += _swiglu_chunk(h, w1_ref[c], w3_ref[c], w2_ref[c])
        return carry

    lax.fori_loop(1, nc, body, 0)
    o_ref[0] = x1_ref[0] + mod_ref[0, 5] * acc_scr[...]


def _ffn(h2, x1, mod, w1, w3, w2, *, tm):
    b, s, d = x1.shape
    nc = w1.shape[0]
    r = mod.shape[2]
    rspec = lambda w: pl.BlockSpec((1, tm, w), lambda bi, ti: (bi, ti, 0))
    mspec = (pl.BlockSpec((1, 6, 1, d), lambda bi, ti: (bi, 0, 0, 0)) if r == 1 else
             pl.BlockSpec((1, 6, tm, d), lambda bi, ti: (bi, 0, ti, 0)))
    return pl.pallas_call(
        functools.partial(_ffn_kernel, nc=nc),
        grid=(b, s // tm),
        in_specs=[rspec(d), rspec(d), mspec, _resident(w1.shape), _resident(w3.shape), _resident(w2.shape)],
        out_specs=rspec(d),
        out_shape=jax.ShapeDtypeStruct((b, s, d), F32),
        scratch_shapes=[pltpu.VMEM((tm, d), F32)],
        compiler_params=_cparams(2), name="ffn",
    )(h2, x1, mod, w1, w3, w2)


def _swiglu_chunk(xb, w1, w3, w2):
    a = jnp.dot(xb, w1, preferred_element_type=F32)
    hm = (a * _sigmoid(a) * jnp.dot(xb, w3, preferred_element_type=F32)).astype(BF16)
    return jnp.dot(hm, w2, preferred_element_type=F32)


def _moe_dense_kernel(h2_ref, x1_ref, rt_ref, mod_ref, w1_ref, w3_ref, w2_ref, fg_ref, o_ref,
                      eacc_scr, acc_scr, *, final_norm):
    e = pl.program_id(2)
    c = pl.program_id(3)
    ne = pl.num_programs(2)
    nc = pl.num_programs(3)

    @pl.when(jnp.logical_and(e == 0, c == 0))
    def _():
        acc_scr[...] = jnp.zeros(acc_scr.shape, F32)

    @pl.when(c == 0)
    def _():
        eacc_scr[...] = jnp.zeros(eacc_scr.shape, F32)

    eacc_scr[...] += _swiglu_chunk(h2_ref[0].astype(BF16), w1_ref[0], w3_ref[0], w2_ref[0])

    @pl.when(c == nc - 1)
    def _():
        rt = rt_ref[0]
        ef = e.astype(F32)
        gate = (jnp.where(rt[:, 0:1] == ef, rt[:, 2:3], 0.0)
                + jnp.where(rt[:, 1:2] == ef, rt[:, 3:4], 0.0))
        acc_scr[...] += gate * eacc_scr[...]

    @pl.when(jnp.logical_and(e == ne - 1, c == nc - 1))
    def _():
        x2 = x1_ref[0] + mod_ref[0, 5] * acc_scr[...]
        o_ref[0] = _rmsnorm(x2, fg_ref[...]) if final_norm else x2


def _moe_dense(h2, x1, route, mod, w1, w3, w2, fg, *, tm, final_norm):
    b, s, d = x1.shape
    ne, _, f = w1.shape
    fc = MOE_CHUNK
    r = mod.shape[2]
    rspec = lambda w: pl.BlockSpec((1, tm, w), lambda bi, ti, ei, ci: (bi, ti, 0))
    mspec = (pl.BlockSpec((1, 6, 1, d), lambda bi, ti, ei, ci: (bi, 0, 0, 0)) if r == 1 else
             pl.BlockSpec((1, 6, tm, d), lambda bi, ti, ei, ci: (bi, 0, ti, 0)))
    wcol = pl.BlockSpec((1, d, fc), lambda bi, ti, ei, ci: (ei, 0, ci))
    return pl.pallas_call(
        functools.partial(_moe_dense_kernel, final_norm=final_norm),
        grid=(b, s // tm, ne, f // fc),
        in_specs=[rspec(d), rspec(d), rspec(LANES), mspec, wcol, wcol,
                  pl.BlockSpec((1, fc, d), lambda bi, ti, ei, ci: (ei, ci, 0)),
                  _resident((1, d))],
        out_specs=rspec(d),
        out_shape=jax.ShapeDtypeStruct((b, s, d), F32),
        scratch_shapes=[pltpu.VMEM((tm, d), F32), pltpu.VMEM((tm, d), F32)],
        compiler_params=_cparams(4), name="moe_dense",
    )(h2, x1, route, mod, w1, w3, w2, fg)


def _route_tables(route, tmr):
    n = route.shape[0]
    e1 = route[:, 0].astype(I32)
    e2 = route[:, 1].astype(I32)
    ar = jnp.arange(N_EXPERTS, dtype=I32)
    oh1 = e1[:, None] == ar
    oh2 = e2[:, None] == ar
    oh = oh1.astype(I32) + oh2.astype(I32)
    cum = jnp.cumsum(oh, axis=0)
    excl = cum - oh
    cnt = cum[-1]
    tiles_e = (cnt + tmr - 1) // tmr
    tile_end = jnp.cumsum(tiles_e)
    tile_start = tile_end - tiles_e
    row_off = tile_start * tmr
    d1 = row_off[e1] + jnp.sum(jnp.where(oh1, excl, 0), axis=1)
    d2 = row_off[e2] + jnp.sum(jnp.where(oh2, excl, 0), axis=1)
    nt = -(-2 * n // tmr) + N_EXPERTS
    tok = jnp.arange(n, dtype=I32)
    pair = jnp.zeros((nt * tmr,), I32).at[jnp.concatenate([d1, d2])].set(
        jnp.concatenate([2 * tok, 2 * tok + 1]), unique_indices=True)
    src = pair >> 1
    dst = (pair & 1) * n + src
    tile = jnp.arange(nt, dtype=I32)
    te = jnp.minimum(jnp.searchsorted(tile_end, tile, side="right"), N_EXPERTS - 1).astype(I32)
    nv = jnp.where(tile < tile_end[-1], jnp.clip(cnt[te] - (tile - tile_start[te]) * tmr, 0, tmr), 0)
    return (src.reshape(nt, 1, tmr), dst.reshape(nt, 1, tmr), te, nv.astype(I32),
            tile_end[-1:].astype(I32))


def _moe_routed_kernel(te_ref, nv_ref, nu_ref, src0_ref, srcn_ref, dstp_ref, h2_hbm, w1_ref, w3_ref, w2_ref,
                       out_hbm, xbuf, xb_scr, ybuf, gsem, ssem, *, tmr, nc, gs):
    i = pl.program_id(0)
    c = pl.program_id(1)
    nu = nu_ref[0]
    slot = i % 2
    other = 1 - slot
    rpc = tmr // nc
    rpg = tmr // gs

    def gather(src_ref, cc, k, sl):
        return pltpu.make_async_copy(h2_hbm.at[pl.ds(src_ref[0, 0, cc * rpg + k], 1)],
                                     xbuf.at[sl, cc, pl.ds(k, 1)], gsem.at[sl])

    def scatter(cc, k, sl):
        return pltpu.make_async_copy(ybuf.at[sl, cc, pl.ds(k, 1)],
                                     out_hbm.at[pl.ds(dstp_ref[0, 0, cc * rpc + k], 1)], ssem.at[sl])

    @pl.when(jnp.logical_and(i == 0, c == 0))
    def _():
        def body(cc, carry):
            for k in range(rpg):
                gather(src0_ref, cc, k, 0).start()
            return carry
        lax.fori_loop(0, gs, body, 0)

    @pl.when(jnp.logical_and(c == 0, i < nu))
    def _():
        pltpu.make_async_copy(xbuf.at[slot], xbuf.at[slot], gsem.at[slot]).wait()
        xb_scr[...] = xbuf[slot].reshape(tmr, xb_scr.shape[1]).astype(BF16)

    last = nv_ref.shape[0] - 1
    nv_drain = nv_ref[jnp.clip(i - 2, 0, last)]
    push = jnp.logical_and(i >= 1, i - 1 < nu)
    nv_push = nv_ref[jnp.clip(i - 1, 0, last)]
    fast = jnp.logical_and(i >= 1, jnp.logical_and(i + 1 < nu, nv_push == tmr))
    slow = jnp.logical_not(fast)
    drain = jnp.logical_and(c == 0, jnp.logical_and(i >= 2, i - 2 < nu))

    @pl.when(jnp.logical_and(drain, nv_drain == tmr))
    def _():
        pltpu.make_async_copy(ybuf.at[slot], ybuf.at[slot], ssem.at[slot]).wait()

    @pl.when(jnp.logical_and(drain, nv_drain < tmr))
    def _():
        def body(r, carry):
            scatter(0, 0, slot).wait()
            return carry
        lax.fori_loop(0, nv_drain, body, 0)

    @pl.when(fast)
    def _():
        xb = xb_scr[...]
        a = jnp.dot(xb, w1_ref[0], preferred_element_type=F32)
        b = jnp.dot(xb, w3_ref[0], preferred_element_type=F32)
        for k in range(rpg):
            gather(srcn_ref, c, k, other).start()
        for k in range(rpc):
            scatter(c, k, other).start(priority=k % 2)
        hm = (a * _sigmoid(a) * b).astype(BF16)
        y = jnp.dot(hm, w2_ref[0], preferred_element_type=F32).reshape(nc, rpc, xb_scr.shape[1])
        ybuf[slot] = jnp.where(c == 0, y, ybuf[slot] + y)

    @pl.when(jnp.logical_and(slow, jnp.logical_and(i + 1 < nu, c < gs)))
    def _():
        for k in range(rpg):
            gather(srcn_ref, c, k, other).start()

    @pl.when(jnp.logical_and(slow, jnp.logical_and(push, nv_push == tmr)))
    def _():
        for k in range(rpc):
            scatter(c, k, other).start()

    @pl.when(jnp.logical_and(slow, jnp.logical_and(push, nv_push < tmr)))
    def _():
        def body(k, carry):
            scatter(c, k, other).start()
            return carry
        lax.fori_loop(0, jnp.clip(nv_push - c * rpc, 0, rpc), body, 0)

    @pl.when(jnp.logical_and(slow, i < nu))
    def _():
        y = _swiglu_chunk(xb_scr[...], w1_ref[0], w3_ref[0], w2_ref[0]).reshape(nc, rpc, xb_scr.shape[1])

        @pl.when(c == 0)
        def _():
            ybuf[slot] = y

        @pl.when(c > 0)
        def _():
            ybuf[slot] += y


def _moe_routed(h2, route, w1, w3, w2, *, tmr):
    n, d = h2.shape
    f = w1.shape[2]
    fc = MOE_CHUNK
    nc = f // fc
    gs = MOE_GATHER_STEPS
    src, dst, te, nv, nu = _route_tables(route, tmr)
    nt = src.shape[0]
    last = nt - 1
    smem = lambda imap: pl.BlockSpec((1, 1, tmr), imap, memory_space=pltpu.SMEM)
    chunk = lambda i, c, nu_r: jnp.where(i < nu_r[0], c, nc - 1)
    wcol = pl.BlockSpec((1, d, fc), lambda i, c, te_r, nv_r, nu_r: (te_r[jnp.minimum(i, last)], 0, chunk(i, c, nu_r)))
    wrow = pl.BlockSpec((1, fc, d), lambda i, c, te_r, nv_r, nu_r: (te_r[jnp.minimum(i, last)], chunk(i, c, nu_r), 0))
    return pl.pallas_call(
        functools.partial(_moe_routed_kernel, tmr=tmr, nc=nc, gs=gs),
        grid_spec=pltpu.PrefetchScalarGridSpec(
            num_scalar_prefetch=3,
            grid=(nt + 2, nc),
            in_specs=[smem(lambda i, c, te_r, nv_r, nu_r: (0, 0, 0)),
                      smem(lambda i, c, te_r, nv_r, nu_r: (jnp.minimum(i + 1, last), 0, 0)),
                      smem(lambda i, c, te_r, nv_r, nu_r: (jnp.clip(i - 1, 0, last), 0, 0)),
                      pl.BlockSpec(memory_space=pl.ANY), wcol, wcol, wrow],
            out_specs=pl.BlockSpec(memory_space=pl.ANY),
            scratch_shapes=[pltpu.VMEM((2, gs, tmr // gs, d), F32), pltpu.VMEM((tmr, d), BF16),
                            pltpu.VMEM((2, nc, tmr // nc, d), F32),
                            pltpu.SemaphoreType.DMA((2,)), pltpu.SemaphoreType.DMA((2,))]),
        out_shape=jax.ShapeDtypeStruct((2 * n, d), F32),
        compiler_params=_cparams(2), name="moe_routed",
    )(te, nv, nu, src, src, dst, h2, w1, w3, w2)


def _moe_combine_kernel(x1_ref, ya_ref, yb_ref, rt_ref, mod_ref, fg_ref, o_ref, *, final_norm):
    rt = rt_ref[0]
    f = rt[:, 2:3] * ya_ref[...] + rt[:, 3:4] * yb_ref[...]
    x2 = x1_ref[0] + mod_ref[0, 5] * f
    o_ref[0] = _rmsnorm(x2, fg_ref[...]) if final_norm else x2


def _moe_combine(x1, y2, route, mod, fg, *, tm, final_norm):
    b, s, d = x1.shape
    nt = s // tm
    rspec = lambda w: pl.BlockSpec((1, tm, w), lambda bi, ti: (bi, ti, 0))
    return pl.pallas_call(
        functools.partial(_moe_combine_kernel, final_norm=final_norm),
        grid=(b, nt),
        in_specs=[rspec(d),
                  pl.BlockSpec((tm, d), lambda bi, ti: (bi * nt + ti, 0)),
                  pl.BlockSpec((tm, d), lambda bi, ti: (b * nt + bi * nt + ti, 0)),
                  rspec(LANES),
                  pl.BlockSpec((1, 6, 1, d), lambda bi, ti: (bi, 0, 0, 0)), _resident((1, d))],
        out_specs=rspec(d),
        out_shape=jax.ShapeDtypeStruct((b, s, d), F32),
        compiler_params=_cparams(2), name="moe_combine",
    )(x1, y2, y2, route, mod, fg)


def _pack_w_in(w):
    cut = C_KIW + IDX_DIM + N_IDX_HEADS
    pad = jnp.zeros((w.shape[0], C_GATE - cut), w.dtype)
    return jnp.concatenate([w[:, :cut], pad, w[:, cut:]], axis=1).astype(BF16)


def _chunk_cols(w, fc):
    *lead, d, f = w.shape
    return jnp.moveaxis(w.astype(BF16).reshape(*lead, d, f // fc, fc), -2, -3)


def _chunk_w2(w2, fc):
    *lead, f, d = w2.shape
    return w2.reshape(*lead, f // fc, fc, d).astype(BF16)


def _router_terms(rw, rb):
    d = rw.shape[0]
    rwp = jnp.zeros((d, LANES), F32).at[:, :N_EXPERTS].set(rw)
    hi, mid, lo = _split3(rwp)
    rbp = jnp.zeros((1, LANES), F32).at[0, :N_EXPERTS].set(rb)
    return jnp.stack([hi, mid, lo]), rbp


def _trunk(x, mod_all, caches, p, *, tm, tq):
    b, t, d = x.shape
    depth = mod_all.shape[0]
    flat = caches is not None
    stacks, tails = None, []
    for l in range(depth):
        mod = mod_all[l].reshape(b, 6, 1, d)
        if caches is None:
            conv_state = jnp.zeros((b, CONV_W - 1, D_CONV), F32)
        else:
            conv_state = caches[3][l]
        ya, q, k_st, v_st, qi, kiw, ki_st, sg, tail = _inproj(
            x, mod, p["norm1_g"][l], p["w_in"][l], p["conv_w"][l], conv_state, tm,
            layer=l, depth=depth, stacks=stacks)
        stacks = (k_st, v_st, ki_st)
        if caches is None:
            ao = _attention(q, qi, kiw, k_st, v_st, kiw, layer=l, tq=tq, pos0=0, l_valid=t,
                            topk=min(TOPK_MAX, t // 4))
        else:
            past = caches[0][l].shape[1]
            l_valid = past + t
            l_pad = -(-l_valid // KEY_BLOCK) * KEY_BLOCK
            cat = lambda old, new: jnp.concatenate(
                [old.reshape(b, past, -1), new, jnp.zeros((b, l_pad - l_valid, new.shape[-1]), F32)], axis=1)
            k_all, v_all = cat(caches[0][l], k_st[l])[None], cat(caches[1][l], v_st[l])[None]
            ki_all = jnp.pad(cat(caches[2][l], ki_st[l]), ((0, 0), (0, 0), (0, LANES - IDX_DIM)))
            padq = lambda a: jnp.pad(a, ((0, 0), (0, tq - t), (0, 0)))
            ao = _attention(padq(q), padq(qi), padq(kiw), k_all, v_all, ki_all, layer=0, tq=tq, pos0=past,
                            l_valid=l_valid, topk=min(TOPK_MAX, l_valid // 4))[:, :t]
        is_moe = l % 2 == 1
        router = (p["router_w"][l // 2], p["router_b"][l // 2]) if is_moe else None
        if flat:
            n = b * t
            fl = lambda a: a.reshape(1, n, a.shape[-1])
            modf = jnp.repeat(mod_all[l].reshape(b, 6, d), t, axis=0).transpose(1, 0, 2)[None]
            xs, yas, aos, sgs, tmf = fl(x), fl(ya), fl(ao), fl(sg), n
        else:
            modf, xs, yas, aos, sgs, tmf = mod, x, ya, ao, sg, tm
        outs = _merge(xs, yas, aos, sgs, modf, p["w_conv_out"][l], p["w_attn_out"][l], p["w_o"][l],
                      p["norm2_g"][l], router, tm=tmf if flat else min(2 * tm, t))
        if is_moe and flat:
            x1, h2, route = outs
            x = _moe_dense(h2, x1, route, modf, p["moe_w1"][l // 2], p["moe_w3"][l // 2],
                           p["moe_w2"][l // 2], p["final_g"], tm=xs.shape[1], final_norm=l == depth - 1)
        elif is_moe:
            x1, h2, route = outs
            y2 = _moe_routed(h2.reshape(b * t, d), route.reshape(b * t, LANES), p["moe_w1"][l // 2],
                             p["moe_w3"][l // 2], p["moe_w2"][l // 2], tmr=MOE_ROWS)
            x = _moe_combine(x1, y2, route, mod, p["final_g"], tm=tm,
                             final_norm=l == depth - 1)
        else:
            x1, h2 = outs
            x = _ffn(h2, x1, modf, p["ffn_w1"][l // 2], p["ffn_w3"][l // 2], p["ffn_w2"][l // 2],
                     tm=tmf if flat else min(2 * tm, t))
        x = x.reshape(b, t, d)
        tails.append(tail)
    k_st, v_st, ki_st = stacks
    heads = (depth, b, t, N_KV_HEADS, HEAD_DIM)
    return x, k_st.reshape(heads), v_st.reshape(heads), ki_st, jnp.stack(tails)


def kernel(x_prompt, x_sample, c_prompt, c_sample, cache_k, cache_v, cache_idx_k, state_conv, w_ada, b_ada, norm1_g, w_in, conv_w, w_conv_out, w_attn_out, w_o, norm2_g, ffn_w1, ffn_w3, ffn_w2, router_w, router_b, moe_w1, moe_w3, moe_w2, final_g):
    depth = w_in.shape[0]
    bp = x_prompt.shape[0]
    rterms = [_router_terms(router_w[i], router_b[i]) for i in range(router_w.shape[0])]
    p = {
        "norm1_g": norm1_g.reshape(depth, 1, -1),
        "norm2_g": norm2_g.reshape(depth, 1, -1),
        "final_g": final_g.reshape(1, -1),
        "w_in": jnp.stack([_pack_w_in(w_in[l]) for l in range(depth)]),
        "conv_w": conv_w,
        "w_conv_out": w_conv_out.astype(BF16),
        "w_attn_out": w_attn_out.astype(BF16),
        "w_o": w_o.astype(BF16),
        "ffn_w1": _chunk_cols(ffn_w1, FFN_CHUNK),
        "ffn_w3": _chunk_cols(ffn_w3, FFN_CHUNK),
        "ffn_w2": _chunk_w2(ffn_w2, FFN_CHUNK),
        "moe_w1": moe_w1.astype(BF16),
        "moe_w3": moe_w3.astype(BF16),
        "moe_w2": moe_w2.astype(BF16),
        "router_w": [r[0] for r in rterms],
        "router_b": [r[1] for r in rterms],
    }
    mod_all = _adaln(jnp.concatenate([c_prompt, c_sample], axis=0), w_ada.astype(BF16), b_ada)
    seq = x_prompt.shape[1]
    tm = min(512, seq)
    tq = min(256, seq)
    yp, kp, vp, kip, cp = _trunk(x_prompt, mod_all[:, :bp], None, p, tm=tm, tq=tq)
    ts = x_sample.shape[1]
    ys, ksm, vsm, kism, csm = _trunk(x_sample, mod_all[:, bp:], (cache_k, cache_v, cache_idx_k, state_conv), p,
                                     tm=ts, tq=-(-ts // LANES) * LANES)
    return (yp, ys, kp, vp, kip, cp, ksm, vsm, kism, csm)
```
